```python
import math
import jax, jax.numpy as jnp
from jax import lax
import numpy as np

D_MODEL = 1024
BATCH = 32
SEQ = 2048
DEPTH = 2

GRID_W = 64
CTX_LEN = 256
MIX_W = D_MODEL
D_FF = 4 * D_MODEL
N_EVEN = (DEPTH + 1) // 2
N_ODD = DEPTH // 2
DEEPNORM_ALPHA = (2.0 * DEPTH) ** 0.25
DEEPNORM_BETA = (8.0 * DEPTH) ** -0.25
LN_EPS = 1e-5
N_MOD = 6

HY_W = MIX_W // 2
HY_SHORT = 3
HY_BANDS = 16
HY_PE_DIM = 2 * HY_BANDS + 1
HY_FFN = 64
HY_DECAY_MIN = -math.log(1e-2) / 1.5
HY_DECAY_MAX = -math.log(1e-2) / 0.3

HEAD_DIM = 64
SWA_HEADS = (MIX_W - HY_W) // HEAD_DIM
SWA_KV_HEADS = 2
SWA_GROUP = SWA_HEADS // SWA_KV_HEADS
WINDOW = 128
SWA_BLOCK = 128
ROPE_BASE = 10000.0

E_HY_COLS = 3 * HY_W
E_Q_COLS = SWA_HEADS * HEAD_DIM
E_KV_COLS = SWA_KV_HEADS * HEAD_DIM
E_IN_COLS = E_HY_COLS + E_Q_COLS + 2 * E_KV_COLS

RW_HEAD = 64
RW_W = MIX_W // 2
RW_HEADS = RW_W // RW_HEAD
RW_DECAY_LORA = 64
RW_AAA_LORA = 64
RW_GATE_LORA = 128
RW_GN_EPS = 64e-5
RW_COLS = 3 * RW_W + 2 * RW_DECAY_LORA + 2 * RW_AAA_LORA + RW_GATE_LORA

DN_HEAD = 128
DN_W = MIX_W - RW_W
DN_HEADS = DN_W // DN_HEAD
DN_SHORT = 3
DN_CHUNK = 64
DN_COLS = 4 * DN_W + 4 * DN_HEADS
O_IN_COLS = RW_COLS + DN_COLS

kernel_name = "hybrid_hyena_swa_rwkv7_gdn_dit_block"

F32 = jnp.float32


def layer_norm(x, g, b):
    xf = x.astype(F32)
    mu = jnp.mean(xf, -1, keepdims=True)
    var = jnp.mean(jnp.square(xf - mu), -1, keepdims=True)
    return ((xf - mu) * lax.rsqrt(var + LN_EPS) * g + b).astype(x.dtype)


def l2_normalize(t, eps=1e-6):
    tf = t.astype(F32)
    return tf * lax.rsqrt(jnp.sum(tf * tf, -1, keepdims=True) + eps)


def centred_conv(u, w):
    width = w.shape[0]
    pad = width // 2
    L = u.shape[1]
    up = jnp.pad(u, ((0, 0), (pad, pad), (0, 0)))
    out = up[:, 0:L] * w[0]
    for j in range(1, width):
        out = out + up[:, j:j + L] * w[j]
    return out


def sq_relu_mlp(h, w1, w2):
    return jnp.square(jax.nn.relu(h @ w1)) @ w2


def hyena_filters(L, w1, b1, w2, b2, freq, w3, decay):
    t = jnp.arange(L, dtype=F32)
    t_norm = t / max(L - 1, 1)
    bands = jnp.linspace(1e-4, HY_BANDS - 1, HY_BANDS, dtype=F32)
    ang = 2.0 * math.pi * t[:, None] * bands[None, :] / L
    pe = jnp.concatenate([t_norm[:, None], jnp.cos(ang), -jnp.sin(ang)], axis=-1)
    h = jnp.sin(freq * (pe @ w1 + b1))
    h = jnp.sin(freq * (h @ w2 + b2))
    h = (h @ w3) * jnp.exp(-t_norm[:, None] * jnp.abs(decay))
    return h[:, :HY_W], h[:, HY_W:]


def bidir_long_conv(u, h_fwd, h_bwd, bias):
    L = u.shape[1]
    k = jnp.concatenate([h_fwd, jnp.zeros_like(h_fwd[:1]), h_bwd[:0:-1]], axis=0)
    uf = jnp.fft.rfft(u.astype(F32), n=2 * L, axis=1)
    kf = jnp.fft.rfft(k.astype(F32), n=2 * L, axis=0)
    y = jnp.fft.irfft(uf * kf[None], n=2 * L, axis=1)[:, :L]
    return (y + u.astype(F32) * bias).astype(u.dtype)


def hyena_mixer(p, conv_w, w1, b1, w2, b2, freq, w3, decay, bias):
    z = centred_conv(p, conv_w)
    x0, x1, v = jnp.split(z, 3, axis=-1)
    h_f, h_b = hyena_filters(p.shape[1], w1, b1, w2, b2, freq, w3, decay)
    return x0 * bidir_long_conv(x1 * v, h_f, h_b, bias)


def axial_rope(t):
    L = t.shape[1]
    pos = jnp.arange(L, dtype=jnp.int32)
    half = HEAD_DIM // 2
    quarter = half // 2
    inv_freq = ROPE_BASE ** (-jnp.arange(quarter, dtype=F32) / quarter)
    tf = t.astype(F32)

    def rot(u, p):
        ang = p.astype(F32)[:, None] * inv_freq[None, :]
        cos = jnp.cos(ang)[None, :, None, :]
        sin = jnp.sin(ang)[None, :, None, :]
        u1, u2 = u[..., :quarter], u[..., quarter:]
        return jnp.concatenate([u1 * cos - u2 * sin, u1 * sin + u2 * cos], axis=-1)

    out = jnp.concatenate([rot(tf[..., :half], pos // GRID_W), rot(tf[..., half:], pos % GRID_W)], axis=-1)
    return out.astype(t.dtype)


def sink_softmax(s, sink):
    sk = sink.astype(F32)[None, :, :, None, None]
    m = jnp.maximum(jnp.max(s, -1, keepdims=True), sk)
    p = jnp.exp(s - m)
    return p / (jnp.sum(p, -1, keepdims=True) + jnp.exp(sk - m))


def windowed_attention_latent(q, k, v, kc, vc, sink):
    bsz, L = q.shape[0], q.shape[1]
    n_blk = L // SWA_BLOCK
    span = SWA_BLOCK + 2 * WINDOW
    scale = HEAD_DIM ** -0.5
    kp = jnp.pad(k, ((0, 0), (WINDOW, WINDOW), (0, 0), (0, 0)))
    vp = jnp.pad(v, ((0, 0), (WINDOW, WINDOW), (0, 0), (0, 0)))

    def one_block(b):
        start = b * SWA_BLOCK
        qb = lax.dynamic_slice_in_dim(q, start, SWA_BLOCK, axis=1)
        kb = lax.dynamic_slice_in_dim(kp, start, span, axis=1)
        vb = lax.dynamic_slice_in_dim(vp, start, span, axis=1)
        qpos = start + jnp.arange(SWA_BLOCK)
        kpos = start - WINDOW + jnp.arange(span)
        ok = (jnp.abs(qpos[:, None] - kpos[None, :]) <= WINDOW) & (kpos[None, :] >= 0) & (kpos[None, :] < L)
        s_loc = jnp.einsum('bqhgd,bkhd->bhgqk', qb, kb).astype(F32) * scale
        s_loc = jnp.where(ok, s_loc, -jnp.inf)
        s_ctx = jnp.einsum('bqhgd,bchd->bhgqc', qb, kc).astype(F32) * scale
        p = sink_softmax(jnp.concatenate([s_loc, s_ctx], axis=-1), sink).astype(v.dtype)
        return (jnp.einsum('bhgqk,bkhd->bqhgd', p[..., :span], vb)
                + jnp.einsum('bhgqc,bchd->bqhgd', p[..., span:], vc))

    o = lax.map(one_block, jnp.arange(n_blk))
    return jnp.moveaxis(o, 0, 1).reshape(bsz, L, SWA_HEADS * HEAD_DIM)


def context_attention(qc, kc, vc, sink):
    bsz, Lc = qc.shape[0], qc.shape[1]
    s = jnp.einsum('bqhgd,bchd->bhgqc', qc, kc).astype(F32) * (HEAD_DIM ** -0.5)
    p = sink_softmax(s, sink).astype(vc.dtype)
    return jnp.einsum('bhgqc,bchd->bqhgd', p, vc).reshape(bsz, Lc, SWA_HEADS * HEAD_DIM)


def split_swa(p):
    bsz, L = p.shape[0], p.shape[1]
    o1 = E_HY_COLS
    o2 = o1 + E_Q_COLS
    o3 = o2 + E_KV_COLS
    q = p[..., o1:o2].reshape(bsz, L, SWA_HEADS, HEAD_DIM)
    k = p[..., o2:o3].reshape(bsz, L, SWA_KV_HEADS, HEAD_DIM)
    v = p[..., o3:].reshape(bsz, L, SWA_KV_HEADS, HEAD_DIM)
    return q, k, v


def even_mixer(h_lat, h_ctx, w_in, w_out, hy_conv, hy_w1, hy_b1, hy_w2, hy_b2, hy_freq, hy_w3,
               hy_decay, hy_bias, sink, ctx_out):
    bsz, L = h_lat.shape[0], h_lat.shape[1]
    Lc = h_ctx.shape[1]
    p_lat = h_lat @ w_in
    p_ctx = h_ctx @ w_in
    sink_g = sink.reshape(SWA_KV_HEADS, SWA_GROUP)
    q, k, v = split_swa(p_lat)
    qc, kc, vc = split_swa(p_ctx)
    q = axial_rope(q).reshape(bsz, L, SWA_KV_HEADS, SWA_GROUP, HEAD_DIM)
    k = axial_rope(k)
    y_a = hyena_mixer(p_lat[..., :E_HY_COLS], hy_conv, hy_w1, hy_b1, hy_w2, hy_b2, hy_freq, hy_w3,
                      hy_decay, hy_bias)
    y_b = windowed_attention_latent(q, k, v, kc, vc, sink_g)
    y_lat = jnp.concatenate([y_a, y_b], axis=-1) @ w_out
    if not ctx_out:
        return y_lat, None
    yc_a = hyena_mixer(p_ctx[..., :E_HY_COLS], hy_conv, hy_w1, hy_b1, hy_w2, hy_b2, hy_freq, hy_w3,
                       hy_decay, hy_bias)
    yc_b = context_attention(qc.reshape(bsz, Lc, SWA_KV_HEADS, SWA_GROUP, HEAD_DIM), kc, vc, sink_g)
    y_ctx = jnp.concatenate([yc_a, yc_b], axis=-1) @ w_out
    return y_lat, y_ctx


def _heads(t):
    return t.astype(F32).reshape(t.shape[0], t.shape[1], RW_HEADS, RW_HEAD)


def rwkv7_features(p, mu, w0, w2, a0, a2, g2, k_k, k_a):
    bsz, L = p.shape[0], p.shape[1]
    pp = jnp.pad(p, ((0, 0), (1, 1), (0, 0)))
    p = p + mu * (0.5 * (pp[:, :-2] + pp[:, 2:]) - p)
    o1, o2, o3 = RW_W, 2 * RW_W, 3 * RW_W
    o4 = o3 + 2 * RW_DECAY_LORA
    o5 = o4 + 2 * RW_AAA_LORA
    r, k, v = p[..., :o1], p[..., o1:o2], p[..., o2:o3]
    wd = p[..., o3:o4].reshape(bsz, L, 2, RW_DECAY_LORA)
    ad = p[..., o4:o5].reshape(bsz, L, 2, RW_AAA_LORA)
    gd = p[..., o5:]
    w_log = -jax.nn.softplus(-(w0 + jnp.einsum('bldr,drc->bldc', jnp.tanh(wd), w2))) - 0.5
    decay = jnp.exp(-jnp.exp(w_log.astype(F32)))
    a = jax.nn.sigmoid(a0 + jnp.einsum('bldr,drc->bldc', ad, a2))
    g = jax.nn.sigmoid(gd) @ g2
    kk = l2_normalize((k * k_k).reshape(bsz, L, RW_HEADS, RW_HEAD)).reshape(bsz, L, RW_W)
    k_dir = k[:, :, None] * (1.0 + (a - 1.0) * k_a)
    b_dir = kk[:, :, None] * a
    return r, v, g, kk, decay, k_dir, b_dir


def rwkv7_scan(r, decay, k, v, kk, b, s0, reverse):
    def step(S, inp):
        r_t, w_t, k_t, v_t, kk_t, b_t = inp
        sa = jnp.einsum('bhvk,bhk->bhv', S, kk_t)
        S = S * w_t[:, :, None, :] - sa[..., None] * b_t[:, :, None, :] + v_t[..., None] * k_t[:, :, None, :]
        return S, jnp.einsum('bhvk,bhk->bhv', S, r_t)

    xs = tuple(jnp.moveaxis(t, 1, 0) for t in (r, decay, k, v, kk, b))
    S, ys = lax.scan(step, s0, xs, reverse=reverse)
    return jnp.moveaxis(ys, 0, 1), S


def rwkv7_direction(f_lat, f_ctx, d, s0, r_k):
    reverse = d == 1

    def run(f, s_init):
        r, v, g, kk, decay, k_dir, b_dir = f
        rh, vh, kh = _heads(r), _heads(v), _heads(k_dir[:, :, d])
        y, s = rwkv7_scan(rh, _heads(decay[:, :, d]), kh, vh, _heads(kk), _heads(b_dir[:, :, d]), s_init, reverse)
        bonus = jnp.sum(rh * kh * r_k, -1, keepdims=True) * vh
        return y, bonus, s

    yc, bc, s_ctx = run(f_ctx, s0)
    yl, bl, _ = run(f_lat, s_ctx)
    return yl, bl, yc, bc


def rwkv7_output(y, bonus, g, gn_g, gn_b):
    bsz, L = y.shape[0], y.shape[1]
    mu = jnp.mean(y, -1, keepdims=True)
    var = jnp.mean(jnp.square(y - mu), -1, keepdims=True)
    yn = ((y - mu) * lax.rsqrt(var + RW_GN_EPS)).reshape(bsz, L, RW_W) * gn_g + gn_b
    return ((yn + bonus.reshape(bsz, L, RW_W)) * g).astype(g.dtype)


def rwkv7_mixer(p_lat, p_ctx, mu, w0, w2, a0, a2, g2, k_k, k_a, r_k, gn_g, gn_b, ctx_out):
    f_lat = rwkv7_features(p_lat, mu, w0, w2, a0, a2, g2, k_k, k_a)
    f_ctx = rwkv7_features(p_ctx, mu, w0, w2, a0, a2, g2, k_k, k_a)
    s_zero = jnp.zeros((p_lat.shape[0], RW_HEADS, RW_HEAD, RW_HEAD), F32)
    yl_f, bl_f, yc_f, bc_f = rwkv7_direction(f_lat, f_ctx, 0, s_zero, r_k)
    yl_b, bl_b, yc_b, bc_b = rwkv7_direction(f_lat, f_ctx, 1, s_zero, r_k)
    y_lat = rwkv7_output(yl_f + yl_b, bl_f + bl_b, f_lat[2], gn_g, gn_b)
    y_ctx = rwkv7_output(yc_f + yc_b, bc_f + bc_b, f_ctx[2], gn_g, gn_b) if ctx_out else None
    return y_lat, y_ctx


def gdn_features(p, conv_w, A_log, dt_bias):
    bsz, L = p.shape[0], p.shape[1]
    qkv = jax.nn.silu(centred_conv(p[..., :3 * DN_W], conv_w))
    q, k, v = jnp.split(qkv, 3, axis=-1)
    q = l2_normalize(q.reshape(bsz, L, DN_HEADS, DN_HEAD)) * (DN_HEAD ** -0.5)
    k = l2_normalize(k.reshape(bsz, L, DN_HEADS, DN_HEAD))
    v = v.reshape(bsz, L, DN_HEADS, DN_HEAD).astype(F32)
    z = p[..., 3 * DN_W:4 * DN_W]
    gates = p[..., 4 * DN_W:].astype(F32).reshape(bsz, L, 2, 2, DN_HEADS)
    g_log = -jnp.exp(A_log) * jax.nn.softplus(gates[:, :, 0] + dt_bias)
    beta = jax.nn.sigmoid(gates[:, :, 1])
    return q, k, v, z, g_log, beta


def gdn_chunked(q, k, v, g_log, beta, s0):
    bsz, L = q.shape[0], q.shape[1]
    C = DN_CHUNK
    n = L // C

    def chunks(t):
        return jnp.moveaxis(t.reshape(bsz, n, C, *t.shape[2:]), 2, 3)

    qc, kc, vc = chunks(q), chunks(k), chunks(v)
    G = jnp.cumsum(chunks(g_log), axis=-1)
    bc = chunks(beta)
    causal = jnp.tril(jnp.ones((C, C), bool))
    strict = jnp.tril(jnp.ones((C, C), bool), -1)
    diff = G[..., :, None] - G[..., None, :]
    decay_mat = jnp.where(causal, jnp.exp(jnp.where(causal, diff, 0.0)), 0.0)
    A = jnp.where(strict, bc[..., :, None] * jnp.einsum('bnhid,bnhjd->bnhij', kc, kc) * decay_mat, 0.0)
    T = A + jnp.eye(C, dtype=F32)
    u0 = lax.linalg.triangular_solve(T, bc[..., None] * vc, left_side=True, lower=True, unit_diagonal=True)
    w = lax.linalg.triangular_solve(T, (bc * jnp.exp(G))[..., None] * kc, left_side=True, lower=True,
                                    unit_diagonal=True)
    qk = jnp.einsum('bnhid,bnhjd->bnhij', qc, kc) * decay_mat
    q_dec = qc * jnp.exp(G)[..., None]
    k_dec = kc * jnp.exp(G[..., -1:] - G)[..., None]
    g_end = jnp.exp(G[..., -1])

    def step(S, inp):
        u0_c, w_c, qk_c, q_c, k_c, ge = inp
        u = u0_c - jnp.einsum('bhck,bhkv->bhcv', w_c, S)
        o = jnp.einsum('bhck,bhkv->bhcv', q_c, S) + jnp.einsum('bhij,bhjv->bhiv', qk_c, u)
        S = ge[..., None, None] * S + jnp.einsum('bhck,bhcv->bhkv', k_c, u)
        return S, o

    xs = tuple(jnp.moveaxis(t, 1, 0) for t in (u0, w, qk, q_dec, k_dec, g_end))
    S, o = lax.scan(step, s0, xs)
    o = jnp.moveaxis(jnp.moveaxis(o, 0, 1), 2, 3).reshape(bsz, L, DN_HEADS, DN_HEAD)
    return o, S


def gdn_run(f, d, s0):
    q, k, v, z, g_log, beta = f
    g_d, b_d = g_log[:, :, d], beta[:, :, d]
    if d == 0:
        return gdn_chunked(q, k, v, g_d, b_d, s0)
    flip = lambda t: jnp.flip(t, axis=1)
    o, S = gdn_chunked(flip(q), flip(k), flip(v), flip(g_d), flip(b_d), s0)
    return flip(o), S


def gdn_output(o, z, norm_g):
    bsz, L = o.shape[0], o.shape[1]
    on = o * lax.rsqrt(jnp.mean(o * o, -1, keepdims=True) + 1e-6) * norm_g
    return (on.reshape(bsz, L, DN_W) * jax.nn.silu(z.astype(F32))).astype(z.dtype)


def gdn_mixer(p_lat, p_ctx, conv_w, A_log, dt_bias, norm_g, ctx_out):
    f_lat = gdn_features(p_lat, conv_w, A_log, dt_bias)
    f_ctx = gdn_features(p_ctx, conv_w, A_log, dt_bias)
    s_zero = jnp.zeros((p_lat.shape[0], DN_HEADS, DN_HEAD, DN_HEAD), F32)
    oc_f, sc_f = gdn_run(f_ctx, 0, s_zero)
    ol_f, _ = gdn_run(f_lat, 0, sc_f)
    oc_b, sc_b = gdn_run(f_ctx, 1, s_zero)
    ol_b, _ = gdn_run(f_lat, 1, sc_b)
    y_lat = gdn_output(ol_f + ol_b, f_lat[3], norm_g)
    y_ctx = gdn_output(oc_f + oc_b, f_ctx[3], norm_g) if ctx_out else None
    return y_lat, y_ctx


def odd_mixer(h_lat, h_ctx, w_in, w_out, rw_mu, rw_w0, rw_w2, rw_a0, rw_a2, rw_g2, rw_kk, rw_ka, rw_rk,
              rw_lnx_g, rw_lnx_b, dn_conv, dn_A_log, dn_dt_bias, dn_norm_g, ctx_out):
    p_lat = h_lat @ w_in
    p_ctx = h_ctx @ w_in
    yc_l, yc_c = rwkv7_mixer(p_lat[..., :RW_COLS], p_ctx[..., :RW_COLS], rw_mu, rw_w0, rw_w2, rw_a0, rw_a2,
                             rw_g2, rw_kk, rw_ka, rw_rk, rw_lnx_g, rw_lnx_b, ctx_out)
    yd_l, yd_c = gdn_mixer(p_lat[..., RW_COLS:], p_ctx[..., RW_COLS:], dn_conv, dn_A_log, dn_dt_bias,
                           dn_norm_g, ctx_out)
    y_lat = jnp.concatenate([yc_l, yd_l], axis=-1) @ w_out
    y_ctx = jnp.concatenate([yc_c, yd_c], axis=-1) @ w_out if ctx_out else None
    return y_lat, y_ctx


def setup_inputs(seed: int = 0) -> dict:
    key = jax.random.key(seed)
    keys = iter(jax.random.split(key, 64))

    def normal(shape, std):
        return std * jax.random.normal(next(keys), shape, F32)

    def uniform(shape, lo, hi):
        return jax.random.uniform(next(keys), shape, F32, lo, hi)

    dt = jnp.exp(uniform((N_ODD, 2, DN_HEADS), math.log(1e-3), math.log(1e-1)))
    return {
        "x": normal((BATCH, SEQ, D_MODEL), 1.0),
        "c": normal((BATCH, D_MODEL), 1.0),
        "ctx": normal((BATCH, CTX_LEN, D_MODEL), 1.0),
        "c_ctx": normal((D_MODEL,), 1.0),
        "mod_w": normal((DEPTH, D_MODEL, N_MOD * D_MODEL), 0.5 * D_MODEL ** -0.5),
        "mod_b": normal((DEPTH, N_MOD * D_MODEL), 0.02),
        "ln_g": 1.0 + normal((DEPTH, 2, D_MODEL), 0.02),
        "ln_b": normal((DEPTH, 2, D_MODEL), 0.02),
        "mlp_w1": normal((DEPTH, D_MODEL, D_FF), D_MODEL ** -0.5),
        "mlp_w2": normal((DEPTH, D_FF, D_MODEL), DEEPNORM_BETA * D_FF ** -0.5),
        "e_w_in": normal((N_EVEN, D_MODEL, E_IN_COLS), D_MODEL ** -0.5),
        "e_w_out": normal((N_EVEN, MIX_W, D_MODEL), DEEPNORM_BETA * MIX_W ** -0.5),
        "hy_conv": normal((N_EVEN, HY_SHORT, E_HY_COLS), HY_SHORT ** -0.5),
        "hy_ffn_w1": normal((N_EVEN, HY_PE_DIM, HY_FFN), HY_PE_DIM ** -0.5),
        "hy_ffn_b1": normal((N_EVEN, HY_FFN), 0.1),
        "hy_ffn_w2": normal((N_EVEN, HY_FFN, HY_FFN), HY_FFN ** -0.5),
        "hy_ffn_b2": normal((N_EVEN, HY_FFN), 0.1),
        "hy_sin_freq": 1.0 + normal((N_EVEN, HY_FFN), 0.1),
        "hy_ffn_w3": normal((N_EVEN, HY_FFN, 2 * HY_W), HY_FFN ** -0.5),
        "hy_decay": jnp.linspace(HY_DECAY_MIN, HY_DECAY_MAX, 2 * HY_W, dtype=F32)
                    * (1.0 + normal((N_EVEN, 2 * HY_W), 0.05)),
        "hy_bias": normal((N_EVEN, HY_W), 1.0),
        "attn_sink": normal((N_EVEN, SWA_HEADS), 0.5),
        "o_w_in": normal((N_ODD, D_MODEL, O_IN_COLS), D_MODEL ** -0.5),
        "o_w_out": normal((N_ODD, MIX_W, D_MODEL), DEEPNORM_BETA * MIX_W ** -0.5),
        "rw_mu": 0.5 + normal((N_ODD, RW_COLS), 0.1),
        "rw_w0": jnp.linspace(-6.0, -1.0, RW_W, dtype=F32) + normal((N_ODD, 2, RW_W), 0.1),
        "rw_w2": normal((N_ODD, 2, RW_DECAY_LORA, RW_W), 0.5 * RW_DECAY_LORA ** -0.5),
        "rw_a0": normal((N_ODD, 2, RW_W), 0.1),
        "rw_a2": normal((N_ODD, 2, RW_AAA_LORA, RW_W), 0.5 * RW_AAA_LORA ** -0.5),
        "rw_g2": normal((N_ODD, RW_GATE_LORA, RW_W), RW_GATE_LORA ** -0.5),
        "rw_kk": 0.85 + normal((N_ODD, RW_W), 0.05),
        "rw_ka": 1.0 + normal((N_ODD, RW_W), 0.05),
        "rw_rk": normal((N_ODD, RW_HEADS, RW_HEAD), 0.1),
        "rw_lnx_g": 1.0 + normal((N_ODD, RW_W), 0.02),
        "rw_lnx_b": normal((N_ODD, RW_W), 0.02),
        "dn_conv": normal((N_ODD, DN_SHORT, 3 * DN_W), DN_SHORT ** -0.5),
        "dn_A_log": jnp.log(uniform((N_ODD, 2, DN_HEADS), 1.0, 16.0)),
        "dn_dt_bias": dt + jnp.log(-jnp.expm1(-dt)),
        "dn_norm_g": 1.0 + normal((N_ODD, DN_HEAD), 0.02),
    }


def reference(x, c, ctx, c_ctx, mod_w, mod_b, ln_g, ln_b, mlp_w1, mlp_w2, e_w_in, e_w_out, hy_conv,
              hy_ffn_w1, hy_ffn_b1, hy_ffn_w2, hy_ffn_b2, hy_sin_freq, hy_ffn_w3, hy_decay, hy_bias, attn_sink,
              o_w_in, o_w_out, rw_mu, rw_w0, rw_w2, rw_a0, rw_a2, rw_g2, rw_kk, rw_ka, rw_rk, rw_lnx_g,
              rw_lnx_b, dn_conv, dn_A_log, dn_dt_bias, dn_norm_g):
    alpha = DEEPNORM_ALPHA
    for i in range(DEPTH):
        ctx_out = i != DEPTH - 1
        m_lat = jax.nn.silu(c) @ mod_w[i] + mod_b[i]
        m_ctx = jax.nn.silu(c_ctx) @ mod_w[i] + mod_b[i]
        sh1, sc1, g1, sh2, sc2, g2 = jnp.split(m_lat[:, None, :], N_MOD, axis=-1)
        csh1, csc1, cg1, csh2, csc2, cg2 = jnp.split(m_ctx, N_MOD, axis=-1)
        h_lat = x * (1.0 + sc1) + sh1
        h_ctx = ctx * (1.0 + csc1) + csh1
        j = i // 2
        if i % 2 == 0:
            y_lat, y_ctx = even_mixer(h_lat, h_ctx, e_w_in[j], e_w_out[j], hy_conv[j], hy_ffn_w1[j], hy_ffn_b1[j],
                                      hy_ffn_w2[j], hy_ffn_b2[j], hy_sin_freq[j], hy_ffn_w3[j], hy_decay[j],
                                      hy_bias[j], attn_sink[j], ctx_out)
        else:
            y_lat, y_ctx = odd_mixer(h_lat, h_ctx, o_w_in[j], o_w_out[j], rw_mu[j], rw_w0[j], rw_w2[j], rw_a0[j],
                                     rw_a2[j], rw_g2[j], rw_kk[j], rw_ka[j], rw_rk[j], rw_lnx_g[j], rw_lnx_b[j],
                                     dn_conv[j], dn_A_log[j], dn_dt_bias[j], dn_norm_g[j], ctx_out)
        x = layer_norm(alpha * x + g1 * y_lat, ln_g[i, 0], ln_b[i, 0])
        x = layer_norm(alpha * x + g2 * sq_relu_mlp(x * (1.0 + sc2) + sh2, mlp_w1[i], mlp_w2[i]),
                       ln_g[i, 1], ln_b[i, 1])
        if ctx_out:
            ctx = layer_norm(alpha * ctx + cg1 * y_ctx, ln_g[i, 0], ln_b[i, 0])
            ctx = layer_norm(alpha * ctx + cg2 * sq_relu_mlp(ctx * (1.0 + csc2) + csh2, mlp_w1[i], mlp_w2[i]),
                             ln_g[i, 1], ln_b[i, 1])
    return x
```

```python
import functools
import math

import jax
import jax.numpy as jnp
from jax import lax
from jax.experimental import pallas as pl
from jax.experimental.pallas import tpu as pltpu

F32 = jnp.float32
BF16 = jnp.bfloat16

DEPTH = 2
GRID_W = 64
N_MOD = 6
DEEPNORM_ALPHA = (2.0 * DEPTH) ** 0.25
LN_EPS = 1e-5

HY_BANDS = 16
HEAD_DIM = 64
SWA_KV_HEADS = 2
WINDOW = 128
SWA_BLOCK = 128
ROPE_BASE = 10000.0

RW_HEAD = 64
RW_DECAY_LORA = 64
RW_AAA_LORA = 64
RW_GATE_LORA = 128
RW_GN_EPS = 64e-5

DN_HEAD = 128
DN_CHUNK = 64

LANE = 128
VMEM_LIMIT = 56 * 1024 * 1024


def _row_tile(n, cap):
    t = min(n, cap)
    while n % t:
        t //= 2
    return t


def _mod_kernel(c_ref, w_ref, b_ref, o_ref):
    c = c_ref[...]
    s = c * jax.nn.sigmoid(c)
    o_ref[...] = jnp.dot(s.astype(BF16), w_ref[...].astype(BF16), preferred_element_type=F32) + b_ref[...]


def mod_vectors(cc, w, b):
    r, d = cc.shape
    n = w.shape[1]
    tn = _row_tile(n, 1024)
    return pl.pallas_call(
        _mod_kernel,
        grid=(n // tn,),
        in_specs=[pl.BlockSpec((r, d), lambda j: (0, 0)),
                  pl.BlockSpec((d, tn), lambda j: (0, j)),
                  pl.BlockSpec((1, tn), lambda j: (0, j))],
        out_specs=pl.BlockSpec((r, tn), lambda j: (0, j)),
        out_shape=jax.ShapeDtypeStruct((r, n), F32),
        name="mod_vectors",
    )(cc, w, b.reshape(1, n))


def _mod_linear_kernel(x_ref, sc_ref, sh_ref, w_ref, o_ref):
    h = x_ref[0] * (1.0 + sc_ref[0]) + sh_ref[0]
    o_ref[0] = jnp.dot(h.astype(BF16), w_ref[...], preferred_element_type=F32)


def mod_linear(x, sc, sh, w):
    bsz, L, d = x.shape
    n = w.shape[1]
    tm = _row_tile(L, 512)
    return pl.pallas_call(
        _mod_linear_kernel,
        grid=(bsz, L // tm),
        in_specs=[pl.BlockSpec((1, tm, d), lambda b, i: (b, i, 0)),
                  pl.BlockSpec((1, 1, d), lambda b, i: (b, 0, 0)),
                  pl.BlockSpec((1, 1, d), lambda b, i: (b, 0, 0)),
                  pl.BlockSpec((d, n), lambda b, i: (0, 0))],
        out_specs=pl.BlockSpec((1, tm, n), lambda b, i: (b, i, 0)),
        out_shape=jax.ShapeDtypeStruct((bsz, L, n), F32),
        compiler_params=pltpu.CompilerParams(dimension_semantics=("parallel", "parallel"),
                                             vmem_limit_bytes=VMEM_LIMIT),
        name="mod_linear",
    )(x, sc, sh, w)


def _layer_norm_rows(z, g, b):
    mu = jnp.mean(z, -1, keepdims=True)
    zc = z - mu
    var = jnp.mean(zc * zc, -1, keepdims=True)
    return zc * lax.rsqrt(var + LN_EPS) * g + b


def _out_ln_kernel(y_ref, w_ref, x_ref, g_ref, lg_ref, lb_ref, o_ref):
    y = jnp.dot(y_ref[0].astype(BF16), w_ref[...], preferred_element_type=F32)
    z = DEEPNORM_ALPHA * x_ref[0] + g_ref[0] * y
    o_ref[0] = _layer_norm_rows(z, lg_ref[...], lb_ref[...])


def out_proj_ln(y, w, x, gate, ln_g, ln_b):
    bsz, L, d = x.shape
    k = y.shape[-1]
    tm = _row_tile(L, 512)
    return pl.pallas_call(
        _out_ln_kernel,
        grid=(bsz, L // tm),
        in_specs=[pl.BlockSpec((1, tm, k), lambda b, i: (b, i, 0)),
                  pl.BlockSpec((k, d), lambda b, i: (0, 0)),
                  pl.BlockSpec((1, tm, d), lambda b, i: (b, i, 0)),
                  pl.BlockSpec((1, 1, d), lambda b, i: (b, 0, 0)),
                  pl.BlockSpec((1, d), lambda b, i: (0, 0)),
                  pl.BlockSpec((1, d), lambda b, i: (0, 0))],
        out_specs=pl.BlockSpec((1, tm, d), lambda b, i: (b, i, 0)),
        out_shape=jax.ShapeDtypeStruct((bsz, L, d), F32),
        compiler_params=pltpu.CompilerParams(dimension_semantics=("parallel", "parallel"),
                                             vmem_limit_bytes=VMEM_LIMIT),
        name="out_proj_ln",
    )(y, w, x, gate, ln_g.reshape(1, d), ln_b.reshape(1, d))


def _mlp_ln_kernel(x_ref, sc_ref, sh_ref, g_ref, w1_ref, w2_ref, lg_ref, lb_ref, o_ref, *, ff_tile):
    x = x_ref[0]
    h = (x * (1.0 + sc_ref[0]) + sh_ref[0]).astype(BF16)
    d_ff = w1_ref.shape[1]
    acc = jnp.zeros(x.shape, F32)
    for c in range(d_ff // ff_tile):
        a = jnp.dot(h, w1_ref[:, c * ff_tile:(c + 1) * ff_tile], preferred_element_type=F32)
        a = jnp.maximum(a, 0.0)
        a = (a * a).astype(BF16)
        acc = acc + jnp.dot(a, w2_ref[c * ff_tile:(c + 1) * ff_tile, :], preferred_element_type=F32)
    z = DEEPNORM_ALPHA * x + g_ref[0] * acc
    o_ref[0] = _layer_norm_rows(z, lg_ref[...], lb_ref[...])


def mlp_ln(x, sc, sh, gate, w1, w2, ln_g, ln_b):
    bsz, L, d = x.shape
    d_ff = w1.shape[1]
    tm = _row_tile(L, 512)
    ff_tile = _row_tile(d_ff, 1024)
    const = dict(pipeline_mode=pl.Buffered(1))
    return pl.pallas_call(
        functools.partial(_mlp_ln_kernel, ff_tile=ff_tile),
        grid=(bsz, L // tm),
        in_specs=[pl.BlockSpec((1, tm, d), lambda b, i: (b, i, 0)),
                  pl.BlockSpec((1, 1, d), lambda b, i: (b, 0, 0)),
                  pl.BlockSpec((1, 1, d), lambda b, i: (b, 0, 0)),
                  pl.BlockSpec((1, 1, d), lambda b, i: (b, 0, 0)),
                  pl.BlockSpec((d, d_ff), lambda b, i: (0, 0), **const),
                  pl.BlockSpec((d_ff, d), lambda b, i: (0, 0), **const),
                  pl.BlockSpec((1, d), lambda b, i: (0, 0)),
                  pl.BlockSpec((1, d), lambda b, i: (0, 0))],
        out_specs=pl.BlockSpec((1, tm, d), lambda b, i: (b, i, 0)),
        out_shape=jax.ShapeDtypeStruct((bsz, L, d), F32),
        compiler_params=pltpu.CompilerParams(dimension_semantics=("parallel", "parallel"),
                                             vmem_limit_bytes=VMEM_LIMIT),
        name="mlp_ln",
    )(x, sc, sh, gate, w1, w2, ln_g.reshape(1, d), ln_b.reshape(1, d))


def l2_normalize(t, eps=1e-6):
    tf = t.astype(F32)
    return tf * lax.rsqrt(jnp.sum(tf * tf, -1, keepdims=True) + eps)


def centred_conv(u, w):
    width = w.shape[0]
    pad = width // 2
    L = u.shape[1]
    up = jnp.pad(u, ((0, 0), (pad, pad), (0, 0)))
    out = up[:, 0:L] * w[0]
    for j in range(1, width):
        out = out + up[:, j:j + L] * w[j]
    return out


def hyena_filters(L, w1, b1, w2, b2, freq, w3, decay):
    hy_w = w3.shape[1] // 2
    t = jnp.arange(L, dtype=F32)
    t_norm = t / max(L - 1, 1)
    bands = jnp.linspace(1e-4, HY_BANDS - 1, HY_BANDS, dtype=F32)
    ang = 2.0 * math.pi * t[:, None] * bands[None, :] / L
    pe = jnp.concatenate([t_norm[:, None], jnp.cos(ang), -jnp.sin(ang)], axis=-1)
    h = jnp.sin(freq * (pe @ w1 + b1))
    h = jnp.sin(freq * (h @ w2 + b2))
    h = (h @ w3) * jnp.exp(-t_norm[:, None] * jnp.abs(decay))
    return h[:, :hy_w], h[:, hy_w:]


def bidir_long_conv(u, h_fwd, h_bwd, bias):
    L = u.shape[1]
    k = jnp.concatenate([h_fwd, jnp.zeros_like(h_fwd[:1]), h_bwd[:0:-1]], axis=0)
    uf = jnp.fft.rfft(u.astype(F32), n=2 * L, axis=1)
    kf = jnp.fft.rfft(k.astype(F32), n=2 * L, axis=0)
    y = jnp.fft.irfft(uf * kf[None], n=2 * L, axis=1)[:, :L]
    return (y + u.astype(F32) * bias).astype(u.dtype)


def hyena_mixer(p, conv_w, w1, b1, w2, b2, freq, w3, decay, bias):
    z = centred_conv(p, conv_w)
    x0, x1, v = jnp.split(z, 3, axis=-1)
    h_f, h_b = hyena_filters(p.shape[1], w1, b1, w2, b2, freq, w3, decay)
    return x0 * bidir_long_conv(x1 * v, h_f, h_b, bias)


def axial_rope(t):
    L = t.shape[1]
    pos = jnp.arange(L, dtype=jnp.int32)
    half = HEAD_DIM // 2
    quarter = half // 2
    inv_freq = ROPE_BASE ** (-jnp.arange(quarter, dtype=F32) / quarter)
    tf = t.astype(F32)

    def rot(u, p):
        ang = p.astype(F32)[:, None] * inv_freq[None, :]
        cos = jnp.cos(ang)[None, :, None, :]
        sin = jnp.sin(ang)[None, :, None, :]
        u1, u2 = u[..., :quarter], u[..., quarter:]
        return jnp.concatenate([u1 * cos - u2 * sin, u1 * sin + u2 * cos], axis=-1)

    out = jnp.concatenate([rot(tf[..., :half], pos // GRID_W), rot(tf[..., half:], pos % GRID_W)], axis=-1)
    return out.astype(t.dtype)


def sink_softmax(s, sink):
    sk = sink.astype(F32)[None, :, :, None, None]
    m = jnp.maximum(jnp.max(s, -1, keepdims=True), sk)
    p = jnp.exp(s - m)
    return p / (jnp.sum(p, -1, keepdims=True) + jnp.exp(sk - m))


def windowed_attention_latent(q, k, v, kc, vc, sink):
    bsz, L = q.shape[0], q.shape[1]
    n_heads = q.shape[2] * q.shape[3]
    n_blk = L // SWA_BLOCK
    span = SWA_BLOCK + 2 * WINDOW
    scale = HEAD_DIM ** -0.5
    kp = jnp.pad(k, ((0, 0), (WINDOW, WINDOW), (0, 0), (0, 0)))
    vp = jnp.pad(v, ((0, 0), (WINDOW, WINDOW), (0, 0), (0, 0)))

    def one_block(b):
        start = b * SWA_BLOCK
        qb = lax.dynamic_slice_in_dim(q, start, SWA_BLOCK, axis=1)
        kb = lax.dynamic_slice_in_dim(kp, start, span, axis=1)
        vb = lax.dynamic_slice_in_dim(vp, start, span, axis=1)
        qpos = start + jnp.arange(SWA_BLOCK)
        kpos = start - WINDOW + jnp.arange(span)
        ok = (jnp.abs(qpos[:, None] - kpos[None, :]) <= WINDOW) & (kpos[None, :] >= 0) & (kpos[None, :] < L)
        s_loc = jnp.einsum('bqhgd,bkhd->bhgqk', qb, kb).astype(F32) * scale
        s_loc = jnp.where(ok, s_loc, -jnp.inf)
        s_ctx = jnp.einsum('bqhgd,bchd->bhgqc', qb, kc).astype(F32) * scale
        p = sink_softmax(jnp.concatenate([s_loc, s_ctx], axis=-1), sink).astype(v.dtype)
        return (jnp.einsum('bhgqk,bkhd->bqhgd', p[..., :span], vb)
                + jnp.einsum('bhgqc,bchd->bqhgd', p[..., span:], vc))

    o = lax.map(one_block, jnp.arange(n_blk))
    return jnp.moveaxis(o, 0, 1).reshape(bsz, L, n_heads * HEAD_DIM)


def context_attention(qc, kc, vc, sink):
    bsz, Lc = qc.shape[0], qc.shape[1]
    n_heads = qc.shape[2] * qc.shape[3]
    s = jnp.einsum('bqhgd,bchd->bhgqc', qc, kc).astype(F32) * (HEAD_DIM ** -0.5)
    p = sink_softmax(s, sink).astype(vc.dtype)
    return jnp.einsum('bhgqc,bchd->bqhgd', p, vc).reshape(bsz, Lc, n_heads * HEAD_DIM)


def even_mixers(p_lat, p_ctx, hy_w, n_heads, hy_conv, hy_w1, hy_b1, hy_w2, hy_b2, hy_freq, hy_w3, hy_decay,
                hy_bias, sink):
    bsz, L = p_lat.shape[0], p_lat.shape[1]
    Lc = p_ctx.shape[1]
    group = n_heads // SWA_KV_HEADS
    o1 = 3 * hy_w
    o2 = o1 + n_heads * HEAD_DIM
    o3 = o2 + SWA_KV_HEADS * HEAD_DIM
    o4 = o3 + SWA_KV_HEADS * HEAD_DIM

    def split(p):
        n = p.shape[1]
        return (p[..., o1:o2].reshape(bsz, n, n_heads, HEAD_DIM),
                p[..., o2:o3].reshape(bsz, n, SWA_KV_HEADS, HEAD_DIM),
                p[..., o3:o4].reshape(bsz, n, SWA_KV_HEADS, HEAD_DIM))

    sink_g = sink.reshape(SWA_KV_HEADS, group)
    q, k, v = split(p_lat)
    qc, kc, vc = split(p_ctx)
    q = axial_rope(q).reshape(bsz, L, SWA_KV_HEADS, group, HEAD_DIM)
    k = axial_rope(k)
    hy = (hy_conv, hy_w1, hy_b1, hy_w2, hy_b2, hy_freq, hy_w3, hy_decay, hy_bias)
    y_a = hyena_mixer(p_lat[..., :o1], *hy)
    y_b = windowed_attention_latent(q, k, v, kc, vc, sink_g)
    yc_a = hyena_mixer(p_ctx[..., :o1], *hy)
    yc_b = context_attention(qc.reshape(bsz, Lc, SWA_KV_HEADS, group, HEAD_DIM), kc, vc, sink_g)
    return jnp.concatenate([y_a, y_b], axis=-1), jnp.concatenate([yc_a, yc_b], axis=-1)


def rwkv7_features(p, mu, w0, w2, a0, a2, g2, k_k, k_a):
    bsz, L = p.shape[0], p.shape[1]
    rw_w = g2.shape[1]
    heads = rw_w // RW_HEAD
    pp = jnp.pad(p, ((0, 0), (1, 1), (0, 0)))
    p = p + mu * (0.5 * (pp[:, :-2] + pp[:, 2:]) - p)
    o1, o2, o3 = rw_w, 2 * rw_w, 3 * rw_w
    o4 = o3 + 2 * RW_DECAY_LORA
    o5 = o4 + 2 * RW_AAA_LORA
    r, k, v = p[..., :o1], p[..., o1:o2], p[..., o2:o3]
    wd = p[..., o3:o4].reshape(bsz, L, 2, RW_DECAY_LORA)
    ad = p[..., o4:o5].reshape(bsz, L, 2, RW_AAA_LORA)
    gd = p[..., o5:]
    w_log = -jax.nn.softplus(-(w0 + jnp.einsum('bldr,drc->bldc', jnp.tanh(wd), w2))) - 0.5
    decay = jnp.exp(-jnp.exp(w_log.astype(F32)))
    a = jax.nn.sigmoid(a0 + jnp.einsum('bldr,drc->bldc', ad, a2))
    g = jax.nn.sigmoid(gd) @ g2
    kk = l2_normalize((k * k_k).reshape(bsz, L, heads, RW_HEAD)).reshape(bsz, L, rw_w)
    k_dir = k[:, :, None] * (1.0 + (a - 1.0) * k_a)
    b_dir = kk[:, :, None] * a
    return r, v, g, kk, decay, k_dir, b_dir


def rwkv7_scan(r, decay, k, v, kk, b, s0, reverse):
    def step(S, inp):
        r_t, w_t, k_t, v_t, kk_t, b_t = inp
        sa = jnp.einsum('bhvk,bhk->bhv', S, kk_t)
        S = S * w_t[:, :, None, :] - sa[..., None] * b_t[:, :, None, :] + v_t[..., None] * k_t[:, :, None, :]
        return S, jnp.einsum('bhvk,bhk->bhv', S, r_t)

    xs = tuple(jnp.moveaxis(t, 1, 0) for t in (r, decay, k, v, kk, b))
    S, ys = lax.scan(step, s0, xs, reverse=reverse)
    return jnp.moveaxis(ys, 0, 1), S


def rwkv7_mixer(p_lat, p_ctx, mu, w0, w2, a0, a2, g2, k_k, k_a, r_k, gn_g, gn_b, ctx_out):
    rw_w = g2.shape[1]
    heads = rw_w // RW_HEAD

    def _heads(t):
        return t.astype(F32).reshape(t.shape[0], t.shape[1], heads, RW_HEAD)

    f_lat = rwkv7_features(p_lat, mu, w0, w2, a0, a2, g2, k_k, k_a)
    f_ctx = rwkv7_features(p_ctx, mu, w0, w2, a0, a2, g2, k_k, k_a)
    s_zero = jnp.zeros((p_lat.shape[0], heads, RW_HEAD, RW_HEAD), F32)

    def direction(d):
        reverse = d == 1

        def run(f, s_init):
            r, v, g, kk, decay, k_dir, b_dir = f
            rh, vh, kh = _heads(r), _heads(v), _heads(k_dir[:, :, d])
            y, s = rwkv7_scan(rh, _heads(decay[:, :, d]), kh, vh, _heads(kk), _heads(b_dir[:, :, d]), s_init,
                              reverse)
            bonus = jnp.sum(rh * kh * r_k, -1, keepdims=True) * vh
            return y, bonus, s

        yc, bc, s_ctx = run(f_ctx, s_zero)
        yl, bl, _ = run(f_lat, s_ctx)
        return yl, bl, yc, bc

    def output(y, bonus, g):
        bsz, L = y.shape[0], y.shape[1]
        mu_ = jnp.mean(y, -1, keepdims=True)
        var = jnp.mean(jnp.square(y - mu_), -1, keepdims=True)
        yn = ((y - mu_) * lax.rsqrt(var + RW_GN_EPS)).reshape(bsz, L, rw_w) * gn_g + gn_b
        return (yn + bonus.reshape(bsz, L, rw_w)) * g

    yl_f, bl_f, yc_f, bc_f = direction(0)
    yl_b, bl_b, yc_b, bc_b = direction(1)
    y_lat = output(yl_f + yl_b, bl_f + bl_b, f_lat[2])
    y_ctx = output(yc_f + yc_b, bc_f + bc_b, f_ctx[2]) if ctx_out else None
    return y_lat, y_ctx


def gdn_features(p, conv_w, A_log, dt_bias):
    bsz, L = p.shape[0], p.shape[1]
    dn_w = conv_w.shape[1] // 3
    heads = dn_w // DN_HEAD
    qkv = jax.nn.silu(centred_conv(p[..., :3 * dn_w], conv_w))
    q, k, v = jnp.split(qkv, 3, axis=-1)
    q = l2_normalize(q.reshape(bsz, L, heads, DN_HEAD)) * (DN_HEAD ** -0.5)
    k = l2_normalize(k.reshape(bsz, L, heads, DN_HEAD))
    v = v.reshape(bsz, L, heads, DN_HEAD).astype(F32)
    z = p[..., 3 * dn_w:4 * dn_w]
    gates = p[..., 4 * dn_w:4 * dn_w + 4 * heads].astype(F32).reshape(bsz, L, 2, 2, heads)
    g_log = -jnp.exp(A_log) * jax.nn.softplus(gates[:, :, 0] + dt_bias)
    beta = jax.nn.sigmoid(gates[:, :, 1])
    return q, k, v, z, g_log, beta


def gdn_chunked(q, k, v, g_log, beta, s0):
    bsz, L = q.shape[0], q.shape[1]
    heads = q.shape[2]
    C = DN_CHUNK
    n = L // C

    def chunks(t):
        return jnp.moveaxis(t.reshape(bsz, n, C, *t.shape[2:]), 2, 3)

    qc, kc, vc = chunks(q), chunks(k), chunks(v)
    G = jnp.cumsum(chunks(g_log), axis=-1)
    bc = chunks(beta)
    causal = jnp.tril(jnp.ones((C, C), bool))
    strict = jnp.tril(jnp.ones((C, C), bool), -1)
    diff = G[..., :, None] - G[..., None, :]
    decay_mat = jnp.where(causal, jnp.exp(jnp.where(causal, diff, 0.0)), 0.0)
    A = jnp.where(strict, bc[..., :, None] * jnp.einsum('bnhid,bnhjd->bnhij', kc, kc) * decay_mat, 0.0)
    T = A + jnp.eye(C, dtype=F32)
    u0 = lax.linalg.triangular_solve(T, bc[..., None] * vc, left_side=True, lower=True, unit_diagonal=True)
    w = lax.linalg.triangular_solve(T, (bc * jnp.exp(G))[..., None] * kc, left_side=True, lower=True,
                                    unit_diagonal=True)
    qk = jnp.einsum('bnhid,bnhjd->bnhij', qc, kc) * decay_mat
    q_dec = qc * jnp.exp(G)[..., None]
    k_dec = kc * jnp.exp(G[..., -1:] - G)[..., None]
    g_end = jnp.exp(G[..., -1])

    def step(S, inp):
        u0_c, w_c, qk_c, q_c, k_c, ge = inp
        u = u0_c - jnp.einsum('bhck,bhkv->bhcv', w_c, S)
        o = jnp.einsum('bhck,bhkv->bhcv', q_c, S) + jnp.einsum('bhij,bhjv->bhiv', qk_c, u)
        S = ge[..., None, None] * S + jnp.einsum('bhck,bhcv->bhkv', k_c, u)
        return S, o

    xs = tuple(jnp.moveaxis(t, 1, 0) for t in (u0, w, qk, q_dec, k_dec, g_end))
    S, o = lax.scan(step, s0, xs)
    o = jnp.moveaxis(jnp.moveaxis(o, 0, 1), 2, 3).reshape(bsz, L, heads, DN_HEAD)
    return o, S


def gdn_mixer(p_lat, p_ctx, conv_w, A_log, dt_bias, norm_g, ctx_out):
    dn_w = conv_w.shape[1] // 3
    heads = dn_w // DN_HEAD
    f_lat = gdn_features(p_lat, conv_w, A_log, dt_bias)
    f_ctx = gdn_features(p_ctx, conv_w, A_log, dt_bias)
    s_zero = jnp.zeros((p_lat.shape[0], heads, DN_HEAD, DN_HEAD), F32)

    def run(f, d, s0):
        q, k, v, z, g_log, beta = f
        g_d, b_d = g_log[:, :, d], beta[:, :, d]
        if d == 0:
            return gdn_chunked(q, k, v, g_d, b_d, s0)
        flip = lambda t: jnp.flip(t, axis=1)
        o, S = gdn_chunked(flip(q), flip(k), flip(v), flip(g_d), flip(b_d), s0)
        return flip(o), S

    def output(o, z):
        bsz, L = o.shape[0], o.shape[1]
        on = o * lax.rsqrt(jnp.mean(o * o, -1, keepdims=True) + 1e-6) * norm_g
        return on.reshape(bsz, L, dn_w) * jax.nn.silu(z.astype(F32))

    oc_f, sc_f = run(f_ctx, 0, s_zero)
    ol_f, _ = run(f_lat, 0, sc_f)
    oc_b, sc_b = run(f_ctx, 1, s_zero)
    ol_b, _ = run(f_lat, 1, sc_b)
    y_lat = output(ol_f + ol_b, f_lat[3])
    y_ctx = output(oc_f + oc_b, f_ctx[3]) if ctx_out else None
    return y_lat, y_ctx


def _pad_cols(w, mult):
    n = w.shape[1]
    pad = (-n) % mult
    return jnp.pad(w, ((0, 0), (0, pad))) if pad else w


def kernel(x, c, ctx, c_ctx, mod_w, mod_b, ln_g, ln_b, mlp_w1, mlp_w2, e_w_in, e_w_out, hy_conv, hy_ffn_w1,
           hy_ffn_b1, hy_ffn_w2, hy_ffn_b2, hy_sin_freq, hy_ffn_w3, hy_decay, hy_bias, attn_sink, o_w_in,
           o_w_out, rw_mu, rw_w0, rw_w2, rw_a0, rw_a2, rw_g2, rw_kk, rw_ka, rw_rk, rw_lnx_g, rw_lnx_b,
           dn_conv, dn_A_log, dn_dt_bias, dn_norm_g):
    bsz, L, d = x.shape
    depth = mod_w.shape[0]
    cc = jnp.concatenate([c, c_ctx[None, :]], axis=0)
    cc = jnp.pad(cc, ((0, (-cc.shape[0]) % 8), (0, 0)))
    for i in range(depth):
        ctx_out = i != depth - 1
        m = mod_vectors(cc, mod_w[i], mod_b[i])
        m_lat = m[:bsz].reshape(bsz, 1, N_MOD, d)
        m_ctx = jnp.broadcast_to(m[bsz].reshape(1, 1, N_MOD, d), (bsz, 1, N_MOD, d))
        sh1, sc1, g1, sh2, sc2, g2 = (m_lat[:, :, t] for t in range(N_MOD))
        csh1, csc1, cg1, csh2, csc2, cg2 = (m_ctx[:, :, t] for t in range(N_MOD))
        j = i // 2
        if i % 2 == 0:
            n_in = e_w_in.shape[2]
            w_in = _pad_cols(e_w_in[j], LANE).astype(BF16)
            w_out = e_w_out[j].astype(BF16)
            hy_w = hy_bias.shape[1]
            n_heads = attn_sink.shape[1]
            p_lat = mod_linear(x, sc1, sh1, w_in)
            p_ctx = mod_linear(ctx, csc1, csh1, w_in)
            y_lat, y_ctx = even_mixers(p_lat, p_ctx, hy_w, n_heads, hy_conv[j], hy_ffn_w1[j], hy_ffn_b1[j],
                                       hy_ffn_w2[j], hy_ffn_b2[j], hy_sin_freq[j], hy_ffn_w3[j], hy_decay[j],
                                       hy_bias[j], attn_sink[j])
        else:
            n_in = o_w_in.shape[2]
            w_in = _pad_cols(o_w_in[j], LANE).astype(BF16)
            w_out = o_w_out[j].astype(BF16)
            rw_cols = rw_mu.shape[1]
            p_lat = mod_linear(x, sc1, sh1, w_in)
            p_ctx = mod_linear(ctx, csc1, csh1, w_in)
            yc_l, yc_c = rwkv7_mixer(p_lat[..., :rw_cols], p_ctx[..., :rw_cols], rw_mu[j], rw_w0[j], rw_w2[j],
                                     rw_a0[j], rw_a2[j], rw_g2[j], rw_kk[j], rw_ka[j], rw_rk[j], rw_lnx_g[j],
                                     rw_lnx_b[j], ctx_out)
            yd_l, yd_c = gdn_mixer(p_lat[..., rw_cols:n_in], p_ctx[..., rw_cols:n_in], dn_conv[j], dn_A_log[j],
                                   dn_dt_bias[j], dn_norm_g[j], ctx_out)
            y_lat = jnp.concatenate([yc_l, yd_l], axis=-1)
            y_ctx = jnp.concatenate([yc_c, yd_c], axis=-1) if ctx_out else None
        w1 = mlp_w1[i].astype(BF16)
        w2 = mlp_w2[i].astype(BF16)
        x = out_proj_ln(y_lat, w_out, x, g1, ln_g[i, 0], ln_b[i, 0])
        x = mlp_ln(x, sc2, sh2, g2, w1, w2, ln_g[i, 1], ln_b[i, 1])
        if ctx_out:
            ctx = out_proj_ln(y_ctx, w_out, ctx, cg1, ln_g[i, 0], ln_b[i, 0])
            ctx = mlp_ln(ctx, csc2, csh2, cg2, w1, w2, ln_g[i, 1], ln_b[i, 1])
    return x
```

```python
import functools
import math

import jax
import jax.numpy as jnp
from jax import lax
from jax.experimental import pallas as pl
from jax.experimental.pallas import tpu as pltpu

F32 = jnp.float32
BF16 = jnp.bfloat16

DEPTH = 2
N_MOD = 6
DEEPNORM_ALPHA = (2.0 * DEPTH) ** 0.25
LN_EPS = 1e-5

HY_BANDS = 16
HEAD_DIM = 64
SWA_KV_HEADS = 2
WINDOW = 128
SWA_BLOCK = 128
GRID_W = 64
ROPE_BASE = 10000.0

RW_HEAD = 64
RW_DECAY_LORA = 64
RW_AAA_LORA = 64
RW_GN_EPS = 64e-5

DN_HEAD = 128
CHUNK = 64

LANE = 128
MXU_WIDTH = 256
VMEM_LIMIT = 56 * 1024 * 1024


def _row_tile(n, cap):
    t = min(n, cap)
    while n % t:
        t //= 2
    return t


def _params(*sem):
    return pltpu.CompilerParams(dimension_semantics=sem, vmem_limit_bytes=VMEM_LIMIT)


def _mm(a, b):
    return jnp.dot(a.astype(BF16), b.astype(BF16), preferred_element_type=F32)


def _mm_nt(a, b):
    return lax.dot_general(a.astype(BF16), b.astype(BF16), (((1,), (1,)), ((), ())), preferred_element_type=F32)


def _split2(x):
    hi = x.astype(BF16)
    return hi, (x - hi.astype(F32)).astype(BF16)


def _split3(x):
    hi, r = x.astype(BF16), None
    r = x - hi.astype(F32)
    mid = r.astype(BF16)
    return hi, mid, (r - mid.astype(F32)).astype(BF16)


def _mm_exact_lhs(a_bf16, x):
    return sum(jnp.dot(a_bf16, t, preferred_element_type=F32) for t in _split3(x))


def _mm_exact_rhs(x, b_bf16):
    return sum(jnp.dot(t, b_bf16, preferred_element_type=F32) for t in _split3(x))


def _mm_f32(a, b):
    ah, al = _split2(a)
    bh, bl = _split2(b)
    dot = lambda x, y: jnp.dot(x, y, preferred_element_type=F32)
    return dot(ah, bh) + dot(ah, bl) + dot(al, bh)


def _sigmoid(x):
    return 1.0 / (1.0 + jnp.exp(-x))


def _softplus(x):
    return jnp.maximum(x, 0.0) + jnp.log(1.0 + jnp.exp(-jnp.abs(x)))


def _neighbours(x, prev_row, next_row):
    n = x.shape[0]
    rows = lax.broadcasted_iota(jnp.int32, (n, 1), 0)
    x_prev = jnp.where(rows == 0, prev_row, pltpu.roll(x, 1, axis=0))
    x_next = jnp.where(rows == n - 1, next_row, pltpu.roll(x, n - 1, axis=0))
    return x_prev, x_next


def _halo_rows(hp_ref, hn_ref):
    i = pl.program_id(1)
    nt = pl.num_programs(1)
    return jnp.where(i > 0, hp_ref[0][7:8, :], 0.0), jnp.where(i < nt - 1, hn_ref[0][0:1, :], 0.0)


def _halo_specs(tm, T, cols, col_block=0):
    hb = tm // 8
    return [pl.BlockSpec((1, tm, cols), lambda b, i: (b, i, col_block)),
            pl.BlockSpec((1, 8, cols), lambda b, i: (b, jnp.maximum(i * hb - 1, 0), col_block)),
            pl.BlockSpec((1, 8, cols), lambda b, i: (b, jnp.minimum((i + 1) * hb, T // 8 - 1), col_block))]


def _unit_tri_inverse(a_list, rid, cid, blk):
    eye = jnp.where(rid == cid, 1.0, 0.0)
    pair = (rid >> 1) == (cid >> 1)
    d_list = [eye - jnp.where(pair, a, 0.0) for a in a_list]
    s, sh = 2, 1
    while s < blk:
        join = jnp.where((rid >> (sh + 1)) == (cid >> (sh + 1)), jnp.where((rid >> sh) != (cid >> sh), 1.0, 0.0), 0.0)
        dq = [_mm(d, a * join) for d, a in zip(d_list, a_list)]
        d_list = [d - _mm(t, d) for t, d in zip(dq, d_list)]
        s, sh = 2 * s, sh + 1
    return d_list


def _scan_index_maps(nc, nl):
    def ctx_idx(d, c):
        cc = jnp.minimum(c, nc - 1)
        return jnp.where(d == 1, nc - 1 - cc, cc)

    def lat_idx(d, c):
        cl = jnp.maximum(c - nc, 0)
        return jnp.where(d == 1, nl - 1 - cl, cl)

    return ctx_idx, lat_idx


def _mod_kernel(c_ref, w_ref, b_ref, o_ref):
    c = c_ref[...]
    o_ref[...] = _mm(c * _sigmoid(c), w_ref[...]) + b_ref[...]


def mod_vectors(cc, w, b):
    r, d = cc.shape
    n = w.shape[1]
    tn = _row_tile(n, 1024)
    return pl.pallas_call(
        _mod_kernel,
        grid=(n // tn,),
        in_specs=[pl.BlockSpec((r, d), lambda j: (0, 0)),
                  pl.BlockSpec((d, tn), lambda j: (0, j)),
                  pl.BlockSpec((1, tn), lambda j: (0, j))],
        out_specs=pl.BlockSpec((r, tn), lambda j: (0, j)),
        out_shape=jax.ShapeDtypeStruct((r, n), F32),
        name="mod_vectors",
    )(cc, w, b.reshape(1, n))


def _mod_linear_kernel(x_ref, sc_ref, sh_ref, *refs):
    n = len(refs) // 2
    h = (x_ref[0] * (1.0 + sc_ref[0]) + sh_ref[0]).astype(BF16)
    for w_ref, o_ref in zip(refs[:n], refs[n:]):
        o_ref[0] = jnp.dot(h, w_ref[...], preferred_element_type=F32)


def mod_linear(x, sc, sh, ws):
    bsz, L, d = x.shape
    tm = _row_tile(L, 512)
    vec = pl.BlockSpec((1, 1, d), lambda b, i: (b, 0, 0))
    return pl.pallas_call(
        _mod_linear_kernel,
        grid=(bsz, L // tm),
        in_specs=[pl.BlockSpec((1, tm, d), lambda b, i: (b, i, 0)), vec, vec]
        + [pl.BlockSpec(w.shape, lambda b, i: (0, 0)) for w in ws],
        out_specs=[pl.BlockSpec((1, tm, w.shape[1]), lambda b, i: (b, i, 0)) for w in ws],
        out_shape=[jax.ShapeDtypeStruct((bsz, L, w.shape[1]), F32) for w in ws],
        compiler_params=_params("parallel", "parallel"),
        name="mod_linear",
    )(x, sc, sh, *ws)


def _layer_norm_rows(z, g, b):
    mu = jnp.mean(z, -1, keepdims=True)
    zc = z - mu
    var = jnp.mean(zc * zc, -1, keepdims=True)
    return zc * lax.rsqrt(var + LN_EPS) * g + b


def _out_ln_kernel(ya_ref, yb_ref, wa_ref, wb_ref, x_ref, g_ref, lg_ref, lb_ref, o_ref):
    y = _mm(ya_ref[0], wa_ref[...]) + _mm(yb_ref[0], wb_ref[...])
    z = DEEPNORM_ALPHA * x_ref[0] + g_ref[0] * y
    o_ref[0] = _layer_norm_rows(z, lg_ref[...], lb_ref[...])


def out_proj_ln(ya, yb, w, x, gate, ln_g, ln_b):
    bsz, L, d = x.shape
    ka, kb = ya.shape[-1], yb.shape[-1]
    tm = _row_tile(L, 512)
    row = pl.BlockSpec((1, d), lambda b, i: (0, 0))
    return pl.pallas_call(
        _out_ln_kernel,
        grid=(bsz, L // tm),
        in_specs=[pl.BlockSpec((1, tm, ka), lambda b, i: (b, i, 0)),
                  pl.BlockSpec((1, tm, kb), lambda b, i: (b, i, 0)),
                  pl.BlockSpec((ka, d), lambda b, i: (0, 0)),
                  pl.BlockSpec((kb, d), lambda b, i: (0, 0)),
                  pl.BlockSpec((1, tm, d), lambda b, i: (b, i, 0)),
                  pl.BlockSpec((1, 1, d), lambda b, i: (b, 0, 0)), row, row],
        out_specs=pl.BlockSpec((1, tm, d), lambda b, i: (b, i, 0)),
        out_shape=jax.ShapeDtypeStruct((bsz, L, d), F32),
        compiler_params=_params("parallel", "parallel"),
        name="out_proj_ln",
    )(ya, yb, w[:ka], w[ka:], x, gate, ln_g.reshape(1, d), ln_b.reshape(1, d))


def _mlp_ln_kernel(x_ref, sc_ref, sh_ref, g_ref, w1_ref, w2_ref, lg_ref, lb_ref, o_ref, *, ff_tile):
    x = x_ref[0]
    h = (x * (1.0 + sc_ref[0]) + sh_ref[0]).astype(BF16)
    d_ff = w1_ref.shape[1]
    acc = jnp.zeros(x.shape, F32)
    for c in range(d_ff // ff_tile):
        a = jnp.dot(h, w1_ref[:, c * ff_tile:(c + 1) * ff_tile], preferred_element_type=F32)
        a = jnp.maximum(a, 0.0)
        a = (a * a).astype(BF16)
        acc = acc + jnp.dot(a, w2_ref[c * ff_tile:(c + 1) * ff_tile, :], preferred_element_type=F32)
    z = DEEPNORM_ALPHA * x + g_ref[0] * acc
    o_ref[0] = _layer_norm_rows(z, lg_ref[...], lb_ref[...])


def mlp_ln(x, sc, sh, gate, w1, w2, ln_g, ln_b):
    bsz, L, d = x.shape
    d_ff = w1.shape[1]
    tm = _row_tile(L, 512)
    ff_tile = _row_tile(d_ff, 1024)
    once = dict(pipeline_mode=pl.Buffered(1))
    vec = pl.BlockSpec((1, 1, d), lambda b, i: (b, 0, 0))
    row = pl.BlockSpec((1, d), lambda b, i: (0, 0))
    return pl.pallas_call(
        functools.partial(_mlp_ln_kernel, ff_tile=ff_tile),
        grid=(bsz, L // tm),
        in_specs=[pl.BlockSpec((1, tm, d), lambda b, i: (b, i, 0)), vec, vec, vec,
                  pl.BlockSpec((d, d_ff), lambda b, i: (0, 0), **once),
                  pl.BlockSpec((d_ff, d), lambda b, i: (0, 0), **once), row, row],
        out_specs=pl.BlockSpec((1, tm, d), lambda b, i: (b, i, 0)),
        out_shape=jax.ShapeDtypeStruct((bsz, L, d), F32),
        compiler_params=_params("parallel", "parallel"),
        name="mlp_ln",
    )(x, sc, sh, gate, w1, w2, ln_g.reshape(1, d), ln_b.reshape(1, d))


def dft_tables(T):
    n = 2 * T
    f = jnp.arange(T, dtype=jnp.int32)
    ang = ((f[:, None] * f[None, :]) % n).astype(F32) * (2.0 * math.pi / n)
    return jnp.cos(ang).astype(BF16), (-jnp.sin(ang)).astype(BF16)


def hyena_pos_features(T):
    t = jnp.arange(T, dtype=F32)
    t_norm = t / max(T - 1, 1)
    bands = jnp.linspace(1e-4, HY_BANDS - 1, HY_BANDS, dtype=F32)
    ang = 2.0 * math.pi * t[:, None] * bands[None, :] / T
    pe = jnp.concatenate([t_norm[:, None], jnp.cos(ang), -jnp.sin(ang)], axis=-1)
    return jnp.pad(pe, ((0, 0), (0, (-pe.shape[1]) % 8)))


def _hy_filter_kernel(pe_ref, w1_ref, b1_ref, w2_ref, b2_ref, fr_ref, w3f_ref, w3b_ref, df_ref, db_ref,
                      cm_ref, sm_ref, kr_o, ki_o, kn_o):
    T = pe_ref.shape[0]
    pe = pe_ref[...]
    fr = fr_ref[...]
    h = jnp.sin(fr * (_mm_f32(pe, w1_ref[...]) + b1_ref[...]))
    h = jnp.sin(fr * (_mm_f32(h, w2_ref[...]) + b2_ref[...]))
    t_norm = pe[:, 0:1]
    h_f = _mm_f32(h, w3f_ref[...]) * jnp.exp(-t_norm * jnp.abs(df_ref[...]))
    h_b = _mm_f32(h, w3b_ref[...]) * jnp.exp(-t_norm * jnp.abs(db_ref[...]))
    row = lax.broadcasted_iota(jnp.int32, (T, 1), 0)
    h_b = jnp.where(row == 0, 0.0, h_b)
    scale = jnp.where(row == 0, 1.0, 2.0) * (1.0 / (2 * T))
    kr_o[...] = _mm(cm_ref[...], h_f + h_b) * scale
    ki_o[...] = _mm(sm_ref[...], h_f - h_b) * scale
    sign = jnp.where((row & 1) == 0, 1.0, -1.0)
    kn = jnp.sum(sign * (h_f + h_b), axis=0, keepdims=True) * (1.0 / (2 * T))
    kn_o[...] = jnp.broadcast_to(kn, kn_o.shape)


def hyena_spectrum(T, cm, sm, w1, b1, w2, b2, freq, w3, decay):
    hy_w = w3.shape[1] // 2
    ct = MXU_WIDTH
    nj = hy_w // ct
    pe = hyena_pos_features(T)
    w1p = jnp.pad(w1, ((0, pe.shape[1] - w1.shape[0]), (0, 0)))
    row = lambda t: t.reshape(1, -1)
    full = lambda a: pl.BlockSpec(a.shape, lambda j: (0,) * a.ndim)
    ins = (pe, w1p, row(b1), w2, row(b2), row(freq))
    return pl.pallas_call(
        _hy_filter_kernel,
        grid=(nj,),
        in_specs=[full(a) for a in ins] + [
            pl.BlockSpec((w3.shape[0], ct), lambda j: (0, j)), pl.BlockSpec((w3.shape[0], ct), lambda j: (0, nj + j)),
            pl.BlockSpec((1, ct), lambda j: (0, j)), pl.BlockSpec((1, ct), lambda j: (0, nj + j)),
            pl.BlockSpec((T, T), lambda j: (0, 0)), pl.BlockSpec((T, T), lambda j: (0, 0))],
        out_specs=[pl.BlockSpec((T, ct), lambda j: (0, j)), pl.BlockSpec((T, ct), lambda j: (0, j)),
                   pl.BlockSpec((8, ct), lambda j: (0, j))],
        out_shape=[jax.ShapeDtypeStruct((T, hy_w), F32), jax.ShapeDtypeStruct((T, hy_w), F32),
                   jax.ShapeDtypeStruct((8, hy_w), F32)],
        compiler_params=_params("arbitrary"),
        name="hyena_spectrum",
    )(*ins, w3, w3, row(decay), row(decay), cm, sm)


def _hy_conv_kernel(x0_ref, x1_ref, xv_ref, c0_ref, c1_ref, cv_ref, bias_ref, kr_ref, ki_ref, kn_ref,
                    cm_ref, sm_ref, o_ref):
    T = x0_ref.shape[1]
    row = lax.broadcasted_iota(jnp.int32, (T, 1), 0)
    zero = jnp.zeros((1, x0_ref.shape[2]), F32)

    def short_conv(x_ref, c_ref):
        x = x_ref[0]
        c = c_ref[...]
        x_prev, x_next = _neighbours(x, zero, zero)
        return x_prev * c[0:1] + x * c[1:2] + x_next * c[2:3]

    u = short_conv(x1_ref, c1_ref) * short_conv(xv_ref, cv_ref)
    ub = u.astype(BF16)
    sign = jnp.where((row & 1) == 0, 1.0, -1.0)
    un = jnp.sum(sign * u, axis=0, keepdims=True)
    y = sign * (un * kn_ref[0:1, :]) + u * bias_ref[...]
    ft = min(T, 512)
    for f0 in range(0, T, ft):
        fs = slice(f0, f0 + ft)
        ur = jnp.dot(cm_ref[fs, :], ub, preferred_element_type=F32)
        ui = jnp.dot(sm_ref[fs, :], ub, preferred_element_type=F32)
        kr = kr_ref[fs, :]
        ki = ki_ref[fs, :]
        yr = (ur * kr - ui * ki).astype(BF16)
        yi = (ur * ki + ui * kr).astype(BF16)
        y = y + jnp.dot(cm_ref[:, fs], yr, preferred_element_type=F32) + jnp.dot(sm_ref[:, fs], yi,
                                                                                 preferred_element_type=F32)
    o_ref[0] = short_conv(x0_ref, c0_ref) * y


def hyena_mixer(p, conv_w, w1, b1, w2, b2, freq, w3, decay, bias):
    bsz, T, _ = p.shape
    hy_w = bias.shape[0]
    ct = MXU_WIDTH
    nj = hy_w // ct
    cm, sm = dft_tables(T)
    kr, ki, kn = hyena_spectrum(T, cm, sm, w1, b1, w2, b2, freq, w3, decay)
    cw = jnp.pad(conv_w, ((0, 8 - conv_w.shape[0]), (0, 0)))
    once = dict(pipeline_mode=pl.Buffered(1))
    xs = lambda part: pl.BlockSpec((1, T, ct), lambda j, b: (b, 0, part * nj + j))
    cs = lambda part: pl.BlockSpec((8, ct), lambda j, b: (0, part * nj + j))
    return pl.pallas_call(
        _hy_conv_kernel,
        grid=(nj, bsz),
        in_specs=[xs(0), xs(1), xs(2), cs(0), cs(1), cs(2),
                  pl.BlockSpec((1, ct), lambda j, b: (0, j)),
                  pl.BlockSpec((T, ct), lambda j, b: (0, j), **once),
                  pl.BlockSpec((T, ct), lambda j, b: (0, j), **once),
                  pl.BlockSpec((8, ct), lambda j, b: (0, j)),
                  pl.BlockSpec((T, T), lambda j, b: (0, 0), **once),
                  pl.BlockSpec((T, T), lambda j, b: (0, 0), **once)],
        out_specs=pl.BlockSpec((1, T, ct), lambda j, b: (b, 0, j)),
        out_shape=jax.ShapeDtypeStruct((bsz, T, hy_w), F32),
        compiler_params=_params("parallel", "parallel"),
        name="hyena_conv",
    )(p, p, p, cw, cw, cw, bias.reshape(1, hy_w), kr, ki, kn, cm, sm)


def rope_tables(L):
    quarter = HEAD_DIM // 4
    pos = jnp.arange(L, dtype=jnp.int32)
    inv_freq = ROPE_BASE ** (-jnp.arange(quarter, dtype=F32) / quarter)
    a_row = (pos // GRID_W).astype(F32)[:, None] * inv_freq[None, :]
    a_col = (pos % GRID_W).astype(F32)[:, None] * inv_freq[None, :]
    ang = jnp.concatenate([a_row, a_row, a_col, a_col], axis=-1)
    sgn = jnp.tile(jnp.concatenate([-jnp.ones(quarter, F32), jnp.ones(quarter, F32)]), 2)
    return jnp.tile(jnp.cos(ang), (1, 2)), jnp.tile(jnp.sin(ang) * sgn, (1, 2))


def _rope(t, cos, sin, first_quarter):
    w = t.shape[1]
    partner = jnp.where(first_quarter, pltpu.roll(t, w - 16, axis=1), pltpu.roll(t, 16, axis=1))
    return t * cos + partner * sin


def _attn_kernel(*refs, has_local, L, span, group):
    if has_local:
        q_ref, kv_ref, kvc_ref, cos_ref, sin_ref, sink_ref, o_ref = refs
    else:
        q_ref, kvc_ref, sink_ref, o_ref = refs
    i = pl.program_id(1)
    tq = q_ref.shape[1]
    lane = lax.broadcasted_iota(jnp.int32, (1, LANE), 1)
    low = lane < HEAD_DIM
    q = q_ref[0]
    n_heads = q.shape[1] // HEAD_DIM
    kvc = kvc_ref[0]
    kc, vc = kvc[:, :LANE], kvc[:, LANE:]
    if has_local:
        fq = (lane & 31) < 16
        q0 = pl.multiple_of(i * tq, tq)
        cos_q = cos_ref[pl.ds(q0, tq), :]
        sin_q = sin_ref[pl.ds(q0, tq), :]
        k0 = pl.multiple_of(jnp.clip(i * tq - WINDOW, 0, L - span), SWA_BLOCK)
        kv = kv_ref[0, pl.ds(k0, span), :]
        k = _rope(kv[:, :LANE], cos_ref[pl.ds(k0, span), :], sin_ref[pl.ds(k0, span), :], fq)
        v = kv[:, LANE:]
        qpos = q0 + lax.broadcasted_iota(jnp.int32, (tq, span), 0)
        kpos = k0 + lax.broadcasted_iota(jnp.int32, (tq, span), 1)
        ok1 = jnp.abs(qpos - kpos) <= WINDOW
        ok = jnp.concatenate([ok1] * group, axis=0)
    scale = HEAD_DIM ** -0.5

    def dup(x, g):
        sw = pltpu.roll(x, HEAD_DIM, axis=1)
        return jnp.where(low, x, sw) if g == 0 else jnp.where(low, sw, x)

    outs = []
    for g in range(SWA_KV_HEADS):
        rows, sinks = [], []
        for hh in range(group):
            h = g * group + hh
            j = h // 2
            q2 = q[:, j * LANE:(j + 1) * LANE]
            if has_local:
                q2 = _rope(q2, cos_q, sin_q, fq)
            q2 = q2 * scale
            rows.append(jnp.where(low, q2, 0.0) if h % 2 == 0 else jnp.where(low, 0.0, q2))
            sinks.append(jnp.broadcast_to(sink_ref[h:h + 1, 0:1], (tq, 1)))
        qs = jnp.concatenate(rows, axis=0)
        sk = jnp.concatenate(sinks, axis=0)
        s_ctx = _mm_nt(qs, dup(kc, g))
        m = jnp.maximum(jnp.max(s_ctx, -1, keepdims=True), sk)
        if has_local:
            s_loc = jnp.where(ok, _mm_nt(qs, dup(k, g)), -jnp.inf)
            m = jnp.maximum(m, jnp.max(s_loc, -1, keepdims=True))
            p_loc = jnp.exp(s_loc - m)
        p_ctx = jnp.exp(s_ctx - m)
        den = jnp.sum(p_ctx, -1, keepdims=True) + jnp.exp(sk - m)
        acc = _mm(p_ctx, dup(vc, g))
        if has_local:
            den = den + jnp.sum(p_loc, -1, keepdims=True)
            acc = acc + _mm(p_loc, dup(v, g))
        outs.append(acc / den)
    for j in range(n_heads // 2):
        h0, h1 = 2 * j, 2 * j + 1
        a = outs[h0 // group][(h0 % group) * tq:(h0 % group + 1) * tq]
        b = outs[h1 // group][(h1 % group) * tq:(h1 % group + 1) * tq]
        o_ref[0, :, j * LANE:(j + 1) * LANE] = jnp.where(low, a, b)


def _sink_rows(sink):
    return jnp.broadcast_to(sink.astype(F32)[:, None], (sink.shape[0], LANE))


def windowed_attention(p_lat, p_ctx, sink):
    bsz, L, cols = p_lat.shape
    Lc = p_ctx.shape[1]
    q_cols = cols - 2 * LANE
    n_heads = q_cols // HEAD_DIM
    tq = SWA_BLOCK
    span = min(L, SWA_BLOCK + 2 * WINDOW)
    assert q_cols % (2 * LANE) == 0 and L % tq == 0
    kvb = q_cols // (2 * LANE)
    cos, sin = rope_tables(L)
    return pl.pallas_call(
        functools.partial(_attn_kernel, has_local=True, L=L, span=span, group=n_heads // SWA_KV_HEADS),
        grid=(bsz, L // tq),
        in_specs=[pl.BlockSpec((1, tq, q_cols), lambda b, i: (b, i, 0)),
                  pl.BlockSpec((1, L, 2 * LANE), lambda b, i: (b, 0, kvb)),
                  pl.BlockSpec((1, Lc, 2 * LANE), lambda b, i: (b, 0, kvb)),
                  pl.BlockSpec((L, LANE), lambda b, i: (0, 0)),
                  pl.BlockSpec((L, LANE), lambda b, i: (0, 0)),
                  pl.BlockSpec((n_heads, LANE), lambda b, i: (0, 0))],
        out_specs=pl.BlockSpec((1, tq, q_cols), lambda b, i: (b, i, 0)),
        out_shape=jax.ShapeDtypeStruct((bsz, L, q_cols), F32),
        compiler_params=_params("parallel", "parallel"),
        name="windowed_attention",
    )(p_lat, p_lat, p_ctx, cos, sin, _sink_rows(sink))


def context_attention(p_ctx, sink):
    bsz, Lc, cols = p_ctx.shape
    q_cols = cols - 2 * LANE
    n_heads = q_cols // HEAD_DIM
    tq = min(Lc, SWA_BLOCK)
    kvb = q_cols // (2 * LANE)
    return pl.pallas_call(
        functools.partial(_attn_kernel, has_local=False, L=Lc, span=0, group=n_heads // SWA_KV_HEADS),
        grid=(bsz, Lc // tq),
        in_specs=[pl.BlockSpec((1, tq, q_cols), lambda b, i: (b, i, 0)),
                  pl.BlockSpec((1, Lc, 2 * LANE), lambda b, i: (b, 0, kvb)),
                  pl.BlockSpec((n_heads, LANE), lambda b, i: (0, 0))],
        out_specs=pl.BlockSpec((1, tq, q_cols), lambda b, i: (b, i, 0)),
        out_shape=jax.ShapeDtypeStruct((bsz, Lc, q_cols), F32),
        compiler_params=_params("parallel", "parallel"),
        name="context_attention",
    )(p_ctx, p_ctx, _sink_rows(sink))


def _rwkv_feat_kernel(p_ref, hp_ref, hn_ref, mu_ref, w0_ref, w2_ref, a0_ref, a2_ref, g2_ref, kk_ref, ka_ref,
                      rk_ref, e_ref, r_o, v_o, kkn_o, g_o, bonus_o, lw_o, kd_o, bd_o, *, rw_w):
    p = p_ref[0]
    p_prev, p_next = _neighbours(p, *_halo_rows(hp_ref, hn_ref))
    p = p + mu_ref[...] * (0.5 * (p_prev + p_next) - p)
    o1, o2, o3 = rw_w, 2 * rw_w, 3 * rw_w
    o4 = o3 + 2 * RW_DECAY_LORA
    o5 = o4 + 2 * RW_AAA_LORA
    r, k, v = p[:, :o1], p[:, o1:o2], p[:, o2:o3]
    wd, ad, gd = p[:, o3:o4], p[:, o4:o5], p[:, o5:]
    w_log = -_softplus(-(w0_ref[...] + _mm(jnp.tanh(wd), w2_ref[...]))) - 0.5
    lw = -jnp.exp(w_log)
    a = _sigmoid(a0_ref[...] + _mm(ad, a2_ref[...]))
    e = e_ref[...]
    kq = k * kk_ref[...]
    kkn = kq * lax.rsqrt(_mm_exact_rhs(kq * kq, e) + 1e-6)
    r_o[0] = r
    v_o[0] = v
    kkn_o[0] = kkn
    g_o[0] = _mm(_sigmoid(gd), g2_ref[...])
    ka = ka_ref[...]
    rk = r * rk_ref[...]
    acc = jnp.zeros_like(r)
    for d in range(2):
        a_d = a[:, d * rw_w:(d + 1) * rw_w]
        k_d = k * (1.0 + (a_d - 1.0) * ka)
        lw_o[d, 0] = lw[:, d * rw_w:(d + 1) * rw_w]
        kd_o[d, 0] = k_d
        bd_o[d, 0] = kkn * a_d
        acc = acc + rk * k_d
    bonus_o[0] = _mm_exact_rhs(acc, e) * v


def _head_indicator(w, head):
    hid = jnp.arange(w) // head
    return (hid[:, None] == hid[None, :]).astype(BF16)


def rwkv_features(p, mu, w0, w2, a0, a2, g2, k_k, k_a, r_k):
    bsz, T, cols = p.shape
    rw_w = g2.shape[1]
    tm = _row_tile(T, 256)
    w2_bd = jnp.zeros((2 * RW_DECAY_LORA, 2 * rw_w), F32)
    w2_bd = w2_bd.at[:RW_DECAY_LORA, :rw_w].set(w2[0]).at[RW_DECAY_LORA:, rw_w:].set(w2[1]).astype(BF16)
    a2_bd = jnp.zeros((2 * RW_AAA_LORA, 2 * rw_w), F32)
    a2_bd = a2_bd.at[:RW_AAA_LORA, :rw_w].set(a2[0]).at[RW_AAA_LORA:, rw_w:].set(a2[1]).astype(BF16)
    vec = lambda t: t.reshape(1, -1).astype(F32)
    full = lambda shape: pl.BlockSpec(shape, lambda b, i: (0,) * len(shape))
    tok = pl.BlockSpec((1, tm, rw_w), lambda b, i: (b, i, 0))
    tok2 = pl.BlockSpec((2, 1, tm, rw_w), lambda b, i: (0, b, i, 0))
    s1 = jax.ShapeDtypeStruct((bsz, T, rw_w), F32)
    s2 = jax.ShapeDtypeStruct((2, bsz, T, rw_w), F32)
    return pl.pallas_call(
        functools.partial(_rwkv_feat_kernel, rw_w=rw_w),
        grid=(bsz, T // tm),
        in_specs=_halo_specs(tm, T, cols) + [
            full((1, cols)), full((1, 2 * rw_w)), full(w2_bd.shape), full((1, 2 * rw_w)),
            full(a2_bd.shape), full(g2.shape), full((1, rw_w)), full((1, rw_w)), full((1, rw_w)),
            full((rw_w, rw_w))],
        out_specs=[tok, tok, tok, tok, tok, tok2, tok2, tok2],
        out_shape=[s1, s1, s1, s1, s1, s2, s2, s2],
        compiler_params=_params("parallel", "parallel"),
        name="rwkv_features",
    )(p, p, p, vec(mu), vec(w0), w2_bd, vec(a0), a2_bd, g2.astype(BF16), vec(k_k), vec(k_a), vec(r_k),
      _head_indicator(rw_w, RW_HEAD))


def _rwkv_scan_kernel(rc, vc, kkc, lwc, kdc, bdc, rl, vl, kkl, lwl, kdl, bdl, y_ref, s_ref, *, n_ctx):
    d = pl.program_id(1)
    c = pl.program_id(2)
    C = CHUNK
    n_pairs = s_ref.shape[0]

    @pl.when(c == 0)
    def _():
        s_ref[...] = jnp.zeros_like(s_ref)

    is_ctx = c < n_ctx
    pick = lambda a, b: jnp.where(is_ctx, a, b)
    r = pick(rc[0], rl[0])
    v = pick(vc[0], vl[0])
    kk = pick(kkc[0], kkl[0])
    lw = pick(lwc[0, 0], lwl[0, 0])
    kd = pick(kdc[0, 0], kdl[0, 0])
    bd = pick(bdc[0, 0], bdl[0, 0])
    rev = d == 1

    ti = lax.broadcasted_iota(jnp.int32, (C, C), 0)
    tj = lax.broadcasted_iota(jnp.int32, (C, C), 1)
    tri = jnp.where(jnp.where(rev, tj - ti, ti - tj) >= 0, 1.0, 0.0).astype(BF16)
    linc = _mm_exact_lhs(tri, lw)
    ltot = jnp.sum(lw, axis=0, keepdims=True)
    g_inv = jnp.exp(-linc)
    g_end = jnp.exp(ltot - linc)
    g_tot = jnp.exp(ltot)
    kk_d = kk * jnp.exp(linc - lw)
    r_d = r * jnp.exp(linc)
    k_i = kd * g_inv
    b_i = bd * g_inv
    k_e = kd * g_end
    b_e = bd * g_end

    n = 2 * C
    rid = lax.broadcasted_iota(jnp.int32, (n, n), 0)
    cid = lax.broadcasted_iota(jnp.int32, (n, n), 1)
    same = (rid >> 6) == (cid >> 6)
    diff = jnp.where(rev, cid - rid, rid - cid)
    strict = lambda m: jnp.where(same, jnp.where(diff > 0, m, 0.0), 0.0)
    incl = lambda m: jnp.where(same, jnp.where(diff >= 0, m, 0.0), 0.0)
    first = lax.broadcasted_iota(jnp.int32, (1, LANE), 1) < RW_HEAD

    def stack(x):
        return jnp.concatenate([jnp.where(first, x, 0.0), jnp.where(first, 0.0, x)], axis=0)

    pairs = range(n_pairs)
    sls = [slice(p * LANE, (p + 1) * LANE) for p in pairs]
    stk = [[stack(t[:, sl]) for t in (kk_d, r_d, k_i, b_i, k_e, b_e, v)] for sl in sls]
    kks, rs, kis, bis, kes, bes, vs = ([stk[p][i] for p in pairs] for i in range(7))
    m = [_mm_nt(jnp.concatenate([kks[p], rs[p]], axis=0), jnp.concatenate([kis[p], bis[p]], axis=0)) for p in pairs]
    a_k = [strict(t[:n, :n]) for t in m]
    a_b = [strict(t[:n, n:]) for t in m]
    b_k = [incl(t[n:, :n]) for t in m]
    b_b = [incl(t[n:, n:]) for t in m]
    t_inv = _unit_tri_inverse(a_b, rid, cid, C)
    akv = [_mm(a_k[p], vs[p]) for p in pairs]
    wu = [_mm(t_inv[p], jnp.concatenate([kks[p], akv[p]], axis=1)) for p in pairs]
    w_s = [t[:, :LANE] for t in wu]
    u0_s = [t[:, LANE:] for t in wu]
    bkv = [_mm(b_k[p], vs[p]) for p in pairs]
    vt = [t.T for t in vs]
    u0_t = [t.T for t in u0_s]
    st = [s_ref[p] for p in pairs]
    u_s = [_mm_nt(w_s[p], st[p]) + u0_s[p] for p in pairs]
    u_t = [_mm_nt(st[p], w_s[p]) + u0_t[p] for p in pairs]
    y_s = [_mm_nt(rs[p], st[p]) + bkv[p] - _mm(b_b[p], u_s[p]) for p in pairs]
    for p in pairs:
        s_ref[p] = st[p] * g_tot[:, sls[p]] + _mm(jnp.concatenate([vt[p], -u_t[p]], axis=1),
                                                   jnp.concatenate([kes[p], bes[p]], axis=0))
        y_ref[0, 0, :, sls[p]] = y_s[p][:C] + y_s[p][C:]


def rwkv_scan(f_ctx, f_lat):
    rc, vc, kkc, lwc, kdc, bdc = f_ctx
    rl, vl, kkl, lwl, kdl, bdl = f_lat
    bsz, Lc, w = rc.shape
    L = rl.shape[1]
    C = CHUNK
    assert Lc % C == 0 and L % C == 0 and w % LANE == 0
    nc, nl = Lc // C, L // C
    ctx_idx, lat_idx = _scan_index_maps(nc, nl)
    sc = pl.BlockSpec((1, C, w), lambda b, d, c: (b, ctx_idx(d, c), 0))
    dc = pl.BlockSpec((1, 1, C, w), lambda b, d, c: (d, b, ctx_idx(d, c), 0))
    slt = pl.BlockSpec((1, C, w), lambda b, d, c: (b, lat_idx(d, c), 0))
    dl = pl.BlockSpec((1, 1, C, w), lambda b, d, c: (d, b, lat_idx(d, c), 0))
    return pl.pallas_call(
        functools.partial(_rwkv_scan_kernel, n_ctx=nc),
        grid=(bsz, 2, nc + nl),
        in_specs=[sc, sc, sc, dc, dc, dc, slt, slt, slt, dl, dl, dl],
        out_specs=pl.BlockSpec((1, 1, C, w), lambda b, d, c: (d, b, lat_idx(d, c), 0)),
        out_shape=jax.ShapeDtypeStruct((2, bsz, L, w), F32),
        scratch_shapes=[pltpu.VMEM((w // LANE, LANE, LANE), F32)],
        compiler_params=_params("parallel", "parallel", "arbitrary"),
        name="rwkv_scan",
    )(rc, vc, kkc, lwc, kdc, bdc, rl, vl, kkl, lwl, kdl, bdl)


def _rwkv_out_kernel(y_ref, bonus_ref, g_ref, gg_ref, gb_ref, e_ref, o_ref):
    y = y_ref[0, 0] + y_ref[1, 0]
    e = e_ref[...]
    inv_n = 1.0 / RW_HEAD
    mu = _mm_exact_rhs(y, e) * inv_n
    yc = y - mu
    var = _mm_exact_rhs(yc * yc, e) * inv_n
    yn = yc * lax.rsqrt(var + RW_GN_EPS) * gg_ref[...] + gb_ref[...]
    o_ref[0] = (yn + bonus_ref[0]) * g_ref[0]


def rwkv_output(y, bonus, g, gn_g, gn_b):
    _, bsz, L, w = y.shape
    tm = _row_tile(L, 512)
    tok = pl.BlockSpec((1, tm, w), lambda b, i: (b, i, 0))
    vec = pl.BlockSpec((1, w), lambda b, i: (0, 0))
    return pl.pallas_call(
        _rwkv_out_kernel,
        grid=(bsz, L // tm),
        in_specs=[pl.BlockSpec((2, 1, tm, w), lambda b, i: (0, b, i, 0)), tok, tok, vec, vec,
                  pl.BlockSpec((w, w), lambda b, i: (0, 0))],
        out_specs=tok,
        out_shape=jax.ShapeDtypeStruct((bsz, L, w), F32),
        compiler_params=_params("parallel", "parallel"),
        name="rwkv_output",
    )(y, bonus, g, gn_g.reshape(1, w), gn_b.reshape(1, w), _head_indicator(w, RW_HEAD))


def rwkv7_mixer_lat(p_lat, p_ctx, mu, w0, w2, a0, a2, g2, k_k, k_a, r_k, gn_g, gn_b):
    args = (mu, w0, w2, a0, a2, g2, k_k, k_a, r_k)
    r_l, v_l, kk_l, g_l, bonus_l, lw_l, kd_l, bd_l = rwkv_features(p_lat, *args)
    r_c, v_c, kk_c, _, _, lw_c, kd_c, bd_c = rwkv_features(p_ctx, *args)
    y = rwkv_scan((r_c, v_c, kk_c, lw_c, kd_c, bd_c), (r_l, v_l, kk_l, lw_l, kd_l, bd_l))
    return rwkv_output(y, bonus_l, g_l, gn_g, gn_b)


def _gdn_qkv_kernel(p_ref, hp_ref, hn_ref, cw_ref, q_o, k_o, v_o, *, dn_w):
    x = p_ref[0]
    x_prev, x_next = _neighbours(x, *_halo_rows(hp_ref, hn_ref))
    cw = cw_ref[...]
    z = x_prev * cw[0:1] + x * cw[1:2] + x_next * cw[2:3]
    z = z * _sigmoid(z)
    for h in range(dn_w // DN_HEAD):
        sq = slice(h * DN_HEAD, (h + 1) * DN_HEAD)
        sk = slice(dn_w + h * DN_HEAD, dn_w + (h + 1) * DN_HEAD)
        q = z[:, sq]
        k = z[:, sk]
        q_o[0, :, sq] = q * (lax.rsqrt(jnp.sum(q * q, -1, keepdims=True) + 1e-6) * (DN_HEAD ** -0.5))
        k_o[0, :, sq] = k * lax.rsqrt(jnp.sum(k * k, -1, keepdims=True) + 1e-6)
    v_o[0] = z[:, 2 * dn_w:]


def _gdn_gate_kernel(g_ref, al_ref, dt_ref, gl_o, be_o):
    g = g_ref[0]
    gl_o[0] = -jnp.exp(al_ref[...]) * _softplus(g[:, :LANE] + dt_ref[...])
    be_o[0] = _sigmoid(g[:, LANE:])


def gdn_features(p, conv_w, A_log, dt_bias):
    bsz, T, cols = p.shape
    dn_w = conv_w.shape[1] // 3
    tm = _row_tile(T, 256)
    c3 = 3 * dn_w
    tok = pl.BlockSpec((1, tm, dn_w), lambda b, i: (b, i, 0))
    s1 = jax.ShapeDtypeStruct((bsz, T, dn_w), F32)
    q, k, v = pl.pallas_call(
        functools.partial(_gdn_qkv_kernel, dn_w=dn_w),
        grid=(bsz, T // tm),
        in_specs=_halo_specs(tm, T, c3) + [pl.BlockSpec((8, c3), lambda b, i: (0, 0))],
        out_specs=[tok, tok, tok],
        out_shape=[s1, s1, s1],
        compiler_params=_params("parallel", "parallel"),
        name="gdn_qkv",
    )(p, p, p, jnp.pad(conv_w, ((0, 8 - conv_w.shape[0]), (0, 0))))
    pad = lambda t: jnp.pad(t.reshape(1, -1).astype(F32), ((0, 0), (0, LANE - t.size)))
    assert (4 * dn_w) % (2 * LANE) == 0
    gcol = (4 * dn_w) // (2 * LANE)
    gt = pl.BlockSpec((1, tm, LANE), lambda b, i: (b, i, 0))
    sg = jax.ShapeDtypeStruct((bsz, T, LANE), F32)
    gl, be = pl.pallas_call(
        _gdn_gate_kernel,
        grid=(bsz, T // tm),
        in_specs=[pl.BlockSpec((1, tm, 2 * LANE), lambda b, i: (b, i, gcol)),
                  pl.BlockSpec((1, LANE), lambda b, i: (0, 0)), pl.BlockSpec((1, LANE), lambda b, i: (0, 0))],
        out_specs=[gt, gt],
        out_shape=[sg, sg],
        compiler_params=_params("parallel", "parallel"),
        name="gdn_gates",
    )(p, pad(A_log), pad(dt_bias))
    return q, k, v, gl, be


def _gdn_scan_kernel(qc, kc, vc, glc, bec, ql, kl, vl, gll, bel, o_ref, s_ref, *, n_ctx):
    d = pl.program_id(1)
    c = pl.program_id(2)
    C = CHUNK
    heads = s_ref.shape[0]

    @pl.when(c == 0)
    def _():
        s_ref[...] = jnp.zeros_like(s_ref)

    is_ctx = c < n_ctx
    pick = lambda a, b: jnp.where(is_ctx, a[0], b[0])
    q, k, v, gl, be = pick(qc, ql), pick(kc, kl), pick(vc, vl), pick(glc, gll), pick(bec, bel)
    rev = d == 1

    ti = lax.broadcasted_iota(jnp.int32, (C, C), 0)
    tj = lax.broadcasted_iota(jnp.int32, (C, C), 1)
    tri = jnp.where(jnp.where(rev, tj - ti, ti - tj) >= 0, 1.0, 0.0).astype(BF16)
    g_cum = _mm_exact_lhs(tri, gl)
    g_tot = jnp.sum(gl, axis=0, keepdims=True)
    lane = lax.broadcasted_iota(jnp.int32, (1, LANE), 1)

    def col(x, h):
        return jnp.sum(jnp.where(lane == d * heads + h, x, 0.0), axis=1, keepdims=True)

    n = 2 * C
    rid = lax.broadcasted_iota(jnp.int32, (n, n), 0)
    cid = lax.broadcasted_iota(jnp.int32, (n, n), 1)
    same = (rid >> 6) == (cid >> 6)
    diff = jnp.where(rev, cid - rid, rid - cid)
    keep_incl = jnp.where(same, jnp.where(diff >= 0, 1.0, 0.0), 0.0)
    eye = jnp.where(rid == cid, 1.0, 0.0)

    pairs = range(heads // 2)
    hs = lambda p, t: slice((2 * p + t) * DN_HEAD, (2 * p + t + 1) * DN_HEAD)
    stack = lambda x, p: jnp.concatenate([x[:, hs(p, 0)], x[:, hs(p, 1)]], axis=0)
    k_s = [stack(k, p) for p in pairs]
    q_s = [stack(q, p) for p in pairs]
    v_s = [stack(v, p) for p in pairs]
    g_col = [jnp.concatenate([col(g_cum, 2 * p), col(g_cum, 2 * p + 1)], axis=0) for p in pairs]
    b_col = [jnp.concatenate([col(be, 2 * p), col(be, 2 * p + 1)], axis=0) for p in pairs]
    ge_col = [jnp.concatenate([jnp.broadcast_to(col(g_tot, 2 * p), (C, 1)),
                               jnp.broadcast_to(col(g_tot, 2 * p + 1), (C, 1))], axis=0) for p in pairs]
    g_row = [jnp.broadcast_to(g_col[p], (n, n)).T for p in pairs]
    decay = [keep_incl * jnp.exp(keep_incl * (g_col[p] - g_row[p])) for p in pairs]
    kq = [_mm_nt(jnp.concatenate([k_s[p], q_s[p]], axis=0), k_s[p]) for p in pairs]
    a = [(1.0 - eye) * b_col[p] * kq[p][:n] * decay[p] for p in pairs]
    qk = [kq[p][n:] * decay[p] for p in pairs]
    t_inv = _unit_tri_inverse(a, rid, cid, C)
    eg = [jnp.exp(g_col[p]) for p in pairs]
    uw = [_mm(t_inv[p], jnp.concatenate([b_col[p] * v_s[p], (b_col[p] * eg[p]) * k_s[p]], axis=1)) for p in pairs]
    u0 = [t[:, :DN_HEAD] for t in uw]
    w = [t[:, DN_HEAD:] for t in uw]
    q_dec = [q_s[p] * eg[p] for p in pairs]
    k_dec_t = [_mm_nt(eye, k_s[p] * jnp.exp(ge_col[p] - g_col[p])) for p in pairs]
    head_cols = lax.broadcasted_iota(jnp.int32, (1, n), 1) >> 6

    st = [s_ref[h] for h in range(heads)]
    ws = [[_mm(jnp.concatenate([w[p][t * C:(t + 1) * C], q_dec[p][t * C:(t + 1) * C]], axis=0), st[2 * p + t])
           for t in range(2)] for p in pairs]
    u = [jnp.concatenate([u0[p][:C] - ws[p][0][:C], u0[p][C:] - ws[p][1][:C]], axis=0) for p in pairs]
    o = [jnp.concatenate([ws[p][0][C:], ws[p][1][C:]], axis=0) + _mm(qk[p], u[p]) for p in pairs]
    for p in pairs:
        for t in range(2):
            h = 2 * p + t
            g_end = jnp.exp(col(g_tot, h))
            kd = jnp.where(head_cols == t, k_dec_t[p], 0.0)
            s_ref[h] = g_end * st[h] + _mm(kd, u[p])
            o_ref[0, 0, :, hs(p, t)] = o[p][t * C:(t + 1) * C]


def gdn_scan(f_ctx, f_lat):
    qc, kc, vc, glc, bec = f_ctx
    ql, kl, vl, gll, bel = f_lat
    bsz, Lc, w = qc.shape
    L = ql.shape[1]
    C = CHUNK
    heads = w // DN_HEAD
    assert Lc % C == 0 and L % C == 0 and heads % 2 == 0
    nc, nl = Lc // C, L // C
    ctx_idx, lat_idx = _scan_index_maps(nc, nl)
    sc = pl.BlockSpec((1, C, w), lambda b, d, c: (b, ctx_idx(d, c), 0))
    gc = pl.BlockSpec((1, C, LANE), lambda b, d, c: (b, ctx_idx(d, c), 0))
    sl = pl.BlockSpec((1, C, w), lambda b, d, c: (b, lat_idx(d, c), 0))
    gl = pl.BlockSpec((1, C, LANE), lambda b, d, c: (b, lat_idx(d, c), 0))
    return pl.pallas_call(
        functools.partial(_gdn_scan_kernel, n_ctx=nc),
        grid=(bsz, 2, nc + nl),
        in_specs=[sc, sc, sc, gc, gc, sl, sl, sl, gl, gl],
        out_specs=pl.BlockSpec((1, 1, C, w), lambda b, d, c: (d, b, lat_idx(d, c), 0)),
        out_shape=jax.ShapeDtypeStruct((2, bsz, L, w), F32),
        scratch_shapes=[pltpu.VMEM((heads, DN_HEAD, DN_HEAD), F32)],
        compiler_params=_params("parallel", "parallel", "arbitrary"),
        name="gdn_scan",
    )(qc, kc, vc, glc, bec, ql, kl, vl, gll, bel)


def _gdn_out_kernel(o_ref, z_ref, ng_ref, y_ref):
    o = o_ref[0, 0] + o_ref[1, 0]
    z = z_ref[0]
    for h in range(o.shape[1] // DN_HEAD):
        sl = slice(h * DN_HEAD, (h + 1) * DN_HEAD)
        oh = o[:, sl]
        on = oh * lax.rsqrt(jnp.mean(oh * oh, -1, keepdims=True) + 1e-6) * ng_ref[...]
        zh = z[:, sl]
        y_ref[0, :, sl] = on * (zh * _sigmoid(zh))


def gdn_output(o, p, norm_g):
    _, bsz, L, w = o.shape
    tm = _row_tile(L, 512)
    z_block = 3
    tok = pl.BlockSpec((1, tm, w), lambda b, i: (b, i, 0))
    return pl.pallas_call(
        _gdn_out_kernel,
        grid=(bsz, L // tm),
        in_specs=[pl.BlockSpec((2, 1, tm, w), lambda b, i: (0, b, i, 0)),
                  pl.BlockSpec((1, tm, w), lambda b, i: (b, i, z_block)),
                  pl.BlockSpec((1, DN_HEAD), lambda b, i: (0, 0))],
        out_specs=tok,
        out_shape=jax.ShapeDtypeStruct((bsz, L, w), F32),
        compiler_params=_params("parallel", "parallel"),
        name="gdn_output",
    )(o, p, norm_g.reshape(1, DN_HEAD))


def gdn_mixer_lat(p_lat, p_ctx, conv_w, A_log, dt_bias, norm_g):
    f_lat = gdn_features(p_lat, conv_w, A_log, dt_bias)
    f_ctx = gdn_features(p_ctx, conv_w, A_log, dt_bias)
    return gdn_output(gdn_scan(f_ctx, f_lat), p_lat, norm_g)


def _gdn_in_weight(w_dn, dn_w, heads):
    padw = ((0, 0), (0, LANE - 2 * heads))
    g = w_dn[:, 4 * dn_w:]
    return jnp.concatenate([w_dn[:, :4 * dn_w], jnp.pad(g[:, :2 * heads], padw), jnp.pad(g[:, 2 * heads:], padw)], 1)


def kernel(x, c, ctx, c_ctx, mod_w, mod_b, ln_g, ln_b, mlp_w1, mlp_w2, e_w_in, e_w_out, hy_conv, hy_ffn_w1,
           hy_ffn_b1, hy_ffn_w2, hy_ffn_b2, hy_sin_freq, hy_ffn_w3, hy_decay, hy_bias, attn_sink, o_w_in,
           o_w_out, rw_mu, rw_w0, rw_w2, rw_a0, rw_a2, rw_g2, rw_kk, rw_ka, rw_rk, rw_lnx_g, rw_lnx_b,
           dn_conv, dn_A_log, dn_dt_bias, dn_norm_g):
    bsz, L, d = x.shape
    assert mod_w.shape[0] == DEPTH == 2
    cc = jnp.concatenate([c, c_ctx[None, :]], axis=0)
    cc = jnp.pad(cc, ((0, (-cc.shape[0]) % 8), (0, 0)))

    def modulation(i):
        m = mod_vectors(cc, mod_w[i], mod_b[i])
        m_lat = m[:bsz].reshape(bsz, 1, N_MOD, d)
        m_ctx = jnp.broadcast_to(m[bsz].reshape(1, 1, N_MOD, d), (bsz, 1, N_MOD, d))
        return [m_lat[:, :, t] for t in range(N_MOD)], [m_ctx[:, :, t] for t in range(N_MOD)]

    (sh1, sc1, g1, sh2, sc2, g2), (csh1, csc1, cg1, csh2, csc2, cg2) = modulation(0)
    hy_cols = hy_conv.shape[2]
    w_in = e_w_in[0].astype(BF16)
    ws = [w_in[:, :hy_cols], w_in[:, hy_cols:]]
    w_out = e_w_out[0].astype(BF16)
    w1, w2 = mlp_w1[0].astype(BF16), mlp_w2[0].astype(BF16)
    hy = (hy_conv[0], hy_ffn_w1[0], hy_ffn_b1[0], hy_ffn_w2[0], hy_ffn_b2[0], hy_sin_freq[0], hy_ffn_w3[0],
          hy_decay[0], hy_bias[0])
    p_hy, p_qkv = mod_linear(x, sc1, sh1, ws)
    pc_hy, pc_qkv = mod_linear(ctx, csc1, csh1, ws)
    y_a = hyena_mixer(p_hy, *hy)
    y_b = windowed_attention(p_qkv, pc_qkv, attn_sink[0])
    yc_a = hyena_mixer(pc_hy, *hy)
    yc_b = context_attention(pc_qkv, attn_sink[0])
    x = out_proj_ln(y_a, y_b, w_out, x, g1, ln_g[0, 0], ln_b[0, 0])
    x = mlp_ln(x, sc2, sh2, g2, w1, w2, ln_g[0, 1], ln_b[0, 1])
    ctx = out_proj_ln(yc_a, yc_b, w_out, ctx, cg1, ln_g[0, 0], ln_b[0, 0])
    ctx = mlp_ln(ctx, csc2, csh2, cg2, w1, w2, ln_g[0, 1], ln_b[0, 1])

    (sh1, sc1, g1, sh2, sc2, g2), (csh1, csc1, _, _, _, _) = modulation(1)
    rw_cols = rw_mu.shape[1]
    dn_w = dn_conv.shape[2] // 3
    w_in = o_w_in[0]
    ws = [w_in[:, :rw_cols].astype(BF16), _gdn_in_weight(w_in[:, rw_cols:], dn_w, dn_w // DN_HEAD).astype(BF16)]
    w_out = o_w_out[0].astype(BF16)
    w1, w2 = mlp_w1[1].astype(BF16), mlp_w2[1].astype(BF16)
    p_rw, p_dn = mod_linear(x, sc1, sh1, ws)
    pc_rw, pc_dn = mod_linear(ctx, csc1, csh1, ws)
    y_c = rwkv7_mixer_lat(p_rw, pc_rw, rw_mu[0], rw_w0[0], rw_w2[0], rw_a0[0], rw_a2[0], rw_g2[0], rw_kk[0],
                          rw_ka[0], rw_rk[0], rw_lnx_g[0], rw_lnx_b[0])
    y_d = gdn_mixer_lat(p_dn, pc_dn, dn_conv[0], dn_A_log[0], dn_dt_bias[0], dn_norm_g[0])
    x = out_proj_ln(y_c, y_d, w_out, x, g1, ln_g[1, 0], ln_b[1, 0])
    x = mlp_ln(x, sc2, sh2, g2, w1, w2, ln_g[1, 1], ln_b[1, 1])
    return x
```

```python
import functools
import math

import jax
import jax.numpy as jnp
from jax import lax
from jax.experimental import pallas as pl
from jax.experimental.pallas import tpu as pltpu

F32 = jnp.float32
BF16 = jnp.bfloat16

DEPTH = 2
N_MOD = 6
DEEPNORM_ALPHA = (2.0 * DEPTH) ** 0.25
LN_EPS = 1e-5

HY_BANDS = 16
HEAD_DIM = 64
SWA_KV_HEADS = 2
WINDOW = 128
SWA_BLOCK = 128
SWA_QUERY_TILE = 128
GRID_W = 64
ROPE_BASE = 10000.0

RW_HEAD = 64
RW_DECAY_LORA = 64
RW_AAA_LORA = 64
RW_GN_EPS = 64e-5

DN_HEAD = 128
CHUNK = 64
RW_SCAN_BATCH = 4
DN_SCAN_BATCH = 8

LANE = 128
MXU_WIDTH = 256
VMEM_LIMIT = 56 * 1024 * 1024


def _row_tile(n, cap):
    t = min(n, cap)
    while n % t:
        t //= 2
    return t


def _params(*sem):
    return pltpu.CompilerParams(dimension_semantics=sem, vmem_limit_bytes=VMEM_LIMIT)


def _mm(a, b):
    return jnp.dot(a.astype(BF16), b.astype(BF16), preferred_element_type=F32)


def _mm_nt(a, b):
    return lax.dot_general(a.astype(BF16), b.astype(BF16), (((1,), (1,)), ((), ())), preferred_element_type=F32)


def _split2(x):
    hi = x.astype(BF16)
    return hi, (x - hi.astype(F32)).astype(BF16)


def _split3(x):
    hi, r = x.astype(BF16), None
    r = x - hi.astype(F32)
    mid = r.astype(BF16)
    return hi, mid, (r - mid.astype(F32)).astype(BF16)


def _mm_exact_lhs(a_bf16, x):
    return sum(jnp.dot(a_bf16, t, preferred_element_type=F32) for t in _split3(x))


def _mm_exact_rhs(x, b_bf16):
    return sum(jnp.dot(t, b_bf16, preferred_element_type=F32) for t in _split3(x))


def _mm_f32(a, b):
    ah, al = _split2(a)
    bh, bl = _split2(b)
    dot = lambda x, y: jnp.dot(x, y, preferred_element_type=F32)
    return dot(ah, bh) + dot(ah, bl) + dot(al, bh)


def _sigmoid(x):
    return 1.0 / (1.0 + jnp.exp(-x))


def _softplus(x):
    return jnp.maximum(x, 0.0) + jnp.log(1.0 + jnp.exp(-jnp.abs(x)))


def _neighbours(x, prev_row, next_row):
    n = x.shape[0]
    rows = lax.broadcasted_iota(jnp.int32, (n, 1), 0)
    x_prev = jnp.where(rows == 0, prev_row, pltpu.roll(x, 1, axis=0))
    x_next = jnp.where(rows == n - 1, next_row, pltpu.roll(x, n - 1, axis=0))
    return x_prev, x_next


def _halo_rows(hp_ref, hn_ref):
    i = pl.program_id(1)
    nt = pl.num_programs(1)
    return jnp.where(i > 0, hp_ref[0][7:8, :], 0.0), jnp.where(i < nt - 1, hn_ref[0][0:1, :], 0.0)


def _halo_specs(tm, T, cols, col_block=0):
    hb = tm // 8
    return [pl.BlockSpec((1, tm, cols), lambda b, i: (b, i, col_block)),
            pl.BlockSpec((1, 8, cols), lambda b, i: (b, jnp.maximum(i * hb - 1, 0), col_block)),
            pl.BlockSpec((1, 8, cols), lambda b, i: (b, jnp.minimum((i + 1) * hb, T // 8 - 1), col_block))]


def _unit_tri_inverse(a_list, rid, cid, blk):
    eye = jnp.where(rid == cid, 1.0, 0.0)
    pair = (rid >> 1) == (cid >> 1)
    d_list = [eye - jnp.where(pair, a, 0.0) for a in a_list]
    s, sh = 2, 1
    while s < blk:
        join = jnp.where((rid >> (sh + 1)) == (cid >> (sh + 1)), jnp.where((rid >> sh) != (cid >> sh), 1.0, 0.0), 0.0)
        dq = [_mm(d, a * join) for d, a in zip(d_list, a_list)]
        d_list = [d - _mm(t, d) for t, d in zip(dq, d_list)]
        s, sh = 2 * s, sh + 1
    return d_list


def _scan_index_maps(nc, nl):
    def ctx_idx(d, c):
        cc = jnp.minimum(c, nc - 1)
        return jnp.where(d == 1, nc - 1 - cc, cc)

    def lat_idx(d, c):
        cl = jnp.maximum(c - nc, 0)
        return jnp.where(d == 1, nl - 1 - cl, cl)

    return ctx_idx, lat_idx


def _mod_kernel(c_ref, w_ref, b_ref, o_ref):
    c = c_ref[...]
    o_ref[...] = _mm(c * _sigmoid(c), w_ref[...]) + b_ref[...]


def mod_vectors(cc, w, b):
    r, d = cc.shape
    n = w.shape[1]
    tn = _row_tile(n, 1024)
    return pl.pallas_call(
        _mod_kernel,
        grid=(n // tn,),
        in_specs=[pl.BlockSpec((r, d), lambda j: (0, 0)),
                  pl.BlockSpec((d, tn), lambda j: (0, j)),
                  pl.BlockSpec((1, tn), lambda j: (0, j))],
        out_specs=pl.BlockSpec((r, tn), lambda j: (0, j)),
        out_shape=jax.ShapeDtypeStruct((r, n), F32),
        name="mod_vectors",
    )(cc, w, b.reshape(1, n))


def _mod_linear_kernel(x_ref, sc_ref, sh_ref, *refs):
    n = len(refs) // 2
    h = (x_ref[0] * (1.0 + sc_ref[0]) + sh_ref[0]).astype(BF16)
    for w_ref, o_ref in zip(refs[:n], refs[n:]):
        o_ref[0] = jnp.dot(h, w_ref[...], preferred_element_type=F32)


def mod_linear(x, sc, sh, ws):
    bsz, L, d = x.shape
    tm = _row_tile(L, 512)
    vec = pl.BlockSpec((1, 1, d), lambda b, i: (b, 0, 0))
    return pl.pallas_call(
        _mod_linear_kernel,
        grid=(bsz, L // tm),
        in_specs=[pl.BlockSpec((1, tm, d), lambda b, i: (b, i, 0)), vec, vec]
        + [pl.BlockSpec(w.shape, lambda b, i: (0, 0)) for w in ws],
        out_specs=[pl.BlockSpec((1, tm, w.shape[1]), lambda b, i: (b, i, 0)) for w in ws],
        out_shape=[jax.ShapeDtypeStruct((bsz, L, w.shape[1]), F32) for w in ws],
        compiler_params=_params("parallel", "parallel"),
        name="mod_linear",
    )(x, sc, sh, *ws)


def _layer_norm_rows(z, g, b):
    mu = jnp.mean(z, -1, keepdims=True)
    zc = z - mu
    var = jnp.mean(zc * zc, -1, keepdims=True)
    return zc * lax.rsqrt(var + LN_EPS) * g + b


def _out_ln_kernel(ya_ref, yb_ref, wa_ref, wb_ref, x_ref, g_ref, lg_ref, lb_ref, o_ref):
    y = _mm(ya_ref[0], wa_ref[...]) + _mm(yb_ref[0], wb_ref[...])
    z = DEEPNORM_ALPHA * x_ref[0] + g_ref[0] * y
    o_ref[0] = _layer_norm_rows(z, lg_ref[...], lb_ref[...])


def out_proj_ln(ya, yb, w, x, gate, ln_g, ln_b):
    bsz, L, d = x.shape
    ka, kb = ya.shape[-1], yb.shape[-1]
    tm = _row_tile(L, 512)
    row = pl.BlockSpec((1, d), lambda b, i: (0, 0))
    return pl.pallas_call(
        _out_ln_kernel,
        grid=(bsz, L // tm),
        in_specs=[pl.BlockSpec((1, tm, ka), lambda b, i: (b, i, 0)),
                  pl.BlockSpec((1, tm, kb), lambda b, i: (b, i, 0)),
                  pl.BlockSpec((ka, d), lambda b, i: (0, 0)),
                  pl.BlockSpec((kb, d), lambda b, i: (0, 0)),
                  pl.BlockSpec((1, tm, d), lambda b, i: (b, i, 0)),
                  pl.BlockSpec((1, 1, d), lambda b, i: (b, 0, 0)), row, row],
        out_specs=pl.BlockSpec((1, tm, d), lambda b, i: (b, i, 0)),
        out_shape=jax.ShapeDtypeStruct((bsz, L, d), F32),
        compiler_params=_params("parallel", "parallel"),
        name="out_proj_ln",
    )(ya, yb, w[:ka], w[ka:], x, gate, ln_g.reshape(1, d), ln_b.reshape(1, d))


def _mlp_ln_kernel(x_ref, sc_ref, sh_ref, g_ref, w1_ref, w2_ref, lg_ref, lb_ref, o_ref, *, ff_tile):
    x = x_ref[0]
    h = (x * (1.0 + sc_ref[0]) + sh_ref[0]).astype(BF16)
    d_ff = w1_ref.shape[1]
    acc = jnp.zeros(x.shape, F32)
    for c in range(d_ff // ff_tile):
        a = jnp.dot(h, w1_ref[:, c * ff_tile:(c + 1) * ff_tile], preferred_element_type=F32)
        a = jnp.maximum(a, 0.0)
        a = (a * a).astype(BF16)
        acc = acc + jnp.dot(a, w2_ref[c * ff_tile:(c + 1) * ff_tile, :], preferred_element_type=F32)
    z = DEEPNORM_ALPHA * x + g_ref[0] * acc
    o_ref[0] = _layer_norm_rows(z, lg_ref[...], lb_ref[...])


def mlp_ln(x, sc, sh, gate, w1, w2, ln_g, ln_b):
    bsz, L, d = x.shape
    d_ff = w1.shape[1]
    tm = _row_tile(L, 512)
    ff_tile = _row_tile(d_ff, 1024)
    once = dict(pipeline_mode=pl.Buffered(1))
    vec = pl.BlockSpec((1, 1, d), lambda b, i: (b, 0, 0))
    row = pl.BlockSpec((1, d), lambda b, i: (0, 0))
    return pl.pallas_call(
        functools.partial(_mlp_ln_kernel, ff_tile=ff_tile),
        grid=(bsz, L // tm),
        in_specs=[pl.BlockSpec((1, tm, d), lambda b, i: (b, i, 0)), vec, vec, vec,
                  pl.BlockSpec((d, d_ff), lambda b, i: (0, 0), **once),
                  pl.BlockSpec((d_ff, d), lambda b, i: (0, 0), **once), row, row],
        out_specs=pl.BlockSpec((1, tm, d), lambda b, i: (b, i, 0)),
        out_shape=jax.ShapeDtypeStruct((bsz, L, d), F32),
        compiler_params=_params("parallel", "parallel"),
        name="mlp_ln",
    )(x, sc, sh, gate, w1, w2, ln_g.reshape(1, d), ln_b.reshape(1, d))


def dft_tables(T):
    n = 2 * T
    f = jnp.arange(T, dtype=jnp.int32)
    ang = ((f[:, None] * f[None, :]) % n).astype(F32) * (2.0 * math.pi / n)
    return jnp.cos(ang).astype(BF16), (-jnp.sin(ang)).astype(BF16)


def hyena_pos_features(T):
    t = jnp.arange(T, dtype=F32)
    t_norm = t / max(T - 1, 1)
    bands = jnp.linspace(1e-4, HY_BANDS - 1, HY_BANDS, dtype=F32)
    ang = 2.0 * math.pi * t[:, None] * bands[None, :] / T
    pe = jnp.concatenate([t_norm[:, None], jnp.cos(ang), -jnp.sin(ang)], axis=-1)
    return jnp.pad(pe, ((0, 0), (0, (-pe.shape[1]) % 8)))


def _hy_filter_kernel(pe_ref, w1_ref, b1_ref, w2_ref, b2_ref, fr_ref, w3f_ref, w3b_ref, df_ref, db_ref,
                      cm_ref, sm_ref, kr_o, ki_o, kn_o):
    T = pe_ref.shape[0]
    pe = pe_ref[...]
    fr = fr_ref[...]
    h = jnp.sin(fr * (_mm_f32(pe, w1_ref[...]) + b1_ref[...]))
    h = jnp.sin(fr * (_mm_f32(h, w2_ref[...]) + b2_ref[...]))
    t_norm = pe[:, 0:1]
    h_f = _mm_f32(h, w3f_ref[...]) * jnp.exp(-t_norm * jnp.abs(df_ref[...]))
    h_b = _mm_f32(h, w3b_ref[...]) * jnp.exp(-t_norm * jnp.abs(db_ref[...]))
    row = lax.broadcasted_iota(jnp.int32, (T, 1), 0)
    h_b = jnp.where(row == 0, 0.0, h_b)
    scale = jnp.where(row == 0, 1.0, 2.0) * (1.0 / (2 * T))
    kr_o[...] = _mm(cm_ref[...], h_f + h_b) * scale
    ki_o[...] = _mm(sm_ref[...], h_f - h_b) * scale
    sign = jnp.where((row & 1) == 0, 1.0, -1.0)
    kn = jnp.sum(sign * (h_f + h_b), axis=0, keepdims=True) * (1.0 / (2 * T))
    kn_o[...] = jnp.broadcast_to(kn, kn_o.shape)


def hyena_spectrum(T, cm, sm, w1, b1, w2, b2, freq, w3, decay):
    hy_w = w3.shape[1] // 2
    ct = MXU_WIDTH
    nj = hy_w // ct
    pe = hyena_pos_features(T)
    w1p = jnp.pad(w1, ((0, pe.shape[1] - w1.shape[0]), (0, 0)))
    row = lambda t: t.reshape(1, -1)
    full = lambda a: pl.BlockSpec(a.shape, lambda j: (0,) * a.ndim)
    ins = (pe, w1p, row(b1), w2, row(b2), row(freq))
    return pl.pallas_call(
        _hy_filter_kernel,
        grid=(nj,),
        in_specs=[full(a) for a in ins] + [
            pl.BlockSpec((w3.shape[0], ct), lambda j: (0, j)), pl.BlockSpec((w3.shape[0], ct), lambda j: (0, nj + j)),
            pl.BlockSpec((1, ct), lambda j: (0, j)), pl.BlockSpec((1, ct), lambda j: (0, nj + j)),
            pl.BlockSpec((T, T), lambda j: (0, 0)), pl.BlockSpec((T, T), lambda j: (0, 0))],
        out_specs=[pl.BlockSpec((T, ct), lambda j: (0, j)), pl.BlockSpec((T, ct), lambda j: (0, j)),
                   pl.BlockSpec((8, ct), lambda j: (0, j))],
        out_shape=[jax.ShapeDtypeStruct((T, hy_w), F32), jax.ShapeDtypeStruct((T, hy_w), F32),
                   jax.ShapeDtypeStruct((8, hy_w), F32)],
        compiler_params=_params("arbitrary"),
        name="hyena_spectrum",
    )(*ins, w3, w3, row(decay), row(decay), cm, sm)


def _hy_conv_kernel(x0_ref, x1_ref, xv_ref, c0_ref, c1_ref, cv_ref, bias_ref, kr_ref, ki_ref, kn_ref,
                    cm_ref, sm_ref, o_ref):
    T = x0_ref.shape[1]
    row = lax.broadcasted_iota(jnp.int32, (T, 1), 0)
    zero = jnp.zeros((1, x0_ref.shape[2]), F32)

    def short_conv(x_ref, c_ref):
        x = x_ref[0]
        c = c_ref[...]
        x_prev, x_next = _neighbours(x, zero, zero)
        return x_prev * c[0:1] + x * c[1:2] + x_next * c[2:3]

    u = short_conv(x1_ref, c1_ref) * short_conv(xv_ref, cv_ref)
    ub = u.astype(BF16)
    sign = jnp.where((row & 1) == 0, 1.0, -1.0)
    un = jnp.sum(sign * u, axis=0, keepdims=True)
    y = sign * (un * kn_ref[0:1, :]) + u * bias_ref[...]
    ft = min(T, 512)
    for f0 in range(0, T, ft):
        fs = slice(f0, f0 + ft)
        ur = jnp.dot(cm_ref[fs, :], ub, preferred_element_type=F32)
        ui = jnp.dot(sm_ref[fs, :], ub, preferred_element_type=F32)
        kr = kr_ref[fs, :]
        ki = ki_ref[fs, :]
        yr = (ur * kr - ui * ki).astype(BF16)
        yi = (ur * ki + ui * kr).astype(BF16)
        y = y + jnp.dot(cm_ref[:, fs], yr, preferred_element_type=F32) + jnp.dot(sm_ref[:, fs], yi,
                                                                                 preferred_element_type=F32)
    o_ref[0] = short_conv(x0_ref, c0_ref) * y


def hyena_mixer(p, conv_w, w1, b1, w2, b2, freq, w3, decay, bias):
    bsz, T, _ = p.shape
    hy_w = bias.shape[0]
    ct = MXU_WIDTH
    nj = hy_w // ct
    cm, sm = dft_tables(T)
    kr, ki, kn = hyena_spectrum(T, cm, sm, w1, b1, w2, b2, freq, w3, decay)
    cw = jnp.pad(conv_w, ((0, 8 - conv_w.shape[0]), (0, 0)))
    once = dict(pipeline_mode=pl.Buffered(1))
    xs = lambda part: pl.BlockSpec((1, T, ct), lambda j, b: (b, 0, part * nj + j))
    cs = lambda part: pl.BlockSpec((8, ct), lambda j, b: (0, part * nj + j))
    return pl.pallas_call(
        _hy_conv_kernel,
        grid=(nj, bsz),
        in_specs=[xs(0), xs(1), xs(2), cs(0), cs(1), cs(2),
                  pl.BlockSpec((1, ct), lambda j, b: (0, j)),
                  pl.BlockSpec((T, ct), lambda j, b: (0, j), **once),
                  pl.BlockSpec((T, ct), lambda j, b: (0, j), **once),
                  pl.BlockSpec((8, ct), lambda j, b: (0, j)),
                  pl.BlockSpec((T, T), lambda j, b: (0, 0), **once),
                  pl.BlockSpec((T, T), lambda j, b: (0, 0), **once)],
        out_specs=pl.BlockSpec((1, T, ct), lambda j, b: (b, 0, j)),
        out_shape=jax.ShapeDtypeStruct((bsz, T, hy_w), F32),
        compiler_params=_params("parallel", "parallel"),
        name="hyena_conv",
    )(p, p, p, cw, cw, cw, bias.reshape(1, hy_w), kr, ki, kn, cm, sm)


def rope_tables(L):
    quarter = HEAD_DIM // 4
    pos = jnp.arange(L, dtype=jnp.int32)
    inv_freq = ROPE_BASE ** (-jnp.arange(quarter, dtype=F32) / quarter)
    a_row = (pos // GRID_W).astype(F32)[:, None] * inv_freq[None, :]
    a_col = (pos % GRID_W).astype(F32)[:, None] * inv_freq[None, :]
    ang = jnp.concatenate([a_row, a_row, a_col, a_col], axis=-1)
    sgn = jnp.tile(jnp.concatenate([-jnp.ones(quarter, F32), jnp.ones(quarter, F32)]), 2)
    return jnp.tile(jnp.cos(ang), (1, 2)), jnp.tile(jnp.sin(ang) * sgn, (1, 2))


def _rope(t, cos, sin, first_quarter):
    w = t.shape[1]
    partner = jnp.where(first_quarter, pltpu.roll(t, w - 16, axis=1), pltpu.roll(t, 16, axis=1))
    return t * cos + partner * sin


def _attn_kernel(*refs, has_local, L, span, group):
    if has_local:
        q_ref, kv_ref, kvc_ref, cos_ref, sin_ref, sink_ref, o_ref = refs
    else:
        q_ref, kvc_ref, sink_ref, o_ref = refs
    i = pl.program_id(1)
    tq = q_ref.shape[1]
    lane = lax.broadcasted_iota(jnp.int32, (1, LANE), 1)
    low = lane < HEAD_DIM
    q = q_ref[0]
    n_heads = q.shape[1] // HEAD_DIM
    kvc = kvc_ref[0]
    kc, vc = kvc[:, :LANE], kvc[:, LANE:]
    if has_local:
        fq = (lane & 31) < 16
        q0 = pl.multiple_of(i * tq, tq)
        cos_q = cos_ref[pl.ds(q0, tq), :]
        sin_q = sin_ref[pl.ds(q0, tq), :]
        k0 = pl.multiple_of(jnp.clip(i * tq - WINDOW, 0, L - span), SWA_BLOCK)
        kv = kv_ref[0, pl.ds(k0, span), :]
        k = _rope(kv[:, :LANE], cos_ref[pl.ds(k0, span), :], sin_ref[pl.ds(k0, span), :], fq)
        v = kv[:, LANE:]
        qpos = q0 + lax.broadcasted_iota(jnp.int32, (tq, span), 0)
        kpos = k0 + lax.broadcasted_iota(jnp.int32, (tq, span), 1)
        ok1 = jnp.abs(qpos - kpos) <= WINDOW
        ok = jnp.concatenate([ok1] * group, axis=0)
    scale = HEAD_DIM ** -0.5

    def dup(x, g):
        sw = pltpu.roll(x, HEAD_DIM, axis=1)
        return jnp.where(low, x, sw) if g == 0 else jnp.where(low, sw, x)

    outs = []
    for g in range(SWA_KV_HEADS):
        rows, sinks = [], []
        for hh in range(group):
            h = g * group + hh
            j = h // 2
            q2 = q[:, j * LANE:(j + 1) * LANE]
            if has_local:
                q2 = _rope(q2, cos_q, sin_q, fq)
            q2 = q2 * scale
            rows.append(jnp.where(low, q2, 0.0) if h % 2 == 0 else jnp.where(low, 0.0, q2))
            sinks.append(jnp.broadcast_to(sink_ref[h:h + 1, 0:1], (tq, 1)))
        qs = jnp.concatenate(rows, axis=0)
        sk = jnp.concatenate(sinks, axis=0)
        s_ctx = _mm_nt(qs, dup(kc, g))
        m = jnp.maximum(jnp.max(s_ctx, -1, keepdims=True), sk)
        if has_local:
            s_loc = jnp.where(ok, _mm_nt(qs, dup(k, g)), -jnp.inf)
            m = jnp.maximum(m, jnp.max(s_loc, -1, keepdims=True))
            p_loc = jnp.exp(s_loc - m)
        p_ctx = jnp.exp(s_ctx - m)
        den = jnp.sum(p_ctx, -1, keepdims=True) + jnp.exp(sk - m)
        acc = _mm(p_ctx, dup(vc, g))
        if has_local:
            den = den + jnp.sum(p_loc, -1, keepdims=True)
            acc = acc + _mm(p_loc, dup(v, g))
        outs.append(acc / den)
    for j in range(n_heads // 2):
        h0, h1 = 2 * j, 2 * j + 1
        a = outs[h0 // group][(h0 % group) * tq:(h0 % group + 1) * tq]
        b = outs[h1 // group][(h1 % group) * tq:(h1 % group + 1) * tq]
        o_ref[0, :, j * LANE:(j + 1) * LANE] = jnp.where(low, a, b)


def _sink_rows(sink):
    return jnp.broadcast_to(sink.astype(F32)[:, None], (sink.shape[0], LANE))


def windowed_attention(p_lat, p_ctx, sink):
    bsz, L, cols = p_lat.shape
    Lc = p_ctx.shape[1]
    q_cols = cols - 2 * LANE
    n_heads = q_cols // HEAD_DIM
    tq = _row_tile(L, SWA_QUERY_TILE)
    span = min(L, tq + 2 * WINDOW)
    assert q_cols % (2 * LANE) == 0 and tq % SWA_BLOCK == 0
    kvb = q_cols // (2 * LANE)
    cos, sin = rope_tables(L)
    return pl.pallas_call(
        functools.partial(_attn_kernel, has_local=True, L=L, span=span, group=n_heads // SWA_KV_HEADS),
        grid=(bsz, L // tq),
        in_specs=[pl.BlockSpec((1, tq, q_cols), lambda b, i: (b, i, 0)),
                  pl.BlockSpec((1, L, 2 * LANE), lambda b, i: (b, 0, kvb)),
                  pl.BlockSpec((1, Lc, 2 * LANE), lambda b, i: (b, 0, kvb)),
                  pl.BlockSpec((L, LANE), lambda b, i: (0, 0)),
                  pl.BlockSpec((L, LANE), lambda b, i: (0, 0)),
                  pl.BlockSpec((n_heads, LANE), lambda b, i: (0, 0))],
        out_specs=pl.BlockSpec((1, tq, q_cols), lambda b, i: (b, i, 0)),
        out_shape=jax.ShapeDtypeStruct((bsz, L, q_cols), F32),
        compiler_params=_params("parallel", "parallel"),
        name="windowed_attention",
    )(p_lat, p_lat, p_ctx, cos, sin, _sink_rows(sink))


def context_attention(p_ctx, sink):
    bsz, Lc, cols = p_ctx.shape
    q_cols = cols - 2 * LANE
    n_heads = q_cols // HEAD_DIM
    tq = min(Lc, SWA_BLOCK)
    kvb = q_cols // (2 * LANE)
    return pl.pallas_call(
        functools.partial(_attn_kernel, has_local=False, L=Lc, span=0, group=n_heads // SWA_KV_HEADS),
        grid=(bsz, Lc // tq),
        in_specs=[pl.BlockSpec((1, tq, q_cols), lambda b, i: (b, i, 0)),
                  pl.BlockSpec((1, Lc, 2 * LANE), lambda b, i: (b, 0, kvb)),
                  pl.BlockSpec((n_heads, LANE), lambda b, i: (0, 0))],
        out_specs=pl.BlockSpec((1, tq, q_cols), lambda b, i: (b, i, 0)),
        out_shape=jax.ShapeDtypeStruct((bsz, Lc, q_cols), F32),
        compiler_params=_params("parallel", "parallel"),
        name="context_attention",
    )(p_ctx, p_ctx, _sink_rows(sink))


def _rwkv_feat_kernel(p_ref, hp_ref, hn_ref, mu_ref, w0_ref, w2_ref, a0_ref, a2_ref, g2_ref, kk_ref, ka_ref,
                      rk_ref, e_ref, r_o, v_o, kkn_o, g_o, bonus_o, lw_o, kd_o, bd_o, *, rw_w):
    p = p_ref[0]
    p_prev, p_next = _neighbours(p, *_halo_rows(hp_ref, hn_ref))
    p = p + mu_ref[...] * (0.5 * (p_prev + p_next) - p)
    o1, o2, o3 = rw_w, 2 * rw_w, 3 * rw_w
    o4 = o3 + 2 * RW_DECAY_LORA
    o5 = o4 + 2 * RW_AAA_LORA
    r, k, v = p[:, :o1], p[:, o1:o2], p[:, o2:o3]
    wd, ad, gd = p[:, o3:o4], p[:, o4:o5], p[:, o5:]
    w_log = -_softplus(-(w0_ref[...] + _mm(jnp.tanh(wd), w2_ref[...]))) - 0.5
    lw = -jnp.exp(w_log)
    a = _sigmoid(a0_ref[...] + _mm(ad, a2_ref[...]))
    e = e_ref[...]
    kq = k * kk_ref[...]
    kkn = kq * lax.rsqrt(_mm_exact_rhs(kq * kq, e) + 1e-6)
    r_o[0] = r
    v_o[0] = v
    kkn_o[0] = kkn
    g_o[0] = _mm(_sigmoid(gd), g2_ref[...])
    ka = ka_ref[...]
    rk = r * rk_ref[...]
    acc = jnp.zeros_like(r)
    for d in range(2):
        a_d = a[:, d * rw_w:(d + 1) * rw_w]
        k_d = k * (1.0 + (a_d - 1.0) * ka)
        lw_o[d, 0] = lw[:, d * rw_w:(d + 1) * rw_w]
        kd_o[d, 0] = k_d
        bd_o[d, 0] = kkn * a_d
        acc = acc + rk * k_d
    bonus_o[0] = _mm_exact_rhs(acc, e) * v


def _head_indicator(w, head):
    hid = jnp.arange(w) // head
    return (hid[:, None] == hid[None, :]).astype(BF16)


def rwkv_features(p, mu, w0, w2, a0, a2, g2, k_k, k_a, r_k):
    bsz, T, cols = p.shape
    rw_w = g2.shape[1]
    tm = _row_tile(T, 256)
    w2_bd = jnp.zeros((2 * RW_DECAY_LORA, 2 * rw_w), F32)
    w2_bd = w2_bd.at[:RW_DECAY_LORA, :rw_w].set(w2[0]).at[RW_DECAY_LORA:, rw_w:].set(w2[1]).astype(BF16)
    a2_bd = jnp.zeros((2 * RW_AAA_LORA, 2 * rw_w), F32)
    a2_bd = a2_bd.at[:RW_AAA_LORA, :rw_w].set(a2[0]).at[RW_AAA_LORA:, rw_w:].set(a2[1]).astype(BF16)
    vec = lambda t: t.reshape(1, -1).astype(F32)
    full = lambda shape: pl.BlockSpec(shape, lambda b, i: (0,) * len(shape))
    tok = pl.BlockSpec((1, tm, rw_w), lambda b, i: (b, i, 0))
    tok2 = pl.BlockSpec((2, 1, tm, rw_w), lambda b, i: (0, b, i, 0))
    s1 = jax.ShapeDtypeStruct((bsz, T, rw_w), F32)
    s2 = jax.ShapeDtypeStruct((2, bsz, T, rw_w), F32)
    return pl.pallas_call(
        functools.partial(_rwkv_feat_kernel, rw_w=rw_w),
        grid=(bsz, T // tm),
        in_specs=_halo_specs(tm, T, cols) + [
            full((1, cols)), full((1, 2 * rw_w)), full(w2_bd.shape), full((1, 2 * rw_w)),
            full(a2_bd.shape), full(g2.shape), full((1, rw_w)), full((1, rw_w)), full((1, rw_w)),
            full((rw_w, rw_w))],
        out_specs=[tok, tok, tok, tok, tok, tok2, tok2, tok2],
        out_shape=[s1, s1, s1, s1, s1, s2, s2, s2],
        compiler_params=_params("parallel", "parallel"),
        name="rwkv_features",
    )(p, p, p, vec(mu), vec(w0), w2_bd, vec(a0), a2_bd, g2.astype(BF16), vec(k_k), vec(k_a), vec(r_k),
      _head_indicator(rw_w, RW_HEAD))


def _rwkv_scan_kernel(rc, vc, kkc, lwc, kdc, bdc, rl, vl, kkl, lwl, kdl, bdl, y_ref, s_ref, *, n_ctx, bb):
    d = pl.program_id(1)
    c = pl.program_id(2)
    C = CHUNK
    n_pairs = s_ref.shape[0] // bb

    @pl.when(c == 0)
    def _():
        s_ref[...] = jnp.zeros_like(s_ref)

    is_ctx = c < n_ctx
    pick = lambda a, b: jnp.where(is_ctx, a, b)
    rev = d == 1
    ti = lax.broadcasted_iota(jnp.int32, (C, C), 0)
    tj = lax.broadcasted_iota(jnp.int32, (C, C), 1)
    tri = jnp.where(jnp.where(rev, tj - ti, ti - tj) >= 0, 1.0, 0.0).astype(BF16)

    def features(bi):
        r = pick(rc[bi], rl[bi])
        v = pick(vc[bi], vl[bi])
        kk = pick(kkc[bi], kkl[bi])
        lw = pick(lwc[0, bi], lwl[0, bi])
        kd = pick(kdc[0, bi], kdl[0, bi])
        bd = pick(bdc[0, bi], bdl[0, bi])
        linc = _mm_exact_lhs(tri, lw)
        ltot = jnp.sum(lw, axis=0, keepdims=True)
        g_inv = jnp.exp(-linc)
        g_end = jnp.exp(ltot - linc)
        return (kk * jnp.exp(linc - lw), r * jnp.exp(linc), kd * g_inv, bd * g_inv, kd * g_end, bd * g_end, v,
                jnp.exp(ltot))

    feats = [features(bi) for bi in range(bb)]

    n = 2 * C
    rid = lax.broadcasted_iota(jnp.int32, (n, n), 0)
    cid = lax.broadcasted_iota(jnp.int32, (n, n), 1)
    same = (rid >> 6) == (cid >> 6)
    diff = jnp.where(rev, cid - rid, rid - cid)
    strict = lambda m: jnp.where(same, jnp.where(diff > 0, m, 0.0), 0.0)
    incl = lambda m: jnp.where(same, jnp.where(diff >= 0, m, 0.0), 0.0)
    first = lax.broadcasted_iota(jnp.int32, (1, LANE), 1) < RW_HEAD

    def stack(x):
        return jnp.concatenate([jnp.where(first, x, 0.0), jnp.where(first, 0.0, x)], axis=0)

    prob = [(bi, slice(p * LANE, (p + 1) * LANE)) for bi in range(bb) for p in range(n_pairs)]
    qs = range(len(prob))
    stk = [[stack(t[:, sl]) for t in feats[bi][:7]] for bi, sl in prob]
    kks, rs, kis, bis, kes, bes, vs = ([stk[q][i] for q in qs] for i in range(7))
    m = [_mm_nt(jnp.concatenate([kks[q], rs[q]], axis=0), jnp.concatenate([kis[q], bis[q]], axis=0)) for q in qs]
    a_k = [strict(t[:n, :n]) for t in m]
    a_b = [strict(t[:n, n:]) for t in m]
    b_k = [incl(t[n:, :n]) for t in m]
    b_b = [incl(t[n:, n:]) for t in m]
    t_inv = _unit_tri_inverse(a_b, rid, cid, C)
    abv = [_mm(jnp.concatenate([a_k[q], b_k[q]], axis=0), vs[q]) for q in qs]
    wu = [_mm(t_inv[q], jnp.concatenate([kks[q], abv[q][:n]], axis=1)) for q in qs]
    w_s = [t[:, :LANE] for t in wu]
    u0_s = [t[:, LANE:] for t in wu]
    vt = [t.T for t in vs]
    u0_t = [t.T for t in u0_s]
    st = [s_ref[q] for q in qs]
    wr = [_mm_nt(jnp.concatenate([w_s[q], rs[q]], axis=0), st[q]) for q in qs]
    u_s = [wr[q][:n] + u0_s[q] for q in qs]
    u_t = [_mm_nt(st[q], w_s[q]) + u0_t[q] for q in qs]
    y_s = [wr[q][n:] + abv[q][n:] - _mm(b_b[q], u_s[q]) for q in qs]
    for q, (bi, sl) in enumerate(prob):
        s_ref[q] = st[q] * feats[bi][7][:, sl] + _mm(jnp.concatenate([vt[q], -u_t[q]], axis=1),
                                                    jnp.concatenate([kes[q], bes[q]], axis=0))
        y_ref[0, bi, :, sl] = y_s[q][:C] + y_s[q][C:]


def rwkv_scan(f_ctx, f_lat):
    rc, vc, kkc, lwc, kdc, bdc = f_ctx
    rl, vl, kkl, lwl, kdl, bdl = f_lat
    bsz, Lc, w = rc.shape
    L = rl.shape[1]
    C = CHUNK
    assert Lc % C == 0 and L % C == 0 and w % LANE == 0
    nc, nl = Lc // C, L // C
    bb = _row_tile(bsz, RW_SCAN_BATCH)
    ctx_idx, lat_idx = _scan_index_maps(nc, nl)
    sc = pl.BlockSpec((bb, C, w), lambda b, d, c: (b, ctx_idx(d, c), 0))
    dc = pl.BlockSpec((1, bb, C, w), lambda b, d, c: (d, b, ctx_idx(d, c), 0))
    slt = pl.BlockSpec((bb, C, w), lambda b, d, c: (b, lat_idx(d, c), 0))
    dl = pl.BlockSpec((1, bb, C, w), lambda b, d, c: (d, b, lat_idx(d, c), 0))
    return pl.pallas_call(
        functools.partial(_rwkv_scan_kernel, n_ctx=nc, bb=bb),
        grid=(bsz // bb, 2, nc + nl),
        in_specs=[sc, sc, sc, dc, dc, dc, slt, slt, slt, dl, dl, dl],
        out_specs=pl.BlockSpec((1, bb, C, w), lambda b, d, c: (d, b, lat_idx(d, c), 0)),
        out_shape=jax.ShapeDtypeStruct((2, bsz, L, w), F32),
        scratch_shapes=[pltpu.VMEM((bb * (w // LANE), LANE, LANE), F32)],
        compiler_params=_params("parallel", "parallel", "arbitrary"),
        name="rwkv_scan",
    )(rc, vc, kkc, lwc, kdc, bdc, rl, vl, kkl, lwl, kdl, bdl)


def _rwkv_out_kernel(y_ref, bonus_ref, g_ref, gg_ref, gb_ref, e_ref, o_ref):
    y = y_ref[0, 0] + y_ref[1, 0]
    e = e_ref[...]
    inv_n = 1.0 / RW_HEAD
    mu = _mm_exact_rhs(y, e) * inv_n
    yc = y - mu
    var = _mm_exact_rhs(yc * yc, e) * inv_n
    yn = yc * lax.rsqrt(var + RW_GN_EPS) * gg_ref[...] + gb_ref[...]
    o_ref[0] = (yn + bonus_ref[0]) * g_ref[0]


def rwkv_output(y, bonus, g, gn_g, gn_b):
    _, bsz, L, w = y.shape
    tm = _row_tile(L, 512)
    tok = pl.BlockSpec((1, tm, w), lambda b, i: (b, i, 0))
    vec = pl.BlockSpec((1, w), lambda b, i: (0, 0))
    return pl.pallas_call(
        _rwkv_out_kernel,
        grid=(bsz, L // tm),
        in_specs=[pl.BlockSpec((2, 1, tm, w), lambda b, i: (0, b, i, 0)), tok, tok, vec, vec,
                  pl.BlockSpec((w, w), lambda b, i: (0, 0))],
        out_specs=tok,
        out_shape=jax.ShapeDtypeStruct((bsz, L, w), F32),
        compiler_params=_params("parallel", "parallel"),
        name="rwkv_output",
    )(y, bonus, g, gn_g.reshape(1, w), gn_b.reshape(1, w), _head_indicator(w, RW_HEAD))


def rwkv7_mixer_lat(p_lat, p_ctx, mu, w0, w2, a0, a2, g2, k_k, k_a, r_k, gn_g, gn_b):
    args = (mu, w0, w2, a0, a2, g2, k_k, k_a, r_k)
    r_l, v_l, kk_l, g_l, bonus_l, lw_l, kd_l, bd_l = rwkv_features(p_lat, *args)
    r_c, v_c, kk_c, _, _, lw_c, kd_c, bd_c = rwkv_features(p_ctx, *args)
    y = rwkv_scan((r_c, v_c, kk_c, lw_c, kd_c, bd_c), (r_l, v_l, kk_l, lw_l, kd_l, bd_l))
    return rwkv_output(y, bonus_l, g_l, gn_g, gn_b)


def _gdn_qkv_kernel(p_ref, hp_ref, hn_ref, cw_ref, q_o, k_o, v_o, *, dn_w):
    x = p_ref[0]
    x_prev, x_next = _neighbours(x, *_halo_rows(hp_ref, hn_ref))
    cw = cw_ref[...]
    z = x_prev * cw[0:1] + x * cw[1:2] + x_next * cw[2:3]
    z = z * _sigmoid(z)
    for h in range(dn_w // DN_HEAD):
        sq = slice(h * DN_HEAD, (h + 1) * DN_HEAD)
        sk = slice(dn_w + h * DN_HEAD, dn_w + (h + 1) * DN_HEAD)
        q = z[:, sq]
        k = z[:, sk]
        q_o[0, :, sq] = q * (lax.rsqrt(jnp.sum(q * q, -1, keepdims=True) + 1e-6) * (DN_HEAD ** -0.5))
        k_o[0, :, sq] = k * lax.rsqrt(jnp.sum(k * k, -1, keepdims=True) + 1e-6)
    v_o[0] = z[:, 2 * dn_w:]


def _gdn_gate_kernel(g_ref, al_ref, dt_ref, gl_o, be_o):
    g = g_ref[0]
    gl_o[0] = -jnp.exp(al_ref[...]) * _softplus(g[:, :LANE] + dt_ref[...])
    be_o[0] = _sigmoid(g[:, LANE:])


def gdn_features(p, conv_w, A_log, dt_bias):
    bsz, T, cols = p.shape
    dn_w = conv_w.shape[1] // 3
    tm = _row_tile(T, 256)
    c3 = 3 * dn_w
    tok = pl.BlockSpec((1, tm, dn_w), lambda b, i: (b, i, 0))
    s1 = jax.ShapeDtypeStruct((bsz, T, dn_w), F32)
    q, k, v = pl.pallas_call(
        functools.partial(_gdn_qkv_kernel, dn_w=dn_w),
        grid=(bsz, T // tm),
        in_specs=_halo_specs(tm, T, c3) + [pl.BlockSpec((8, c3), lambda b, i: (0, 0))],
        out_specs=[tok, tok, tok],
        out_shape=[s1, s1, s1],
        compiler_params=_params("parallel", "parallel"),
        name="gdn_qkv",
    )(p, p, p, jnp.pad(conv_w, ((0, 8 - conv_w.shape[0]), (0, 0))))
    pad = lambda t: jnp.pad(t.reshape(1, -1).astype(F32), ((0, 0), (0, LANE - t.size)))
    assert (4 * dn_w) % (2 * LANE) == 0
    gcol = (4 * dn_w) // (2 * LANE)
    gt = pl.BlockSpec((1, tm, LANE), lambda b, i: (b, i, 0))
    sg = jax.ShapeDtypeStruct((bsz, T, LANE), F32)
    gl, be = pl.pallas_call(
        _gdn_gate_kernel,
        grid=(bsz, T // tm),
        in_specs=[pl.BlockSpec((1, tm, 2 * LANE), lambda b, i: (b, i, gcol)),
                  pl.BlockSpec((1, LANE), lambda b, i: (0, 0)), pl.BlockSpec((1, LANE), lambda b, i: (0, 0))],
        out_specs=[gt, gt],
        out_shape=[sg, sg],
        compiler_params=_params("parallel", "parallel"),
        name="gdn_gates",
    )(p, pad(A_log), pad(dt_bias))
    return q, k, v, gl, be


def _gdn_scan_kernel(qc, kc, vc, glc, bec, ql, kl, vl, gll, bel, o_ref, s_ref, *, n_ctx, bb):
    d = pl.program_id(1)
    c = pl.program_id(2)
    C = CHUNK
    heads = s_ref.shape[0] // bb

    @pl.when(c == 0)
    def _():
        s_ref[...] = jnp.zeros_like(s_ref)

    is_ctx = c < n_ctx
    pick = lambda a, b, bi: jnp.where(is_ctx, a[bi], b[bi])
    rev = d == 1

    ti = lax.broadcasted_iota(jnp.int32, (C, C), 0)
    tj = lax.broadcasted_iota(jnp.int32, (C, C), 1)
    tri = jnp.where(jnp.where(rev, tj - ti, ti - tj) >= 0, 1.0, 0.0).astype(BF16)
    lane = lax.broadcasted_iota(jnp.int32, (1, LANE), 1)

    def col(x, h):
        return jnp.sum(jnp.where(lane == d * heads + h, x, 0.0), axis=1, keepdims=True)

    n = 2 * C
    rid = lax.broadcasted_iota(jnp.int32, (n, n), 0)
    cid = lax.broadcasted_iota(jnp.int32, (n, n), 1)
    same = (rid >> 6) == (cid >> 6)
    diff = jnp.where(rev, cid - rid, rid - cid)
    keep_incl = jnp.where(same, jnp.where(diff >= 0, 1.0, 0.0), 0.0)
    eye = jnp.where(rid == cid, 1.0, 0.0)

    hs = lambda h: slice(h * DN_HEAD, (h + 1) * DN_HEAD)

    prob = [(bi, p) for bi in range(bb) for p in range(heads // 2)]
    qs = range(len(prob))
    k_s, q_s, v_s, g_col, b_col, ge_col, g_end = [], [], [], [], [], [], []
    for bi in range(bb):
        q, k, v = pick(qc, ql, bi), pick(kc, kl, bi), pick(vc, vl, bi)
        gl, be = pick(glc, gll, bi), pick(bec, bel, bi)
        g_cum = _mm_exact_lhs(tri, gl)
        g_tot = jnp.sum(gl, axis=0, keepdims=True)
        for p in range(heads // 2):
            h0, h1 = 2 * p, 2 * p + 1
            stack = lambda x: jnp.concatenate([x[:, hs(h0)], x[:, hs(h1)]], axis=0)
            k_s.append(stack(k))
            q_s.append(stack(q))
            v_s.append(stack(v))
            g_col.append(jnp.concatenate([col(g_cum, h0), col(g_cum, h1)], axis=0))
            b_col.append(jnp.concatenate([col(be, h0), col(be, h1)], axis=0))
            gt = [col(g_tot, h0), col(g_tot, h1)]
            ge_col.append(jnp.concatenate([jnp.broadcast_to(t, (C, 1)) for t in gt], axis=0))
            g_end.append([jnp.exp(t) for t in gt])
    g_row = [jnp.broadcast_to(g_col[i], (n, n)).T for i in qs]
    decay = [keep_incl * jnp.exp(keep_incl * (g_col[i] - g_row[i])) for i in qs]
    kq = [_mm_nt(jnp.concatenate([k_s[i], q_s[i]], axis=0), k_s[i]) for i in qs]
    a = [(1.0 - eye) * b_col[i] * kq[i][:n] * decay[i] for i in qs]
    qk = [kq[i][n:] * decay[i] for i in qs]
    t_inv = _unit_tri_inverse(a, rid, cid, C)
    eg = [jnp.exp(g_col[i]) for i in qs]
    uw = [_mm(t_inv[i], jnp.concatenate([b_col[i] * v_s[i], (b_col[i] * eg[i]) * k_s[i]], axis=1)) for i in qs]
    u0 = [t[:, :DN_HEAD] for t in uw]
    w = [t[:, DN_HEAD:] for t in uw]
    q_dec = [q_s[i] * eg[i] for i in qs]
    k_dec_t = [_mm_nt(eye, k_s[i] * jnp.exp(ge_col[i] - g_col[i])) for i in qs]
    head_cols = lax.broadcasted_iota(jnp.int32, (1, n), 1) >> 6

    st = [[s_ref[2 * i + t] for t in range(2)] for i in qs]
    ws = [[_mm(jnp.concatenate([w[i][t * C:(t + 1) * C], q_dec[i][t * C:(t + 1) * C]], axis=0), st[i][t])
           for t in range(2)] for i in qs]
    u = [jnp.concatenate([u0[i][:C] - ws[i][0][:C], u0[i][C:] - ws[i][1][:C]], axis=0) for i in qs]
    o = [jnp.concatenate([ws[i][0][C:], ws[i][1][C:]], axis=0) + _mm(qk[i], u[i]) for i in qs]
    for i, (bi, p) in enumerate(prob):
        for t in range(2):
            kd = jnp.where(head_cols == t, k_dec_t[i], 0.0)
            s_ref[2 * i + t] = g_end[i][t] * st[i][t] + _mm(kd, u[i])
            o_ref[0, bi, :, hs(2 * p + t)] = o[i][t * C:(t + 1) * C]


def gdn_scan(f_ctx, f_lat):
    qc, kc, vc, glc, bec = f_ctx
    ql, kl, vl, gll, bel = f_lat
    bsz, Lc, w = qc.shape
    L = ql.shape[1]
    C = CHUNK
    heads = w // DN_HEAD
    assert Lc % C == 0 and L % C == 0 and heads % 2 == 0
    nc, nl = Lc // C, L // C
    bb = _row_tile(bsz, DN_SCAN_BATCH)
    ctx_idx, lat_idx = _scan_index_maps(nc, nl)
    sc = pl.BlockSpec((bb, C, w), lambda b, d, c: (b, ctx_idx(d, c), 0))
    gc = pl.BlockSpec((bb, C, LANE), lambda b, d, c: (b, ctx_idx(d, c), 0))
    sl = pl.BlockSpec((bb, C, w), lambda b, d, c: (b, lat_idx(d, c), 0))
    gl = pl.BlockSpec((bb, C, LANE), lambda b, d, c: (b, lat_idx(d, c), 0))
    return pl.pallas_call(
        functools.partial(_gdn_scan_kernel, n_ctx=nc, bb=bb),
        grid=(bsz // bb, 2, nc + nl),
        in_specs=[sc, sc, sc, gc, gc, sl, sl, sl, gl, gl],
        out_specs=pl.BlockSpec((1, bb, C, w), lambda b, d, c: (d, b, lat_idx(d, c), 0)),
        out_shape=jax.ShapeDtypeStruct((2, bsz, L, w), F32),
        scratch_shapes=[pltpu.VMEM((bb * heads, DN_HEAD, DN_HEAD), F32)],
        compiler_params=_params("parallel", "parallel", "arbitrary"),
        name="gdn_scan",
    )(qc, kc, vc, glc, bec, ql, kl, vl, gll, bel)


def _gdn_out_kernel(o_ref, z_ref, ng_ref, y_ref):
    o = o_ref[0, 0] + o_ref[1, 0]
    z = z_ref[0]
    for h in range(o.shape[1] // DN_HEAD):
        sl = slice(h * DN_HEAD, (h + 1) * DN_HEAD)
        oh = o[:, sl]
        on = oh * lax.rsqrt(jnp.mean(oh * oh, -1, keepdims=True) + 1e-6) * ng_ref[...]
        zh = z[:, sl]
        y_ref[0, :, sl] = on * (zh * _sigmoid(zh))


def gdn_output(o, p, norm_g):
    _, bsz, L, w = o.shape
    tm = _row_tile(L, 512)
    z_block = 3
    tok = pl.BlockSpec((1, tm, w), lambda b, i: (b, i, 0))
    return pl.pallas_call(
        _gdn_out_kernel,
        grid=(bsz, L // tm),
        in_specs=[pl.BlockSpec((2, 1, tm, w), lambda b, i: (0, b, i, 0)),
                  pl.BlockSpec((1, tm, w), lambda b, i: (b, i, z_block)),
                  pl.BlockSpec((1, DN_HEAD), lambda b, i: (0, 0))],
        out_specs=tok,
        out_shape=jax.ShapeDtypeStruct((bsz, L, w), F32),
        compiler_params=_params("parallel", "parallel"),
        name="gdn_output",
    )(o, p, norm_g.reshape(1, DN_HEAD))


def gdn_mixer_lat(p_lat, p_ctx, conv_w, A_log, dt_bias, norm_g):
    f_lat = gdn_features(p_lat, conv_w, A_log, dt_bias)
    f_ctx = gdn_features(p_ctx, conv_w, A_log, dt_bias)
    return gdn_output(gdn_scan(f_ctx, f_lat), p_lat, norm_g)


def _gdn_in_weight(w_dn, dn_w, heads):
    padw = ((0, 0), (0, LANE - 2 * heads))
    g = w_dn[:, 4 * dn_w:]
    return jnp.concatenate([w_dn[:, :4 * dn_w], jnp.pad(g[:, :2 * heads], padw), jnp.pad(g[:, 2 * heads:], padw)], 1)


def kernel(x, c, ctx, c_ctx, mod_w, mod_b, ln_g, ln_b, mlp_w1, mlp_w2, e_w_in, e_w_out, hy_conv, hy_ffn_w1,
           hy_ffn_b1, hy_ffn_w2, hy_ffn_b2, hy_sin_freq, hy_ffn_w3, hy_decay, hy_bias, attn_sink, o_w_in,
           o_w_out, rw_mu, rw_w0, rw_w2, rw_a0, rw_a2, rw_g2, rw_kk, rw_ka, rw_rk, rw_lnx_g, rw_lnx_b,
           dn_conv, dn_A_log, dn_dt_bias, dn_norm_g):
    bsz, L, d = x.shape
    assert mod_w.shape[0] == DEPTH == 2
    cc = jnp.concatenate([c, c_ctx[None, :]], axis=0)
    cc = jnp.pad(cc, ((0, (-cc.shape[0]) % 8), (0, 0)))

    def modulation(i):
        m = mod_vectors(cc, mod_w[i], mod_b[i])
        m_lat = m[:bsz].reshape(bsz, 1, N_MOD, d)
        m_ctx = jnp.broadcast_to(m[bsz].reshape(1, 1, N_MOD, d), (bsz, 1, N_MOD, d))
        return [m_lat[:, :, t] for t in range(N_MOD)], [m_ctx[:, :, t] for t in range(N_MOD)]

    (sh1, sc1, g1, sh2, sc2, g2), (csh1, csc1, cg1, csh2, csc2, cg2) = modulation(0)
    hy_cols = hy_conv.shape[2]
    w_in = e_w_in[0].astype(BF16)
    ws = [w_in[:, :hy_cols], w_in[:, hy_cols:]]
    w_out = e_w_out[0].astype(BF16)
    w1, w2 = mlp_w1[0].astype(BF16), mlp_w2[0].astype(BF16)
    hy = (hy_conv[0], hy_ffn_w1[0], hy_ffn_b1[0], hy_ffn_w2[0], hy_ffn_b2[0], hy_sin_freq[0], hy_ffn_w3[0],
          hy_decay[0], hy_bias[0])
    p_hy, p_qkv = mod_linear(x, sc1, sh1, ws)
    pc_hy, pc_qkv = mod_linear(ctx, csc1, csh1, ws)
    y_a = hyena_mixer(p_hy, *hy)
    y_b = windowed_attention(p_qkv, pc_qkv, attn_sink[0])
    yc_a = hyena_mixer(pc_hy, *hy)
    yc_b = context_attention(pc_qkv, attn_sink[0])
    x = out_proj_ln(y_a, y_b, w_out, x, g1, ln_g[0, 0], ln_b[0, 0])
    x = mlp_ln(x, sc2, sh2, g2, w1, w2, ln_g[0, 1], ln_b[0, 1])
    ctx = out_proj_ln(yc_a, yc_b, w_out, ctx, cg1, ln_g[0, 0], ln_b[0, 0])
    ctx = mlp_ln(ctx, csc2, csh2, cg2, w1, w2, ln_g[0, 1], ln_b[0, 1])

    (sh1, sc1, g1, sh2, sc2, g2), (csh1, csc1, _, _, _, _) = modulation(1)
    rw_cols = rw_mu.shape[1]
    dn_w = dn_conv.shape[2] // 3
    w_in = o_w_in[0]
    ws = [w_in[:, :rw_cols].astype(BF16), _gdn_in_weight(w_in[:, rw_cols:], dn_w, dn_w // DN_HEAD).astype(BF16)]
    w_out = o_w_out[0].astype(BF16)
    w1, w2 = mlp_w1[1].astype(BF16), mlp_w2[1].astype(BF16)
    p_rw, p_dn = mod_linear(x, sc1, sh1, ws)
    pc_rw, pc_dn = mod_linear(ctx, csc1, csh1, ws)
    y_c = rwkv7_mixer_lat(p_rw, pc_rw, rw_mu[0], rw_w0[0], rw_w2[0], rw_a0[0], rw_a2[0], rw_g2[0], rw_kk[0],
                          rw_ka[0], rw_rk[0], rw_lnx_g[0], rw_lnx_b[0])
    y_d = gdn_mixer_lat(p_dn, pc_dn, dn_conv[0], dn_A_log[0], dn_dt_bias[0], dn_norm_g[0])
    x = out_proj_ln(y_c, y_d, w_out, x, g1, ln_g[1, 0], ln_b[1, 0])
    x = mlp_ln(x, sc2, sh2, g2, w1, w2, ln_g[1, 1], ln_b[1, 1])
    return x
```

```python
import functools
import math

import jax
import jax.numpy as jnp
from jax import lax
from jax.experimental import pallas as pl
from jax.experimental.pallas import tpu as pltpu

F32 = jnp.float32
BF16 = jnp.bfloat16

DEPTH = 2
N_MOD = 6
DEEPNORM_ALPHA = (2.0 * DEPTH) ** 0.25
LN_EPS = 1e-5

HY_BANDS = 16
HEAD_DIM = 64
SWA_KV_HEADS = 2
WINDOW = 128
SWA_BLOCK = 128
SWA_QUERY_TILE = 128
GRID_W = 64
ROPE_BASE = 10000.0

RW_HEAD = 64
RW_DECAY_LORA = 64
RW_AAA_LORA = 64
RW_GN_EPS = 64e-5

DN_HEAD = 128
CHUNK = 64
RW_SCAN_BATCH = 4
DN_SCAN_BATCH = 8

LANE = 128
MXU_WIDTH = 256
VMEM_LIMIT = 56 * 1024 * 1024


def _row_tile(n, cap):
    t = min(n, cap)
    while n % t:
        t //= 2
    return t


def _params(*sem):
    return pltpu.CompilerParams(dimension_semantics=sem, vmem_limit_bytes=VMEM_LIMIT)


def _mm(a, b):
    return jnp.dot(a.astype(BF16), b.astype(BF16), preferred_element_type=F32)


def _mm_nt(a, b):
    return lax.dot_general(a.astype(BF16), b.astype(BF16), (((1,), (1,)), ((), ())), preferred_element_type=F32)


def _split2(x):
    hi = x.astype(BF16)
    return hi, (x - hi.astype(F32)).astype(BF16)


def _split3(x):
    hi, r = x.astype(BF16), None
    r = x - hi.astype(F32)
    mid = r.astype(BF16)
    return hi, mid, (r - mid.astype(F32)).astype(BF16)


def _mm_exact_lhs(a_bf16, x):
    return sum(jnp.dot(a_bf16, t, preferred_element_type=F32) for t in _split3(x))


def _mm_exact_rhs(x, b_bf16):
    return sum(jnp.dot(t, b_bf16, preferred_element_type=F32) for t in _split3(x))


def _mm_f32(a, b):
    ah, al = _split2(a)
    bh, bl = _split2(b)
    dot = lambda x, y: jnp.dot(x, y, preferred_element_type=F32)
    return dot(ah, bh) + dot(ah, bl) + dot(al, bh)


def _sigmoid(x):
    return 1.0 / (1.0 + jnp.exp(-x))


def _softplus(x):
    return jnp.maximum(x, 0.0) + jnp.log(1.0 + jnp.exp(-jnp.abs(x)))


def _neighbours(x, prev_row, next_row):
    n = x.shape[0]
    rows = lax.broadcasted_iota(jnp.int32, (n, 1), 0)
    x_prev = jnp.where(rows == 0, prev_row, pltpu.roll(x, 1, axis=0))
    x_next = jnp.where(rows == n - 1, next_row, pltpu.roll(x, n - 1, axis=0))
    return x_prev, x_next


def _halo_rows(hp_ref, hn_ref):
    i = pl.program_id(1)
    nt = pl.num_programs(1)
    return jnp.where(i > 0, hp_ref[0][7:8, :], 0.0), jnp.where(i < nt - 1, hn_ref[0][0:1, :], 0.0)


def _halo_specs(tm, T, cols, col_block=0):
    hb = tm // 8
    return [pl.BlockSpec((1, tm, cols), lambda b, i: (b, i, col_block)),
            pl.BlockSpec((1, 8, cols), lambda b, i: (b, jnp.maximum(i * hb - 1, 0), col_block)),
            pl.BlockSpec((1, 8, cols), lambda b, i: (b, jnp.minimum((i + 1) * hb, T // 8 - 1), col_block))]


def _unit_tri_inverse(a_list, rid, cid, blk):
    eye = jnp.where(rid == cid, 1.0, 0.0)
    pair = (rid >> 1) == (cid >> 1)
    d_list = [eye - jnp.where(pair, a, 0.0) for a in a_list]
    s, sh = 2, 1
    while s < blk:
        join = jnp.where((rid >> (sh + 1)) == (cid >> (sh + 1)), jnp.where((rid >> sh) != (cid >> sh), 1.0, 0.0), 0.0)
        dq = [_mm(d, a * join) for d, a in zip(d_list, a_list)]
        d_list = [d - _mm(t, d) for t, d in zip(dq, d_list)]
        s, sh = 2 * s, sh + 1
    return d_list


def _scan_index_maps(nc, nl):
    def ctx_idx(d, c):
        cc = jnp.minimum(c, nc - 1)
        return jnp.where(d == 1, nc - 1 - cc, cc)

    def lat_idx(d, c):
        cl = jnp.maximum(c - nc, 0)
        return jnp.where(d == 1, nl - 1 - cl, cl)

    return ctx_idx, lat_idx


def _mod_kernel(c_ref, w_ref, b_ref, o_ref):
    c = c_ref[...]
    o_ref[...] = _mm(c * _sigmoid(c), w_ref[...]) + b_ref[...]


def mod_vectors(cc, w, b):
    r, d = cc.shape
    n = w.shape[1]
    tn = _row_tile(n, 1024)
    return pl.pallas_call(
        _mod_kernel,
        grid=(n // tn,),
        in_specs=[pl.BlockSpec((r, d), lambda j: (0, 0)),
                  pl.BlockSpec((d, tn), lambda j: (0, j)),
                  pl.BlockSpec((1, tn), lambda j: (0, j))],
        out_specs=pl.BlockSpec((r, tn), lambda j: (0, j)),
        out_shape=jax.ShapeDtypeStruct((r, n), F32),
        name="mod_vectors",
    )(cc, w, b.reshape(1, n))


def _mod_linear_kernel(x_ref, sc_ref, sh_ref, *refs):
    n = len(refs) // 2
    h = (x_ref[0] * (1.0 + sc_ref[0]) + sh_ref[0]).astype(BF16)
    for w_ref, o_ref in zip(refs[:n], refs[n:]):
        o_ref[0] = jnp.dot(h, w_ref[...], preferred_element_type=F32)


def mod_linear(x, sc, sh, ws):
    bsz, L, d = x.shape
    tm = _row_tile(L, 512)
    vec = pl.BlockSpec((1, 1, d), lambda b, i: (b, 0, 0))
    return pl.pallas_call(
        _mod_linear_kernel,
        grid=(bsz, L // tm),
        in_specs=[pl.BlockSpec((1, tm, d), lambda b, i: (b, i, 0)), vec, vec]
        + [pl.BlockSpec(w.shape, lambda b, i: (0, 0)) for w in ws],
        out_specs=[pl.BlockSpec((1, tm, w.shape[1]), lambda b, i: (b, i, 0)) for w in ws],
        out_shape=[jax.ShapeDtypeStruct((bsz, L, w.shape[1]), F32) for w in ws],
        compiler_params=_params("parallel", "parallel"),
        name="mod_linear",
    )(x, sc, sh, *ws)


def _layer_norm_rows(z, g, b):
    mu = jnp.mean(z, -1, keepdims=True)
    zc = z - mu
    var = jnp.mean(zc * zc, -1, keepdims=True)
    return zc * lax.rsqrt(var + LN_EPS) * g + b


def _out_ln_kernel(ya_ref, yb_ref, wa_ref, wb_ref, x_ref, g_ref, lg_ref, lb_ref, o_ref):
    y = _mm(ya_ref[0], wa_ref[...]) + _mm(yb_ref[0], wb_ref[...])
    z = DEEPNORM_ALPHA * x_ref[0] + g_ref[0] * y
    o_ref[0] = _layer_norm_rows(z, lg_ref[...], lb_ref[...])


def out_proj_ln(ya, yb, w, x, gate, ln_g, ln_b):
    bsz, L, d = x.shape
    ka, kb = ya.shape[-1], yb.shape[-1]
    tm = _row_tile(L, 512)
    row = pl.BlockSpec((1, d), lambda b, i: (0, 0))
    return pl.pallas_call(
        _out_ln_kernel,
        grid=(bsz, L // tm),
        in_specs=[pl.BlockSpec((1, tm, ka), lambda b, i: (b, i, 0)),
                  pl.BlockSpec((1, tm, kb), lambda b, i: (b, i, 0)),
                  pl.BlockSpec((ka, d), lambda b, i: (0, 0)),
                  pl.BlockSpec((kb, d), lambda b, i: (0, 0)),
                  pl.BlockSpec((1, tm, d), lambda b, i: (b, i, 0)),
                  pl.BlockSpec((1, 1, d), lambda b, i: (b, 0, 0)), row, row],
        out_specs=pl.BlockSpec((1, tm, d), lambda b, i: (b, i, 0)),
        out_shape=jax.ShapeDtypeStruct((bsz, L, d), F32),
        compiler_params=_params("parallel", "parallel"),
        name="out_proj_ln",
    )(ya, yb, w[:ka], w[ka:], x, gate, ln_g.reshape(1, d), ln_b.reshape(1, d))


def _mlp_ln_kernel(x_ref, sc_ref, sh_ref, g_ref, w1_ref, w2_ref, lg_ref, lb_ref, o_ref, *, ff_tile):
    x = x_ref[0]
    h = (x * (1.0 + sc_ref[0]) + sh_ref[0]).astype(BF16)
    d_ff = w1_ref.shape[1]
    acc = jnp.zeros(x.shape, F32)
    for c in range(d_ff // ff_tile):
        a = jnp.dot(h, w1_ref[:, c * ff_tile:(c + 1) * ff_tile], preferred_element_type=F32)
        a = jnp.maximum(a, 0.0)
        a = (a * a).astype(BF16)
        acc = acc + jnp.dot(a, w2_ref[c * ff_tile:(c + 1) * ff_tile, :], preferred_element_type=F32)
    z = DEEPNORM_ALPHA * x + g_ref[0] * acc
    o_ref[0] = _layer_norm_rows(z, lg_ref[...], lb_ref[...])


def mlp_ln(x, sc, sh, gate, w1, w2, ln_g, ln_b):
    bsz, L, d = x.shape
    d_ff = w1.shape[1]
    tm = _row_tile(L, 512)
    ff_tile = _row_tile(d_ff, 1024)
    once = dict(pipeline_mode=pl.Buffered(1))
    vec = pl.BlockSpec((1, 1, d), lambda b, i: (b, 0, 0))
    row = pl.BlockSpec((1, d), lambda b, i: (0, 0))
    return pl.pallas_call(
        functools.partial(_mlp_ln_kernel, ff_tile=ff_tile),
        grid=(bsz, L // tm),
        in_specs=[pl.BlockSpec((1, tm, d), lambda b, i: (b, i, 0)), vec, vec, vec,
                  pl.BlockSpec((d, d_ff), lambda b, i: (0, 0), **once),
                  pl.BlockSpec((d_ff, d), lambda b, i: (0, 0), **once), row, row],
        out_specs=pl.BlockSpec((1, tm, d), lambda b, i: (b, i, 0)),
        out_shape=jax.ShapeDtypeStruct((bsz, L, d), F32),
        compiler_params=_params("parallel", "parallel"),
        name="mlp_ln",
    )(x, sc, sh, gate, w1, w2, ln_g.reshape(1, d), ln_b.reshape(1, d))


def dft_tables(T):
    n = 2 * T
    f = jnp.arange(T, dtype=jnp.int32)
    ang = ((f[:, None] * f[None, :]) % n).astype(F32) * (2.0 * math.pi / n)
    return jnp.cos(ang).astype(BF16), (-jnp.sin(ang)).astype(BF16)


def hyena_pos_features(T):
    t = jnp.arange(T, dtype=F32)
    t_norm = t / max(T - 1, 1)
    bands = jnp.linspace(1e-4, HY_BANDS - 1, HY_BANDS, dtype=F32)
    ang = 2.0 * math.pi * t[:, None] * bands[None, :] / T
    pe = jnp.concatenate([t_norm[:, None], jnp.cos(ang), -jnp.sin(ang)], axis=-1)
    return jnp.pad(pe, ((0, 0), (0, (-pe.shape[1]) % 8)))


def _hy_filter_kernel(pe_ref, w1_ref, b1_ref, w2_ref, b2_ref, fr_ref, w3f_ref, w3b_ref, df_ref, db_ref,
                      cm_ref, sm_ref, kr_o, ki_o, kn_o):
    T = pe_ref.shape[0]
    pe = pe_ref[...]
    fr = fr_ref[...]
    h = jnp.sin(fr * (_mm_f32(pe, w1_ref[...]) + b1_ref[...]))
    h = jnp.sin(fr * (_mm_f32(h, w2_ref[...]) + b2_ref[...]))
    t_norm = pe[:, 0:1]
    h_f = _mm_f32(h, w3f_ref[...]) * jnp.exp(-t_norm * jnp.abs(df_ref[...]))
    h_b = _mm_f32(h, w3b_ref[...]) * jnp.exp(-t_norm * jnp.abs(db_ref[...]))
    row = lax.broadcasted_iota(jnp.int32, (T, 1), 0)
    h_b = jnp.where(row == 0, 0.0, h_b)
    scale = jnp.where(row == 0, 1.0, 2.0) * (1.0 / (2 * T))
    kr_o[...] = _mm(cm_ref[...], h_f + h_b) * scale
    ki_o[...] = _mm(sm_ref[...], h_f - h_b) * scale
    sign = jnp.where((row & 1) == 0, 1.0, -1.0)
    kn = jnp.sum(sign * (h_f + h_b), axis=0, keepdims=True) * (1.0 / (2 * T))
    kn_o[...] = jnp.broadcast_to(kn, kn_o.shape)


def hyena_spectrum(T, cm, sm, w1, b1, w2, b2, freq, w3, decay):
    hy_w = w3.shape[1] // 2
    ct = MXU_WIDTH
    nj = hy_w // ct
    pe = hyena_pos_features(T)
    w1p = jnp.pad(w1, ((0, pe.shape[1] - w1.shape[0]), (0, 0)))
    row = lambda t: t.reshape(1, -1)
    full = lambda a: pl.BlockSpec(a.shape, lambda j: (0,) * a.ndim)
    ins = (pe, w1p, row(b1), w2, row(b2), row(freq))
    return pl.pallas_call(
        _hy_filter_kernel,
        grid=(nj,),
        in_specs=[full(a) for a in ins] + [
            pl.BlockSpec((w3.shape[0], ct), lambda j: (0, j)), pl.BlockSpec((w3.shape[0], ct), lambda j: (0, nj + j)),
            pl.BlockSpec((1, ct), lambda j: (0, j)), pl.BlockSpec((1, ct), lambda j: (0, nj + j)),
            pl.BlockSpec((T, T), lambda j: (0, 0)), pl.BlockSpec((T, T), lambda j: (0, 0))],
        out_specs=[pl.BlockSpec((T, ct), lambda j: (0, j)), pl.BlockSpec((T, ct), lambda j: (0, j)),
                   pl.BlockSpec((8, ct), lambda j: (0, j))],
        out_shape=[jax.ShapeDtypeStruct((T, hy_w), F32), jax.ShapeDtypeStruct((T, hy_w), F32),
                   jax.ShapeDtypeStruct((8, hy_w), F32)],
        compiler_params=_params("arbitrary"),
        name="hyena_spectrum",
    )(*ins, w3, w3, row(decay), row(decay), cm, sm)


def _hy_conv_kernel(x0_ref, x1_ref, xv_ref, c0_ref, c1_ref, cv_ref, bias_ref, kr_ref, ki_ref, kn_ref,
                    cm_ref, sm_ref, o_ref):
    T = x0_ref.shape[1]
    row = lax.broadcasted_iota(jnp.int32, (T, 1), 0)
    zero = jnp.zeros((1, x0_ref.shape[2]), F32)

    def short_conv(x_ref, c_ref):
        x = x_ref[0]
        c = c_ref[...]
        x_prev, x_next = _neighbours(x, zero, zero)
        return x_prev * c[0:1] + x * c[1:2] + x_next * c[2:3]

    u = short_conv(x1_ref, c1_ref) * short_conv(xv_ref, cv_ref)
    ub = u.astype(BF16)
    sign = jnp.where((row & 1) == 0, 1.0, -1.0)
    un = jnp.sum(sign * u, axis=0, keepdims=True)
    y = sign * (un * kn_ref[0:1, :]) + u * bias_ref[...]
    ft = min(T, 512)
    yr, yi = [], []
    for f0 in range(0, T, ft):
        fs = slice(f0, f0 + ft)
        ur = jnp.dot(cm_ref[fs, :], ub, preferred_element_type=F32)
        ui = jnp.dot(sm_ref[fs, :], ub, preferred_element_type=F32)
        kr = kr_ref[fs, :]
        ki = ki_ref[fs, :]
        yr.append((ur * kr - ui * ki).astype(BF16))
        yi.append((ur * ki + ui * kr).astype(BF16))
    y = y + jnp.dot(cm_ref[...], jnp.concatenate(yr, axis=0), preferred_element_type=F32)
    y = y + jnp.dot(sm_ref[...], jnp.concatenate(yi, axis=0), preferred_element_type=F32)
    o_ref[0] = short_conv(x0_ref, c0_ref) * y


def hyena_mixer(p, conv_w, w1, b1, w2, b2, freq, w3, decay, bias):
    bsz, T, _ = p.shape
    hy_w = bias.shape[0]
    ct = MXU_WIDTH
    nj = hy_w // ct
    cm, sm = dft_tables(T)
    kr, ki, kn = hyena_spectrum(T, cm, sm, w1, b1, w2, b2, freq, w3, decay)
    cw = jnp.pad(conv_w, ((0, 8 - conv_w.shape[0]), (0, 0)))
    once = dict(pipeline_mode=pl.Buffered(1))
    xs = lambda part: pl.BlockSpec((1, T, ct), lambda j, b: (b, 0, part * nj + j))
    cs = lambda part: pl.BlockSpec((8, ct), lambda j, b: (0, part * nj + j))
    return pl.pallas_call(
        _hy_conv_kernel,
        grid=(nj, bsz),
        in_specs=[xs(0), xs(1), xs(2), cs(0), cs(1), cs(2),
                  pl.BlockSpec((1, ct), lambda j, b: (0, j)),
                  pl.BlockSpec((T, ct), lambda j, b: (0, j), **once),
                  pl.BlockSpec((T, ct), lambda j, b: (0, j), **once),
                  pl.BlockSpec((8, ct), lambda j, b: (0, j)),
                  pl.BlockSpec((T, T), lambda j, b: (0, 0), **once),
                  pl.BlockSpec((T, T), lambda j, b: (0, 0), **once)],
        out_specs=pl.BlockSpec((1, T, ct), lambda j, b: (b, 0, j)),
        out_shape=jax.ShapeDtypeStruct((bsz, T, hy_w), F32),
        compiler_params=_params("parallel", "parallel"),
        name="hyena_conv",
    )(p, p, p, cw, cw, cw, bias.reshape(1, hy_w), kr, ki, kn, cm, sm)


def rope_tables(L):
    quarter = HEAD_DIM // 4
    pos = jnp.arange(L, dtype=jnp.int32)
    inv_freq = ROPE_BASE ** (-jnp.arange(quarter, dtype=F32) / quarter)
    a_row = (pos // GRID_W).astype(F32)[:, None] * inv_freq[None, :]
    a_col = (pos % GRID_W).astype(F32)[:, None] * inv_freq[None, :]
    ang = jnp.concatenate([a_row, a_row, a_col, a_col], axis=-1)
    sgn = jnp.tile(jnp.concatenate([-jnp.ones(quarter, F32), jnp.ones(quarter, F32)]), 2)
    return jnp.tile(jnp.cos(ang), (1, 2)), jnp.tile(jnp.sin(ang) * sgn, (1, 2))


def _rope(t, cos, sin, first_quarter):
    w = t.shape[1]
    partner = jnp.where(first_quarter, pltpu.roll(t, w - 16, axis=1), pltpu.roll(t, 16, axis=1))
    return t * cos + partner * sin


def _attn_kernel(*refs, has_local, L, span, group):
    if has_local:
        q_ref, kv_ref, kvc_ref, cos_ref, sin_ref, sink_ref, o_ref = refs
    else:
        q_ref, kvc_ref, sink_ref, o_ref = refs
    i = pl.program_id(1)
    tq = q_ref.shape[1]
    lane = lax.broadcasted_iota(jnp.int32, (1, LANE), 1)
    low = lane < HEAD_DIM
    q = q_ref[0]
    n_heads = q.shape[1] // HEAD_DIM
    kvc = kvc_ref[0]
    kc, vc = kvc[:, :LANE], kvc[:, LANE:]
    if has_local:
        fq = (lane & 31) < 16
        q0 = pl.multiple_of(i * tq, tq)
        cos_q = cos_ref[pl.ds(q0, tq), :]
        sin_q = sin_ref[pl.ds(q0, tq), :]
        k0 = pl.multiple_of(jnp.clip(i * tq - WINDOW, 0, L - span), SWA_BLOCK)
        kv = kv_ref[0, pl.ds(k0, span), :]
        k = _rope(kv[:, :LANE], cos_ref[pl.ds(k0, span), :], sin_ref[pl.ds(k0, span), :], fq)
        v = kv[:, LANE:]
        qpos = q0 + lax.broadcasted_iota(jnp.int32, (tq, span), 0)
        kpos = k0 + lax.broadcasted_iota(jnp.int32, (tq, span), 1)
        ok1 = jnp.abs(qpos - kpos) <= WINDOW
        ok = jnp.concatenate([ok1] * group, axis=0)
    scale = HEAD_DIM ** -0.5

    def dup(x, g):
        sw = pltpu.roll(x, HEAD_DIM, axis=1)
        return jnp.where(low, x, sw) if g == 0 else jnp.where(low, sw, x)

    def with_ones(x, g):
        return jnp.where(low, x if g == 0 else pltpu.roll(x, HEAD_DIM, axis=1), 1.0)

    gs = range(SWA_KV_HEADS)
    qs, sk = [], []
    for g in gs:
        rows, sinks = [], []
        for hh in range(group):
            h = g * group + hh
            j = h // 2
            q2 = q[:, j * LANE:(j + 1) * LANE]
            if has_local:
                q2 = _rope(q2, cos_q, sin_q, fq)
            q2 = q2 * scale
            rows.append(jnp.where(low, q2, 0.0) if h % 2 == 0 else jnp.where(low, 0.0, q2))
            sinks.append(jnp.broadcast_to(sink_ref[h:h + 1, 0:1], (tq, 1)))
        qs.append(jnp.concatenate(rows, axis=0))
        sk.append(jnp.concatenate(sinks, axis=0))
    s_ctx = [_mm_nt(qs[g], dup(kc, g)) for g in gs]
    m = [jnp.maximum(jnp.max(s_ctx[g], -1, keepdims=True), sk[g]) for g in gs]
    if has_local:
        s_loc = [jnp.where(ok, _mm_nt(qs[g], dup(k, g)), -jnp.inf) for g in gs]
        m = [jnp.maximum(m[g], jnp.max(s_loc[g], -1, keepdims=True)) for g in gs]
    acc = [_mm(jnp.exp(s_ctx[g] - m[g]), with_ones(vc, g)) for g in gs]
    if has_local:
        acc = [acc[g] + _mm(jnp.exp(s_loc[g] - m[g]), with_ones(v, g)) for g in gs]
    den = [pltpu.roll(acc[g], HEAD_DIM, axis=1) + jnp.exp(sk[g] - m[g]) for g in gs]
    outs = [acc[g] / den[g] for g in gs]
    for j in range(n_heads // 2):
        h0, h1 = 2 * j, 2 * j + 1
        a = outs[h0 // group][(h0 % group) * tq:(h0 % group + 1) * tq]
        b = outs[h1 // group][(h1 % group) * tq:(h1 % group + 1) * tq]
        o_ref[0, :, j * LANE:(j + 1) * LANE] = jnp.where(low, a, pltpu.roll(b, HEAD_DIM, axis=1))


def _sink_rows(sink):
    return jnp.broadcast_to(sink.astype(F32)[:, None], (sink.shape[0], LANE))


def windowed_attention(p_lat, p_ctx, sink):
    bsz, L, cols = p_lat.shape
    Lc = p_ctx.shape[1]
    q_cols = cols - 2 * LANE
    n_heads = q_cols // HEAD_DIM
    tq = _row_tile(L, SWA_QUERY_TILE)
    span = min(L, tq + 2 * WINDOW)
    assert q_cols % (2 * LANE) == 0 and tq % SWA_BLOCK == 0
    kvb = q_cols // (2 * LANE)
    cos, sin = rope_tables(L)
    return pl.pallas_call(
        functools.partial(_attn_kernel, has_local=True, L=L, span=span, group=n_heads // SWA_KV_HEADS),
        grid=(bsz, L // tq),
        in_specs=[pl.BlockSpec((1, tq, q_cols), lambda b, i: (b, i, 0)),
                  pl.BlockSpec((1, L, 2 * LANE), lambda b, i: (b, 0, kvb)),
                  pl.BlockSpec((1, Lc, 2 * LANE), lambda b, i: (b, 0, kvb)),
                  pl.BlockSpec((L, LANE), lambda b, i: (0, 0)),
                  pl.BlockSpec((L, LANE), lambda b, i: (0, 0)),
                  pl.BlockSpec((n_heads, LANE), lambda b, i: (0, 0))],
        out_specs=pl.BlockSpec((1, tq, q_cols), lambda b, i: (b, i, 0)),
        out_shape=jax.ShapeDtypeStruct((bsz, L, q_cols), F32),
        compiler_params=_params("parallel", "parallel"),
        name="windowed_attention",
    )(p_lat, p_lat, p_ctx, cos, sin, _sink_rows(sink))


def context_attention(p_ctx, sink):
    bsz, Lc, cols = p_ctx.shape
    q_cols = cols - 2 * LANE
    n_heads = q_cols // HEAD_DIM
    tq = min(Lc, SWA_BLOCK)
    kvb = q_cols // (2 * LANE)
    return pl.pallas_call(
        functools.partial(_attn_kernel, has_local=False, L=Lc, span=0, group=n_heads // SWA_KV_HEADS),
        grid=(bsz, Lc // tq),
        in_specs=[pl.BlockSpec((1, tq, q_cols), lambda b, i: (b, i, 0)),
                  pl.BlockSpec((1, Lc, 2 * LANE), lambda b, i: (b, 0, kvb)),
                  pl.BlockSpec((n_heads, LANE), lambda b, i: (0, 0))],
        out_specs=pl.BlockSpec((1, tq, q_cols), lambda b, i: (b, i, 0)),
        out_shape=jax.ShapeDtypeStruct((bsz, Lc, q_cols), F32),
        compiler_params=_params("parallel", "parallel"),
        name="context_attention",
    )(p_ctx, p_ctx, _sink_rows(sink))


def _rwkv_feat_kernel(p_ref, hp_ref, hn_ref, mu_ref, w0_ref, w2_ref, a0_ref, a2_ref, g2_ref, kk_ref, ka_ref,
                      rk_ref, e_ref, r_o, v_o, kkn_o, g_o, bonus_o, lw_o, kd_o, bd_o, *, rw_w):
    p = p_ref[0]
    p_prev, p_next = _neighbours(p, *_halo_rows(hp_ref, hn_ref))
    p = p + mu_ref[...] * (0.5 * (p_prev + p_next) - p)
    o1, o2, o3 = rw_w, 2 * rw_w, 3 * rw_w
    o4 = o3 + 2 * RW_DECAY_LORA
    o5 = o4 + 2 * RW_AAA_LORA
    r, k, v = p[:, :o1], p[:, o1:o2], p[:, o2:o3]
    wd, ad, gd = p[:, o3:o4], p[:, o4:o5], p[:, o5:]
    w_log = -_softplus(-(w0_ref[...] + _mm(jnp.tanh(wd), w2_ref[...]))) - 0.5
    lw = -jnp.exp(w_log)
    a = _sigmoid(a0_ref[...] + _mm(ad, a2_ref[...]))
    e = e_ref[...]
    kq = k * kk_ref[...]
    kkn = kq * lax.rsqrt(_mm_exact_rhs(kq * kq, e) + 1e-6)
    r_o[0] = r
    v_o[0] = v
    kkn_o[0] = kkn
    g_o[0] = _mm(_sigmoid(gd), g2_ref[...])
    ka = ka_ref[...]
    rk = r * rk_ref[...]
    acc = jnp.zeros_like(r)
    for d in range(2):
        a_d = a[:, d * rw_w:(d + 1) * rw_w]
        k_d = k * (1.0 + (a_d - 1.0) * ka)
        lw_o[d, 0] = lw[:, d * rw_w:(d + 1) * rw_w]
        kd_o[d, 0] = k_d
        bd_o[d, 0] = kkn * a_d
        acc = acc + rk * k_d
    bonus_o[0] = _mm_exact_rhs(acc, e) * v


def _head_indicator(w, head):
    hid = jnp.arange(w) // head
    return (hid[:, None] == hid[None, :]).astype(BF16)


def rwkv_features(p, mu, w0, w2, a0, a2, g2, k_k, k_a, r_k):
    bsz, T, cols = p.shape
    rw_w = g2.shape[1]
    tm = _row_tile(T, 256)
    w2_bd = jnp.zeros((2 * RW_DECAY_LORA, 2 * rw_w), F32)
    w2_bd = w2_bd.at[:RW_DECAY_LORA, :rw_w].set(w2[0]).at[RW_DECAY_LORA:, rw_w:].set(w2[1]).astype(BF16)
    a2_bd = jnp.zeros((2 * RW_AAA_LORA, 2 * rw_w), F32)
    a2_bd = a2_bd.at[:RW_AAA_LORA, :rw_w].set(a2[0]).at[RW_AAA_LORA:, rw_w:].set(a2[1]).astype(BF16)
    vec = lambda t: t.reshape(1, -1).astype(F32)
    full = lambda shape: pl.BlockSpec(shape, lambda b, i: (0,) * len(shape))
    tok = pl.BlockSpec((1, tm, rw_w), lambda b, i: (b, i, 0))
    tok2 = pl.BlockSpec((2, 1, tm, rw_w), lambda b, i: (0, b, i, 0))
    s1 = jax.ShapeDtypeStruct((bsz, T, rw_w), F32)
    s2 = jax.ShapeDtypeStruct((2, bsz, T, rw_w), F32)
    return pl.pallas_call(
        functools.partial(_rwkv_feat_kernel, rw_w=rw_w),
        grid=(bsz, T // tm),
        in_specs=_halo_specs(tm, T, cols) + [
            full((1, cols)), full((1, 2 * rw_w)), full(w2_bd.shape), full((1, 2 * rw_w)),
            full(a2_bd.shape), full(g2.shape), full((1, rw_w)), full((1, rw_w)), full((1, rw_w)),
            full((rw_w, rw_w))],
        out_specs=[tok, tok, tok, tok, tok, tok2, tok2, tok2],
        out_shape=[s1, s1, s1, s1, s1, s2, s2, s2],
        compiler_params=_params("parallel", "parallel"),
        name="rwkv_features",
    )(p, p, p, vec(mu), vec(w0), w2_bd, vec(a0), a2_bd, g2.astype(BF16), vec(k_k), vec(k_a), vec(r_k),
      _head_indicator(rw_w, RW_HEAD))


def _rwkv_scan_kernel(rc, vc, kkc, lwc, kdc, bdc, rl, vl, kkl, lwl, kdl, bdl, y_ref, s_ref, *, n_ctx, bb):
    d = pl.program_id(1)
    c = pl.program_id(2)
    C = CHUNK
    n_pairs = s_ref.shape[0] // bb

    @pl.when(c == 0)
    def _():
        s_ref[...] = jnp.zeros_like(s_ref)

    is_ctx = c < n_ctx
    pick = lambda a, b: jnp.where(is_ctx, a, b)
    rev = d == 1
    ti = lax.broadcasted_iota(jnp.int32, (C, C), 0)
    tj = lax.broadcasted_iota(jnp.int32, (C, C), 1)
    tri = jnp.where(jnp.where(rev, tj - ti, ti - tj) >= 0, 1.0, 0.0).astype(BF16)

    def features(bi):
        r = pick(rc[bi], rl[bi])
        v = pick(vc[bi], vl[bi])
        kk = pick(kkc[bi], kkl[bi])
        lw = pick(lwc[0, bi], lwl[0, bi])
        kd = pick(kdc[0, bi], kdl[0, bi])
        bd = pick(bdc[0, bi], bdl[0, bi])
        linc = _mm_exact_lhs(tri, lw)
        ltot = jnp.sum(lw, axis=0, keepdims=True)
        g_inv = jnp.exp(-linc)
        g_end = jnp.exp(ltot - linc)
        return (kk * jnp.exp(linc - lw), r * jnp.exp(linc), kd * g_inv, bd * g_inv, kd * g_end, bd * g_end, v,
                jnp.exp(ltot))

    feats = [features(bi) for bi in range(bb)]

    n = 2 * C
    rid = lax.broadcasted_iota(jnp.int32, (n, n), 0)
    cid = lax.broadcasted_iota(jnp.int32, (n, n), 1)
    same = (rid >> 6) == (cid >> 6)
    diff = jnp.where(rev, cid - rid, rid - cid)
    strict = lambda m: jnp.where(same, jnp.where(diff > 0, m, 0.0), 0.0)
    incl = lambda m: jnp.where(same, jnp.where(diff >= 0, m, 0.0), 0.0)
    first = lax.broadcasted_iota(jnp.int32, (1, LANE), 1) < RW_HEAD

    def stack(x):
        return jnp.concatenate([jnp.where(first, x, 0.0), jnp.where(first, 0.0, x)], axis=0)

    prob = [(bi, slice(p * LANE, (p + 1) * LANE)) for bi in range(bb) for p in range(n_pairs)]
    qs = range(len(prob))
    stk = [[stack(t[:, sl]) for t in feats[bi][:7]] for bi, sl in prob]
    kks, rs, kis, bis, kes, bes, vs = ([stk[q][i] for q in qs] for i in range(7))
    m = [_mm_nt(jnp.concatenate([kks[q], rs[q]], axis=0), jnp.concatenate([kis[q], bis[q]], axis=0)) for q in qs]
    a_k = [strict(t[:n, :n]) for t in m]
    a_b = [strict(t[:n, n:]) for t in m]
    b_k = [incl(t[n:, :n]) for t in m]
    b_b = [incl(t[n:, n:]) for t in m]
    t_inv = _unit_tri_inverse(a_b, rid, cid, C)
    abv = [_mm(jnp.concatenate([a_k[q], b_k[q]], axis=0), vs[q]) for q in qs]
    wu = [_mm(t_inv[q], jnp.concatenate([kks[q], abv[q][:n]], axis=1)) for q in qs]
    w_s = [t[:, :LANE] for t in wu]
    u0_s = [t[:, LANE:] for t in wu]
    vt = [t.T for t in vs]
    u0_t = [t.T for t in u0_s]
    st = [s_ref[q] for q in qs]
    wr = [_mm_nt(jnp.concatenate([w_s[q], rs[q]], axis=0), st[q]) for q in qs]
    u_s = [wr[q][:n] + u0_s[q] for q in qs]
    u_t = [_mm_nt(st[q], w_s[q]) + u0_t[q] for q in qs]
    y_s = [wr[q][n:] + abv[q][n:] - _mm(b_b[q], u_s[q]) for q in qs]
    for q, (bi, sl) in enumerate(prob):
        s_ref[q] = st[q] * feats[bi][7][:, sl] + _mm(jnp.concatenate([vt[q], -u_t[q]], axis=1),
                                                    jnp.concatenate([kes[q], bes[q]], axis=0))
        y_ref[0, bi, :, sl] = y_s[q][:C] + y_s[q][C:]


def rwkv_scan(f_ctx, f_lat):
    rc, vc, kkc, lwc, kdc, bdc = f_ctx
    rl, vl, kkl, lwl, kdl, bdl = f_lat
    bsz, Lc, w = rc.shape
    L = rl.shape[1]
    C = CHUNK
    assert Lc % C == 0 and L % C == 0 and w % LANE == 0
    nc, nl = Lc // C, L // C
    bb = _row_tile(bsz, RW_SCAN_BATCH)
    ctx_idx, lat_idx = _scan_index_maps(nc, nl)
    sc = pl.BlockSpec((bb, C, w), lambda b, d, c: (b, ctx_idx(d, c), 0))
    dc = pl.BlockSpec((1, bb, C, w), lambda b, d, c: (d, b, ctx_idx(d, c), 0))
    slt = pl.BlockSpec((bb, C, w), lambda b, d, c: (b, lat_idx(d, c), 0))
    dl = pl.BlockSpec((1, bb, C, w), lambda b, d, c: (d, b, lat_idx(d, c), 0))
    return pl.pallas_call(
        functools.partial(_rwkv_scan_kernel, n_ctx=nc, bb=bb),
        grid=(bsz // bb, 2, nc + nl),
        in_specs=[sc, sc, sc, dc, dc, dc, slt, slt, slt, dl, dl, dl],
        out_specs=pl.BlockSpec((1, bb, C, w), lambda b, d, c: (d, b, lat_idx(d, c), 0)),
        out_shape=jax.ShapeDtypeStruct((2, bsz, L, w), F32),
        scratch_shapes=[pltpu.VMEM((bb * (w // LANE), LANE, LANE), F32)],
        compiler_params=_params("parallel", "parallel", "arbitrary"),
        name="rwkv_scan",
    )(rc, vc, kkc, lwc, kdc, bdc, rl, vl, kkl, lwl, kdl, bdl)


def _rwkv_head_out(y, bonus, g, gn_g, gn_b, e):
    inv_n = 1.0 / RW_HEAD
    mu = _mm_exact_rhs(y, e) * inv_n
    yc = y - mu
    var = _mm_exact_rhs(yc * yc, e) * inv_n
    return (yc * lax.rsqrt(var + RW_GN_EPS) * gn_g + gn_b + bonus) * g


def rwkv7_scan_lat(p_lat, p_ctx, mu, w0, w2, a0, a2, g2, k_k, k_a, r_k):
    args = (mu, w0, w2, a0, a2, g2, k_k, k_a, r_k)
    r_l, v_l, kk_l, g_l, bonus_l, lw_l, kd_l, bd_l = rwkv_features(p_lat, *args)
    r_c, v_c, kk_c, _, _, lw_c, kd_c, bd_c = rwkv_features(p_ctx, *args)
    y = rwkv_scan((r_c, v_c, kk_c, lw_c, kd_c, bd_c), (r_l, v_l, kk_l, lw_l, kd_l, bd_l))
    return y, bonus_l, g_l


def _gdn_feat_kernel(p_ref, hp_ref, hn_ref, g_ref, cw_ref, al_ref, dt_ref, q_o, k_o, v_o, gl_o, be_o, *, dn_w):
    x = p_ref[0]
    x_prev, x_next = _neighbours(x, *_halo_rows(hp_ref, hn_ref))
    cw = cw_ref[...]
    z = x_prev * cw[0:1] + x * cw[1:2] + x_next * cw[2:3]
    z = z * _sigmoid(z)
    for h in range(dn_w // DN_HEAD):
        sq = slice(h * DN_HEAD, (h + 1) * DN_HEAD)
        sk = slice(dn_w + h * DN_HEAD, dn_w + (h + 1) * DN_HEAD)
        q = z[:, sq]
        k = z[:, sk]
        q_o[0, :, sq] = q * (lax.rsqrt(jnp.sum(q * q, -1, keepdims=True) + 1e-6) * (DN_HEAD ** -0.5))
        k_o[0, :, sq] = k * lax.rsqrt(jnp.sum(k * k, -1, keepdims=True) + 1e-6)
    v_o[0] = z[:, 2 * dn_w:]
    g = g_ref[0]
    gl_o[0] = -jnp.exp(al_ref[...]) * _softplus(g[:, :LANE] + dt_ref[...])
    be_o[0] = _sigmoid(g[:, LANE:])


def gdn_features(p, conv_w, A_log, dt_bias):
    bsz, T, cols = p.shape
    dn_w = conv_w.shape[1] // 3
    tm = _row_tile(T, 256)
    c3 = 3 * dn_w
    assert (4 * dn_w) % (2 * LANE) == 0
    gcol = (4 * dn_w) // (2 * LANE)
    pad = lambda t: jnp.pad(t.reshape(1, -1).astype(F32), ((0, 0), (0, LANE - t.size)))
    tok = pl.BlockSpec((1, tm, dn_w), lambda b, i: (b, i, 0))
    gt = pl.BlockSpec((1, tm, LANE), lambda b, i: (b, i, 0))
    row = pl.BlockSpec((1, LANE), lambda b, i: (0, 0))
    s1 = jax.ShapeDtypeStruct((bsz, T, dn_w), F32)
    sg = jax.ShapeDtypeStruct((bsz, T, LANE), F32)
    return pl.pallas_call(
        functools.partial(_gdn_feat_kernel, dn_w=dn_w),
        grid=(bsz, T // tm),
        in_specs=_halo_specs(tm, T, c3) + [pl.BlockSpec((1, tm, 2 * LANE), lambda b, i: (b, i, gcol)),
                                          pl.BlockSpec((8, c3), lambda b, i: (0, 0)), row, row],
        out_specs=[tok, tok, tok, gt, gt],
        out_shape=[s1, s1, s1, sg, sg],
        compiler_params=_params("parallel", "parallel"),
        name="gdn_features",
    )(p, p, p, p, jnp.pad(conv_w, ((0, 8 - conv_w.shape[0]), (0, 0))), pad(A_log), pad(dt_bias))


def _gdn_scan_kernel(qc, kc, vc, glc, bec, ql, kl, vl, gll, bel, o_ref, s_ref, *, n_ctx, bb):
    d = pl.program_id(1)
    c = pl.program_id(2)
    C = CHUNK
    heads = s_ref.shape[0] // bb

    @pl.when(c == 0)
    def _():
        s_ref[...] = jnp.zeros_like(s_ref)

    is_ctx = c < n_ctx
    pick = lambda a, b, bi: jnp.where(is_ctx, a[bi], b[bi])
    rev = d == 1

    ti = lax.broadcasted_iota(jnp.int32, (C, C), 0)
    tj = lax.broadcasted_iota(jnp.int32, (C, C), 1)
    tri = jnp.where(jnp.where(rev, tj - ti, ti - tj) >= 0, 1.0, 0.0).astype(BF16)
    lane = lax.broadcasted_iota(jnp.int32, (1, LANE), 1)

    def col(x, h):
        return jnp.sum(jnp.where(lane == d * heads + h, x, 0.0), axis=1, keepdims=True)

    n = 2 * C
    rid = lax.broadcasted_iota(jnp.int32, (n, n), 0)
    cid = lax.broadcasted_iota(jnp.int32, (n, n), 1)
    same = (rid >> 6) == (cid >> 6)
    diff = jnp.where(rev, cid - rid, rid - cid)
    keep_incl = jnp.where(same, jnp.where(diff >= 0, 1.0, 0.0), 0.0)
    eye = jnp.where(rid == cid, 1.0, 0.0)

    hs = lambda h: slice(h * DN_HEAD, (h + 1) * DN_HEAD)

    prob = [(bi, p) for bi in range(bb) for p in range(heads // 2)]
    qs = range(len(prob))
    k_s, q_s, v_s, g_col, b_col, ge_col, g_end = [], [], [], [], [], [], []
    for bi in range(bb):
        q, k, v = pick(qc, ql, bi), pick(kc, kl, bi), pick(vc, vl, bi)
        gl, be = pick(glc, gll, bi), pick(bec, bel, bi)
        g_cum = _mm_exact_lhs(tri, gl)
        g_tot = jnp.sum(gl, axis=0, keepdims=True)
        for p in range(heads // 2):
            h0, h1 = 2 * p, 2 * p + 1
            stack = lambda x: jnp.concatenate([x[:, hs(h0)], x[:, hs(h1)]], axis=0)
            k_s.append(stack(k))
            q_s.append(stack(q))
            v_s.append(stack(v))
            g_col.append(jnp.concatenate([col(g_cum, h0), col(g_cum, h1)], axis=0))
            b_col.append(jnp.concatenate([col(be, h0), col(be, h1)], axis=0))
            gt = [col(g_tot, h0), col(g_tot, h1)]
            ge_col.append(jnp.concatenate([jnp.broadcast_to(t, (C, 1)) for t in gt], axis=0))
            g_end.append([jnp.exp(t) for t in gt])
    g_row = [jnp.broadcast_to(g_col[i], (n, n)).T for i in qs]
    decay = [keep_incl * jnp.exp(keep_incl * (g_col[i] - g_row[i])) for i in qs]
    kq = [_mm_nt(jnp.concatenate([k_s[i], q_s[i]], axis=0), k_s[i]) for i in qs]
    a = [(1.0 - eye) * b_col[i] * kq[i][:n] * decay[i] for i in qs]
    qk = [kq[i][n:] * decay[i] for i in qs]
    t_inv = _unit_tri_inverse(a, rid, cid, C)
    eg = [jnp.exp(g_col[i]) for i in qs]
    uw = [_mm(t_inv[i], jnp.concatenate([b_col[i] * v_s[i], (b_col[i] * eg[i]) * k_s[i]], axis=1)) for i in qs]
    u0 = [t[:, :DN_HEAD] for t in uw]
    w = [t[:, DN_HEAD:] for t in uw]
    q_dec = [q_s[i] * eg[i] for i in qs]
    k_dec_t = [_mm_nt(eye, k_s[i] * jnp.exp(ge_col[i] - g_col[i])) for i in qs]
    head_cols = lax.broadcasted_iota(jnp.int32, (1, n), 1) >> 6

    st = [[s_ref[2 * i + t] for t in range(2)] for i in qs]
    ws = [[_mm(jnp.concatenate([w[i][t * C:(t + 1) * C], q_dec[i][t * C:(t + 1) * C]], axis=0), st[i][t])
           for t in range(2)] for i in qs]
    u = [jnp.concatenate([u0[i][:C] - ws[i][0][:C], u0[i][C:] - ws[i][1][:C]], axis=0) for i in qs]
    o = [jnp.concatenate([ws[i][0][C:], ws[i][1][C:]], axis=0) + _mm(qk[i], u[i]) for i in qs]
    for i, (bi, p) in enumerate(prob):
        for t in range(2):
            kd = jnp.where(head_cols == t, k_dec_t[i], 0.0)
            s_ref[2 * i + t] = g_end[i][t] * st[i][t] + _mm(kd, u[i])
            o_ref[0, bi, :, hs(2 * p + t)] = o[i][t * C:(t + 1) * C]


def gdn_scan(f_ctx, f_lat):
    qc, kc, vc, glc, bec = f_ctx
    ql, kl, vl, gll, bel = f_lat
    bsz, Lc, w = qc.shape
    L = ql.shape[1]
    C = CHUNK
    heads = w // DN_HEAD
    assert Lc % C == 0 and L % C == 0 and heads % 2 == 0
    nc, nl = Lc // C, L // C
    bb = _row_tile(bsz, DN_SCAN_BATCH)
    ctx_idx, lat_idx = _scan_index_maps(nc, nl)
    sc = pl.BlockSpec((bb, C, w), lambda b, d, c: (b, ctx_idx(d, c), 0))
    gc = pl.BlockSpec((bb, C, LANE), lambda b, d, c: (b, ctx_idx(d, c), 0))
    sl = pl.BlockSpec((bb, C, w), lambda b, d, c: (b, lat_idx(d, c), 0))
    gl = pl.BlockSpec((bb, C, LANE), lambda b, d, c: (b, lat_idx(d, c), 0))
    return pl.pallas_call(
        functools.partial(_gdn_scan_kernel, n_ctx=nc, bb=bb),
        grid=(bsz // bb, 2, nc + nl),
        in_specs=[sc, sc, sc, gc, gc, sl, sl, sl, gl, gl],
        out_specs=pl.BlockSpec((1, bb, C, w), lambda b, d, c: (d, b, lat_idx(d, c), 0)),
        out_shape=jax.ShapeDtypeStruct((2, bsz, L, w), F32),
        scratch_shapes=[pltpu.VMEM((bb * heads, DN_HEAD, DN_HEAD), F32)],
        compiler_params=_params("parallel", "parallel", "arbitrary"),
        name="gdn_scan",
    )(qc, kc, vc, glc, bec, ql, kl, vl, gll, bel)


def _gdn_head_out(o, z, norm_g):
    outs = []
    for h in range(o.shape[1] // DN_HEAD):
        sl = slice(h * DN_HEAD, (h + 1) * DN_HEAD)
        oh = o[:, sl]
        zh = z[:, sl]
        outs.append(oh * lax.rsqrt(jnp.mean(oh * oh, -1, keepdims=True) + 1e-6) * norm_g * (zh * _sigmoid(zh)))
    return jnp.concatenate(outs, axis=1)


def gdn_scan_lat(p_lat, p_ctx, conv_w, A_log, dt_bias):
    f_lat = gdn_features(p_lat, conv_w, A_log, dt_bias)
    f_ctx = gdn_features(p_ctx, conv_w, A_log, dt_bias)
    return gdn_scan(f_ctx, f_lat)


def _odd_out_ln_kernel(y_ref, bonus_ref, g_ref, gg_ref, gb_ref, e_ref, o_ref, z_ref, ng_ref, wa_ref, wb_ref,
                       x_ref, gate_ref, lg_ref, lb_ref, out_ref):
    y_c = _rwkv_head_out(y_ref[0, 0] + y_ref[1, 0], bonus_ref[0], g_ref[0], gg_ref[...], gb_ref[...], e_ref[...])
    y_d = _gdn_head_out(o_ref[0, 0] + o_ref[1, 0], z_ref[0], ng_ref[...])
    y = _mm(y_c, wa_ref[...]) + _mm(y_d, wb_ref[...])
    z = DEEPNORM_ALPHA * x_ref[0] + gate_ref[0] * y
    out_ref[0] = _layer_norm_rows(z, lg_ref[...], lb_ref[...])


def odd_out_proj_ln(y, bonus, g, gn_g, gn_b, o, p_dn, norm_g, w, x, gate, ln_g, ln_b):
    bsz, L, d = x.shape
    rw_w, dn_w = y.shape[-1], o.shape[-1]
    tm = _row_tile(L, 256)
    z_block = 3
    two = lambda w_: pl.BlockSpec((2, 1, tm, w_), lambda b, i: (0, b, i, 0))
    tok = lambda w_: pl.BlockSpec((1, tm, w_), lambda b, i: (b, i, 0))
    row = lambda w_: pl.BlockSpec((1, w_), lambda b, i: (0, 0))
    full = lambda a: pl.BlockSpec(a.shape, lambda b, i: (0, 0))
    wa, wb = w[:rw_w], w[rw_w:]
    e = _head_indicator(rw_w, RW_HEAD)
    return pl.pallas_call(
        _odd_out_ln_kernel,
        grid=(bsz, L // tm),
        in_specs=[two(rw_w), tok(rw_w), tok(rw_w), row(rw_w), row(rw_w), full(e), two(dn_w),
                  pl.BlockSpec((1, tm, dn_w), lambda b, i: (b, i, z_block)), row(DN_HEAD), full(wa), full(wb),
                  tok(d), pl.BlockSpec((1, 1, d), lambda b, i: (b, 0, 0)), row(d), row(d)],
        out_specs=tok(d),
        out_shape=jax.ShapeDtypeStruct((bsz, L, d), F32),
        compiler_params=_params("parallel", "parallel"),
        name="odd_out_proj_ln",
    )(y, bonus, g, gn_g.reshape(1, rw_w), gn_b.reshape(1, rw_w), e, o, p_dn, norm_g.reshape(1, DN_HEAD), wa, wb,
      x, gate, ln_g.reshape(1, d), ln_b.reshape(1, d))


def _gdn_in_weight(w_dn, dn_w, heads):
    padw = ((0, 0), (0, LANE - 2 * heads))
    g = w_dn[:, 4 * dn_w:]
    return jnp.concatenate([w_dn[:, :4 * dn_w], jnp.pad(g[:, :2 * heads], padw), jnp.pad(g[:, 2 * heads:], padw)], 1)


def kernel(x, c, ctx, c_ctx, mod_w, mod_b, ln_g, ln_b, mlp_w1, mlp_w2, e_w_in, e_w_out, hy_conv, hy_ffn_w1,
           hy_ffn_b1, hy_ffn_w2, hy_ffn_b2, hy_sin_freq, hy_ffn_w3, hy_decay, hy_bias, attn_sink, o_w_in,
           o_w_out, rw_mu, rw_w0, rw_w2, rw_a0, rw_a2, rw_g2, rw_kk, rw_ka, rw_rk, rw_lnx_g, rw_lnx_b,
           dn_conv, dn_A_log, dn_dt_bias, dn_norm_g):
    bsz, L, d = x.shape
    assert mod_w.shape[0] == DEPTH == 2
    cc = jnp.concatenate([c, c_ctx[None, :]], axis=0)
    cc = jnp.pad(cc, ((0, (-cc.shape[0]) % 8), (0, 0)))

    def modulation(i):
        m = mod_vectors(cc, mod_w[i], mod_b[i])
        m_lat = m[:bsz].reshape(bsz, 1, N_MOD, d)
        m_ctx = jnp.broadcast_to(m[bsz].reshape(1, 1, N_MOD, d), (bsz, 1, N_MOD, d))
        return [m_lat[:, :, t] for t in range(N_MOD)], [m_ctx[:, :, t] for t in range(N_MOD)]

    (sh1, sc1, g1, sh2, sc2, g2), (csh1, csc1, cg1, csh2, csc2, cg2) = modulation(0)
    hy_cols = hy_conv.shape[2]
    w_in = e_w_in[0].astype(BF16)
    ws = [w_in[:, :hy_cols], w_in[:, hy_cols:]]
    w_out = e_w_out[0].astype(BF16)
    w1, w2 = mlp_w1[0].astype(BF16), mlp_w2[0].astype(BF16)
    hy = (hy_conv[0], hy_ffn_w1[0], hy_ffn_b1[0], hy_ffn_w2[0], hy_ffn_b2[0], hy_sin_freq[0], hy_ffn_w3[0],
          hy_decay[0], hy_bias[0])
    p_hy, p_qkv = mod_linear(x, sc1, sh1, ws)
    pc_hy, pc_qkv = mod_linear(ctx, csc1, csh1, ws)
    y_a = hyena_mixer(p_hy, *hy)
    y_b = windowed_attention(p_qkv, pc_qkv, attn_sink[0])
    yc_a = hyena_mixer(pc_hy, *hy)
    yc_b = context_attention(pc_qkv, attn_sink[0])
    x = out_proj_ln(y_a, y_b, w_out, x, g1, ln_g[0, 0], ln_b[0, 0])
    x = mlp_ln(x, sc2, sh2, g2, w1, w2, ln_g[0, 1], ln_b[0, 1])
    ctx = out_proj_ln(yc_a, yc_b, w_out, ctx, cg1, ln_g[0, 0], ln_b[0, 0])
    ctx = mlp_ln(ctx, csc2, csh2, cg2, w1, w2, ln_g[0, 1], ln_b[0, 1])

    (sh1, sc1, g1, sh2, sc2, g2), (csh1, csc1, _, _, _, _) = modulation(1)
    rw_cols = rw_mu.shape[1]
    dn_w = dn_conv.shape[2] // 3
    w_in = o_w_in[0]
    ws = [w_in[:, :rw_cols].astype(BF16), _gdn_in_weight(w_in[:, rw_cols:], dn_w, dn_w // DN_HEAD).astype(BF16)]
    w_out = o_w_out[0].astype(BF16)
    w1, w2 = mlp_w1[1].astype(BF16), mlp_w2[1].astype(BF16)
    p_rw, p_dn = mod_linear(x, sc1, sh1, ws)
    pc_rw, pc_dn = mod_linear(ctx, csc1, csh1, ws)
    y_rw, bonus, g_rw = rwkv7_scan_lat(p_rw, pc_rw, rw_mu[0], rw_w0[0], rw_w2[0], rw_a0[0], rw_a2[0], rw_g2[0],
                                       rw_kk[0], rw_ka[0], rw_rk[0])
    o_dn = gdn_scan_lat(p_dn, pc_dn, dn_conv[0], dn_A_log[0], dn_dt_bias[0])
    x = odd_out_proj_ln(y_rw, bonus, g_rw, rw_lnx_g[0], rw_lnx_b[0], o_dn, p_dn, dn_norm_g[0], w_out, x, g1,
                        ln_g[1, 0], ln_b[1, 0])
    x = mlp_ln(x, sc2, sh2, g2, w1, w2, ln_g[1, 1], ln_b[1, 1])
    return x
```

```python
import functools
import math

import jax
import jax.numpy as jnp
from jax import lax
from jax.experimental import pallas as pl
from jax.experimental.pallas import tpu as pltpu

F32 = jnp.float32
BF16 = jnp.bfloat16

DEPTH = 2
N_MOD = 6
DEEPNORM_ALPHA = (2.0 * DEPTH) ** 0.25
LN_EPS = 1e-5

HY_BANDS = 16
HEAD_DIM = 64
SWA_KV_HEADS = 2
WINDOW = 128
SWA_BLOCK = 128
SWA_QUERY_TILE = 128
GRID_W = 64
ROPE_BASE = 10000.0

RW_HEAD = 64
RW_DECAY_LORA = 64
RW_AAA_LORA = 64
RW_GN_EPS = 64e-5

DN_HEAD = 128
CHUNK = 64
RW_SCAN_BATCH = 4
DN_SCAN_BATCH = 8

LANE = 128
MXU_WIDTH = 256
VMEM_LIMIT = 56 * 1024 * 1024


def _row_tile(n, cap):
    t = min(n, cap)
    while n % t:
        t //= 2
    return t


def _params(*sem):
    return pltpu.CompilerParams(dimension_semantics=sem, vmem_limit_bytes=VMEM_LIMIT)


def _mm(a, b):
    return jnp.dot(a.astype(BF16), b.astype(BF16), preferred_element_type=F32)


def _mm_nt(a, b):
    return lax.dot_general(a.astype(BF16), b.astype(BF16), (((1,), (1,)), ((), ())), preferred_element_type=F32)


def _split2(x):
    hi = x.astype(BF16)
    return hi, (x - hi.astype(F32)).astype(BF16)


def _split3(x):
    hi, r = x.astype(BF16), None
    r = x - hi.astype(F32)
    mid = r.astype(BF16)
    return hi, mid, (r - mid.astype(F32)).astype(BF16)


def _mm_exact_lhs(a_bf16, x):
    return sum(jnp.dot(a_bf16, t, preferred_element_type=F32) for t in _split3(x))


def _mm_exact_rhs(x, b_bf16):
    return sum(jnp.dot(t, b_bf16, preferred_element_type=F32) for t in _split3(x))


def _mm_f32(a, b):
    ah, al = _split2(a)
    bh, bl = _split2(b)
    dot = lambda x, y: jnp.dot(x, y, preferred_element_type=F32)
    return dot(ah, bh) + dot(ah, bl) + dot(al, bh)


def _sigmoid(x):
    return 1.0 / (1.0 + jnp.exp(-x))


def _softplus(x):
    return jnp.maximum(x, 0.0) + jnp.log(1.0 + jnp.exp(-jnp.abs(x)))


def _neighbours(x, prev_row, next_row):
    n = x.shape[0]
    rows = lax.broadcasted_iota(jnp.int32, (n, 1), 0)
    x_prev = jnp.where(rows == 0, prev_row, pltpu.roll(x, 1, axis=0))
    x_next = jnp.where(rows == n - 1, next_row, pltpu.roll(x, n - 1, axis=0))
    return x_prev, x_next


def _halo_rows(hp_ref, hn_ref):
    i = pl.program_id(1)
    nt = pl.num_programs(1)
    return jnp.where(i > 0, hp_ref[0][7:8, :], 0.0), jnp.where(i < nt - 1, hn_ref[0][0:1, :], 0.0)


def _halo_specs(tm, T, cols, col_block=0):
    hb = tm // 8
    return [pl.BlockSpec((1, tm, cols), lambda b, i: (b, i, col_block)),
            pl.BlockSpec((1, 8, cols), lambda b, i: (b, jnp.maximum(i * hb - 1, 0), col_block)),
            pl.BlockSpec((1, 8, cols), lambda b, i: (b, jnp.minimum((i + 1) * hb, T // 8 - 1), col_block))]


def _unit_tri_inverse(a_list, rid, cid, blk):
    eye = jnp.where(rid == cid, 1.0, 0.0)
    pair = (rid >> 1) == (cid >> 1)
    d_list = [eye - jnp.where(pair, a, 0.0) for a in a_list]
    s, sh = 2, 1
    while s < blk:
        join = jnp.where((rid >> (sh + 1)) == (cid >> (sh + 1)), jnp.where((rid >> sh) != (cid >> sh), 1.0, 0.0), 0.0)
        dq = [_mm(d, a * join) for d, a in zip(d_list, a_list)]
        d_list = [d - _mm(t, d) for t, d in zip(dq, d_list)]
        s, sh = 2 * s, sh + 1
    return d_list


def _scan_index_maps(nc, nl):
    def ctx_idx(d, c):
        cc = jnp.minimum(c, nc - 1)
        return jnp.where(d == 1, nc - 1 - cc, cc)

    def lat_idx(d, c):
        cl = jnp.maximum(c - nc, 0)
        return jnp.where(d == 1, nl - 1 - cl, cl)

    return ctx_idx, lat_idx


def _mod_kernel(c_ref, w_ref, b_ref, o_ref):
    c = c_ref[...]
    o_ref[...] = _mm(c * _sigmoid(c), w_ref[...]) + b_ref[...]


def mod_vectors(cc, w, b):
    r, d = cc.shape
    n = w.shape[1]
    tn = _row_tile(n, 1024)
    return pl.pallas_call(
        _mod_kernel,
        grid=(n // tn,),
        in_specs=[pl.BlockSpec((r, d), lambda j: (0, 0)),
                  pl.BlockSpec((d, tn), lambda j: (0, j)),
                  pl.BlockSpec((1, tn), lambda j: (0, j))],
        out_specs=pl.BlockSpec((r, tn), lambda j: (0, j)),
        out_shape=jax.ShapeDtypeStruct((r, n), F32),
        name="mod_vectors",
    )(cc, w, b.reshape(1, n))


def _mod_linear_kernel(x_ref, sc_ref, sh_ref, *refs):
    n = len(refs) // 2
    h = (x_ref[0] * (1.0 + sc_ref[0]) + sh_ref[0]).astype(BF16)
    for w_ref, o_ref in zip(refs[:n], refs[n:]):
        o_ref[0] = jnp.dot(h, w_ref[...], preferred_element_type=F32)


def mod_linear(x, sc, sh, ws):
    bsz, L, d = x.shape
    tm = _row_tile(L, 512)
    vec = pl.BlockSpec((1, 1, d), lambda b, i: (b, 0, 0))
    return pl.pallas_call(
        _mod_linear_kernel,
        grid=(bsz, L // tm),
        in_specs=[pl.BlockSpec((1, tm, d), lambda b, i: (b, i, 0)), vec, vec]
        + [pl.BlockSpec(w.shape, lambda b, i: (0, 0)) for w in ws],
        out_specs=[pl.BlockSpec((1, tm, w.shape[1]), lambda b, i: (b, i, 0)) for w in ws],
        out_shape=[jax.ShapeDtypeStruct((bsz, L, w.shape[1]), F32) for w in ws],
        compiler_params=_params("parallel", "parallel"),
        name="mod_linear",
    )(x, sc, sh, *ws)


def _layer_norm_rows(z, g, b):
    mu = jnp.mean(z, -1, keepdims=True)
    zc = z - mu
    var = jnp.mean(zc * zc, -1, keepdims=True)
    return zc * lax.rsqrt(var + LN_EPS) * g + b


def _out_ln_kernel(ya_ref, yb_ref, wa_ref, wb_ref, x_ref, g_ref, lg_ref, lb_ref, o_ref):
    y = _mm(ya_ref[0], wa_ref[...]) + _mm(yb_ref[0], wb_ref[...])
    z = DEEPNORM_ALPHA * x_ref[0] + g_ref[0] * y
    o_ref[0] = _layer_norm_rows(z, lg_ref[...], lb_ref[...])


def out_proj_ln(ya, yb, w, x, gate, ln_g, ln_b):
    bsz, L, d = x.shape
    ka, kb = ya.shape[-1], yb.shape[-1]
    tm = _row_tile(L, 512)
    row = pl.BlockSpec((1, d), lambda b, i: (0, 0))
    return pl.pallas_call(
        _out_ln_kernel,
        grid=(bsz, L // tm),
        in_specs=[pl.BlockSpec((1, tm, ka), lambda b, i: (b, i, 0)),
                  pl.BlockSpec((1, tm, kb), lambda b, i: (b, i, 0)),
                  pl.BlockSpec((ka, d), lambda b, i: (0, 0)),
                  pl.BlockSpec((kb, d), lambda b, i: (0, 0)),
                  pl.BlockSpec((1, tm, d), lambda b, i: (b, i, 0)),
                  pl.BlockSpec((1, 1, d), lambda b, i: (b, 0, 0)), row, row],
        out_specs=pl.BlockSpec((1, tm, d), lambda b, i: (b, i, 0)),
        out_shape=jax.ShapeDtypeStruct((bsz, L, d), F32),
        compiler_params=_params("parallel", "parallel"),
        name="out_proj_ln",
    )(ya, yb, w[:ka], w[ka:], x, gate, ln_g.reshape(1, d), ln_b.reshape(1, d))


def _mlp_ln_kernel(x_ref, sc_ref, sh_ref, g_ref, w1_ref, w2_ref, lg_ref, lb_ref, o_ref, *, ff_tile):
    x = x_ref[0]
    h = (x * (1.0 + sc_ref[0]) + sh_ref[0]).astype(BF16)
    d_ff = w1_ref.shape[1]
    acc = jnp.zeros(x.shape, F32)
    for c in range(d_ff // ff_tile):
        a = jnp.dot(h, w1_ref[:, c * ff_tile:(c + 1) * ff_tile], preferred_element_type=F32)
        a = jnp.maximum(a, 0.0)
        a = (a * a).astype(BF16)
        acc = acc + jnp.dot(a, w2_ref[c * ff_tile:(c + 1) * ff_tile, :], preferred_element_type=F32)
    z = DEEPNORM_ALPHA * x + g_ref[0] * acc
    o_ref[0] = _layer_norm_rows(z, lg_ref[...], lb_ref[...])


def mlp_ln(x, sc, sh, gate, w1, w2, ln_g, ln_b):
    bsz, L, d = x.shape
    d_ff = w1.shape[1]
    tm = _row_tile(L, 512)
    ff_tile = _row_tile(d_ff, 1024)
    once = dict(pipeline_mode=pl.Buffered(1))
    vec = pl.BlockSpec((1, 1, d), lambda b, i: (b, 0, 0))
    row = pl.BlockSpec((1, d), lambda b, i: (0, 0))
    return pl.pallas_call(
        functools.partial(_mlp_ln_kernel, ff_tile=ff_tile),
        grid=(bsz, L // tm),
        in_specs=[pl.BlockSpec((1, tm, d), lambda b, i: (b, i, 0)), vec, vec, vec,
                  pl.BlockSpec((d, d_ff), lambda b, i: (0, 0), **once),
                  pl.BlockSpec((d_ff, d), lambda b, i: (0, 0), **once), row, row],
        out_specs=pl.BlockSpec((1, tm, d), lambda b, i: (b, i, 0)),
        out_shape=jax.ShapeDtypeStruct((bsz, L, d), F32),
        compiler_params=_params("parallel", "parallel"),
        name="mlp_ln",
    )(x, sc, sh, gate, w1, w2, ln_g.reshape(1, d), ln_b.reshape(1, d))


def dft_tables(T):
    n = 2 * T
    f = jnp.arange(T, dtype=jnp.int32)
    ang = ((f[:, None] * f[None, :]) % n).astype(F32) * (2.0 * math.pi / n)
    return jnp.cos(ang).astype(BF16), (-jnp.sin(ang)).astype(BF16)


def hyena_pos_features(T):
    t = jnp.arange(T, dtype=F32)
    t_norm = t / max(T - 1, 1)
    bands = jnp.linspace(1e-4, HY_BANDS - 1, HY_BANDS, dtype=F32)
    ang = 2.0 * math.pi * t[:, None] * bands[None, :] / T
    pe = jnp.concatenate([t_norm[:, None], jnp.cos(ang), -jnp.sin(ang)], axis=-1)
    return jnp.pad(pe, ((0, 0), (0, (-pe.shape[1]) % 8)))


def _hy_filter_kernel(pe_ref, w1_ref, b1_ref, w2_ref, b2_ref, fr_ref, w3f_ref, w3b_ref, df_ref, db_ref,
                      cm_ref, sm_ref, kr_o, ki_o, kn_o):
    T = pe_ref.shape[0]
    pe = pe_ref[...]
    fr = fr_ref[...]
    h = jnp.sin(fr * (_mm_f32(pe, w1_ref[...]) + b1_ref[...]))
    h = jnp.sin(fr * (_mm_f32(h, w2_ref[...]) + b2_ref[...]))
    t_norm = pe[:, 0:1]
    h_f = _mm_f32(h, w3f_ref[...]) * jnp.exp(-t_norm * jnp.abs(df_ref[...]))
    h_b = _mm_f32(h, w3b_ref[...]) * jnp.exp(-t_norm * jnp.abs(db_ref[...]))
    row = lax.broadcasted_iota(jnp.int32, (T, 1), 0)
    h_b = jnp.where(row == 0, 0.0, h_b)
    scale = jnp.where(row == 0, 1.0, 2.0) * (1.0 / (2 * T))
    kr_o[...] = _mm(cm_ref[...], h_f + h_b) * scale
    ki_o[...] = _mm(sm_ref[...], h_f - h_b) * scale
    sign = jnp.where((row & 1) == 0, 1.0, -1.0)
    kn = jnp.sum(sign * (h_f + h_b), axis=0, keepdims=True) * (1.0 / (2 * T))
    kn_o[...] = jnp.broadcast_to(kn, kn_o.shape)


def hyena_spectrum(T, cm, sm, w1, b1, w2, b2, freq, w3, decay):
    hy_w = w3.shape[1] // 2
    ct = MXU_WIDTH
    nj = hy_w // ct
    pe = hyena_pos_features(T)
    w1p = jnp.pad(w1, ((0, pe.shape[1] - w1.shape[0]), (0, 0)))
    row = lambda t: t.reshape(1, -1)
    full = lambda a: pl.BlockSpec(a.shape, lambda j: (0,) * a.ndim)
    ins = (pe, w1p, row(b1), w2, row(b2), row(freq))
    return pl.pallas_call(
        _hy_filter_kernel,
        grid=(nj,),
        in_specs=[full(a) for a in ins] + [
            pl.BlockSpec((w3.shape[0], ct), lambda j: (0, j)), pl.BlockSpec((w3.shape[0], ct), lambda j: (0, nj + j)),
            pl.BlockSpec((1, ct), lambda j: (0, j)), pl.BlockSpec((1, ct), lambda j: (0, nj + j)),
            pl.BlockSpec((T, T), lambda j: (0, 0)), pl.BlockSpec((T, T), lambda j: (0, 0))],
        out_specs=[pl.BlockSpec((T, ct), lambda j: (0, j)), pl.BlockSpec((T, ct), lambda j: (0, j)),
                   pl.BlockSpec((8, ct), lambda j: (0, j))],
        out_shape=[jax.ShapeDtypeStruct((T, hy_w), F32), jax.ShapeDtypeStruct((T, hy_w), F32),
                   jax.ShapeDtypeStruct((8, hy_w), F32)],
        compiler_params=_params("arbitrary"),
        name="hyena_spectrum",
    )(*ins, w3, w3, row(decay), row(decay), cm, sm)


def _hy_conv_kernel(x0_ref, x1_ref, xv_ref, c0_ref, c1_ref, cv_ref, bias_ref, kr_ref, ki_ref, kn_ref,
                    cm_ref, sm_ref, o_ref):
    T = x0_ref.shape[1]
    row = lax.broadcasted_iota(jnp.int32, (T, 1), 0)
    zero = jnp.zeros((1, x0_ref.shape[2]), F32)

    def short_conv(x_ref, c_ref):
        x = x_ref[0]
        c = c_ref[...]
        x_prev, x_next = _neighbours(x, zero, zero)
        return x_prev * c[0:1] + x * c[1:2] + x_next * c[2:3]

    u = short_conv(x1_ref, c1_ref) * short_conv(xv_ref, cv_ref)
    ub = u.astype(BF16)
    sign = jnp.where((row & 1) == 0, 1.0, -1.0)
    un = jnp.sum(sign * u, axis=0, keepdims=True)
    y = sign * (un * kn_ref[0:1, :]) + u * bias_ref[...]
    ft = min(T, 512)
    yr, yi = [], []
    for f0 in range(0, T, ft):
        fs = slice(f0, f0 + ft)
        ur = jnp.dot(cm_ref[fs, :], ub, preferred_element_type=F32)
        ui = jnp.dot(sm_ref[fs, :], ub, preferred_element_type=F32)
        kr = kr_ref[fs, :]
        ki = ki_ref[fs, :]
        yr.append((ur * kr - ui * ki).astype(BF16))
        yi.append((ur * ki + ui * kr).astype(BF16))
    y = y + jnp.dot(cm_ref[...], jnp.concatenate(yr, axis=0), preferred_element_type=F32)
    y = y + jnp.dot(sm_ref[...], jnp.concatenate(yi, axis=0), preferred_element_type=F32)
    o_ref[0] = short_conv(x0_ref, c0_ref) * y


def hyena_mixer(p, conv_w, w1, b1, w2, b2, freq, w3, decay, bias):
    bsz, T, _ = p.shape
    hy_w = bias.shape[0]
    ct = MXU_WIDTH
    nj = hy_w // ct
    cm, sm = dft_tables(T)
    kr, ki, kn = hyena_spectrum(T, cm, sm, w1, b1, w2, b2, freq, w3, decay)
    cw = jnp.pad(conv_w, ((0, 8 - conv_w.shape[0]), (0, 0)))
    once = dict(pipeline_mode=pl.Buffered(1))
    xs = lambda part: pl.BlockSpec((1, T, ct), lambda j, b: (b, 0, part * nj + j))
    cs = lambda part: pl.BlockSpec((8, ct), lambda j, b: (0, part * nj + j))
    return pl.pallas_call(
        _hy_conv_kernel,
        grid=(nj, bsz),
        in_specs=[xs(0), xs(1), xs(2), cs(0), cs(1), cs(2),
                  pl.BlockSpec((1, ct), lambda j, b: (0, j)),
                  pl.BlockSpec((T, ct), lambda j, b: (0, j), **once),
                  pl.BlockSpec((T, ct), lambda j, b: (0, j), **once),
                  pl.BlockSpec((8, ct), lambda j, b: (0, j)),
                  pl.BlockSpec((T, T), lambda j, b: (0, 0), **once),
                  pl.BlockSpec((T, T), lambda j, b: (0, 0), **once)],
        out_specs=pl.BlockSpec((1, T, ct), lambda j, b: (b, 0, j)),
        out_shape=jax.ShapeDtypeStruct((bsz, T, hy_w), F32),
        compiler_params=_params("parallel", "parallel"),
        name="hyena_conv",
    )(p, p, p, cw, cw, cw, bias.reshape(1, hy_w), kr, ki, kn, cm, sm)


def rope_tables(L):
    quarter = HEAD_DIM // 4
    pos = jnp.arange(L, dtype=jnp.int32)
    inv_freq = ROPE_BASE ** (-jnp.arange(quarter, dtype=F32) / quarter)
    a_row = (pos // GRID_W).astype(F32)[:, None] * inv_freq[None, :]
    a_col = (pos % GRID_W).astype(F32)[:, None] * inv_freq[None, :]
    ang = jnp.concatenate([a_row, a_row, a_col, a_col], axis=-1)
    sgn = jnp.tile(jnp.concatenate([-jnp.ones(quarter, F32), jnp.ones(quarter, F32)]), 2)
    return jnp.tile(jnp.cos(ang), (1, 2)), jnp.tile(jnp.sin(ang) * sgn, (1, 2))


def _rope(t, cos, sin, first_quarter):
    w = t.shape[1]
    partner = jnp.where(first_quarter, pltpu.roll(t, w - 16, axis=1), pltpu.roll(t, 16, axis=1))
    return t * cos + partner * sin


def _attn_kernel(*refs, has_local, L, span, group):
    if has_local:
        q_ref, kv_ref, kvc_ref, cos_ref, sin_ref, sink_ref, o_ref = refs
    else:
        q_ref, kvc_ref, sink_ref, o_ref = refs
    i = pl.program_id(1)
    tq = q_ref.shape[1]
    lane = lax.broadcasted_iota(jnp.int32, (1, LANE), 1)
    low = lane < HEAD_DIM
    q = q_ref[0]
    n_heads = q.shape[1] // HEAD_DIM
    kvc = kvc_ref[0]
    kc, vc = kvc[:, :LANE], kvc[:, LANE:]
    if has_local:
        fq = (lane & 31) < 16
        q0 = pl.multiple_of(i * tq, tq)
        cos_q = cos_ref[pl.ds(q0, tq), :]
        sin_q = sin_ref[pl.ds(q0, tq), :]
        k0 = pl.multiple_of(jnp.clip(i * tq - WINDOW, 0, L - span), SWA_BLOCK)
        kv = kv_ref[0, pl.ds(k0, span), :]
        k = _rope(kv[:, :LANE], cos_ref[pl.ds(k0, span), :], sin_ref[pl.ds(k0, span), :], fq)
        v = kv[:, LANE:]
        qpos = q0 + lax.broadcasted_iota(jnp.int32, (tq, span), 0)
        kpos = k0 + lax.broadcasted_iota(jnp.int32, (tq, span), 1)
        ok1 = jnp.abs(qpos - kpos) <= WINDOW
        ok = jnp.concatenate([ok1, ok1], axis=0)
    scale = HEAD_DIM ** -0.5

    def dup(x, g):
        sw = pltpu.roll(x, HEAD_DIM, axis=1)
        return jnp.where(low, x, sw) if g == 0 else jnp.where(low, sw, x)

    def with_ones(x, g):
        return jnp.where(low, x if g == 0 else pltpu.roll(x, HEAD_DIM, axis=1), 1.0)

    js = range(n_heads // 2)
    kvg = [(2 * j) // group for j in js]
    qs, sk = [], []
    for j in js:
        q2 = q[:, j * LANE:(j + 1) * LANE]
        if has_local:
            q2 = _rope(q2, cos_q, sin_q, fq)
        q2 = q2 * scale
        qs.append(jnp.concatenate([jnp.where(low, q2, 0.0), jnp.where(low, 0.0, q2)], axis=0))
        sk.append(jnp.concatenate([jnp.broadcast_to(sink_ref[h:h + 1, 0:1], (tq, 1)) for h in (2 * j, 2 * j + 1)],
                                  axis=0))
    kc_g = [dup(kc, g) for g in range(SWA_KV_HEADS)]
    vc_g = [with_ones(vc, g) for g in range(SWA_KV_HEADS)]
    s_ctx = [_mm_nt(qs[j], kc_g[kvg[j]]) for j in js]
    m = [jnp.maximum(jnp.max(s_ctx[j], -1, keepdims=True), sk[j]) for j in js]
    if has_local:
        k_g = [dup(k, g) for g in range(SWA_KV_HEADS)]
        v_g = [with_ones(v, g) for g in range(SWA_KV_HEADS)]
        s_loc = [jnp.where(ok, _mm_nt(qs[j], k_g[kvg[j]]), -jnp.inf) for j in js]
        m = [jnp.maximum(m[j], jnp.max(s_loc[j], -1, keepdims=True)) for j in js]
    acc = [_mm(jnp.exp(s_ctx[j] - m[j]), vc_g[kvg[j]]) for j in js]
    if has_local:
        acc = [acc[j] + _mm(jnp.exp(s_loc[j] - m[j]), v_g[kvg[j]]) for j in js]
    den = [pltpu.roll(acc[j], HEAD_DIM, axis=1) + jnp.exp(sk[j] - m[j]) for j in js]
    for j in js:
        o = acc[j] / den[j]
        o_ref[0, :, j * LANE:(j + 1) * LANE] = jnp.where(low, o[:tq], pltpu.roll(o[tq:], HEAD_DIM, axis=1))


def _sink_rows(sink):
    return jnp.broadcast_to(sink.astype(F32)[:, None], (sink.shape[0], LANE))


def windowed_attention(p_lat, p_ctx, sink):
    bsz, L, cols = p_lat.shape
    Lc = p_ctx.shape[1]
    q_cols = cols - 2 * LANE
    n_heads = q_cols // HEAD_DIM
    tq = _row_tile(L, SWA_QUERY_TILE)
    span = min(L, tq + 2 * WINDOW)
    assert q_cols % (2 * LANE) == 0 and tq % SWA_BLOCK == 0
    kvb = q_cols // (2 * LANE)
    cos, sin = rope_tables(L)
    return pl.pallas_call(
        functools.partial(_attn_kernel, has_local=True, L=L, span=span, group=n_heads // SWA_KV_HEADS),
        grid=(bsz, L // tq),
        in_specs=[pl.BlockSpec((1, tq, q_cols), lambda b, i: (b, i, 0)),
                  pl.BlockSpec((1, L, 2 * LANE), lambda b, i: (b, 0, kvb)),
                  pl.BlockSpec((1, Lc, 2 * LANE), lambda b, i: (b, 0, kvb)),
                  pl.BlockSpec((L, LANE), lambda b, i: (0, 0)),
                  pl.BlockSpec((L, LANE), lambda b, i: (0, 0)),
                  pl.BlockSpec((n_heads, LANE), lambda b, i: (0, 0))],
        out_specs=pl.BlockSpec((1, tq, q_cols), lambda b, i: (b, i, 0)),
        out_shape=jax.ShapeDtypeStruct((bsz, L, q_cols), F32),
        compiler_params=_params("parallel", "parallel"),
        name="windowed_attention",
    )(p_lat, p_lat, p_ctx, cos, sin, _sink_rows(sink))


def context_attention(p_ctx, sink):
    bsz, Lc, cols = p_ctx.shape
    q_cols = cols - 2 * LANE
    n_heads = q_cols // HEAD_DIM
    tq = min(Lc, SWA_BLOCK)
    kvb = q_cols // (2 * LANE)
    return pl.pallas_call(
        functools.partial(_attn_kernel, has_local=False, L=Lc, span=0, group=n_heads // SWA_KV_HEADS),
        grid=(bsz, Lc // tq),
        in_specs=[pl.BlockSpec((1, tq, q_cols), lambda b, i: (b, i, 0)),
                  pl.BlockSpec((1, Lc, 2 * LANE), lambda b, i: (b, 0, kvb)),
                  pl.BlockSpec((n_heads, LANE), lambda b, i: (0, 0))],
        out_specs=pl.BlockSpec((1, tq, q_cols), lambda b, i: (b, i, 0)),
        out_shape=jax.ShapeDtypeStruct((bsz, Lc, q_cols), F32),
        compiler_params=_params("parallel", "parallel"),
        name="context_attention",
    )(p_ctx, p_ctx, _sink_rows(sink))


def _rwkv_feat_kernel(p_ref, hp_ref, hn_ref, mu_ref, w0_ref, w2_ref, a0_ref, a2_ref, g2_ref, kk_ref, ka_ref,
                      rk_ref, e_ref, rvk_o, g_o, bonus_o, lw_o, kb_o, *, rw_w):
    p = p_ref[0]
    p_prev, p_next = _neighbours(p, *_halo_rows(hp_ref, hn_ref))
    p = p + mu_ref[...] * (0.5 * (p_prev + p_next) - p)
    o1, o2, o3 = rw_w, 2 * rw_w, 3 * rw_w
    o4 = o3 + 2 * RW_DECAY_LORA
    o5 = o4 + 2 * RW_AAA_LORA
    r, k, v = p[:, :o1], p[:, o1:o2], p[:, o2:o3]
    wd, ad, gd = p[:, o3:o4], p[:, o4:o5], p[:, o5:]
    w_log = -_softplus(-(w0_ref[...] + _mm(jnp.tanh(wd), w2_ref[...]))) - 0.5
    lw = -jnp.exp(w_log)
    a = _sigmoid(a0_ref[...] + _mm(ad, a2_ref[...]))
    e = e_ref[...]
    kq = k * kk_ref[...]
    kkn = kq * lax.rsqrt(_mm_exact_rhs(kq * kq, e) + 1e-6)
    rvk_o[0] = jnp.concatenate([r, v, kkn], axis=1).astype(BF16)
    g_o[0] = _mm(_sigmoid(gd), g2_ref[...])
    ka = ka_ref[...]
    rk = r * rk_ref[...]
    acc = jnp.zeros_like(r)
    for d in range(2):
        a_d = a[:, d * rw_w:(d + 1) * rw_w]
        k_d = k * (1.0 + (a_d - 1.0) * ka)
        lw_o[d, 0] = lw[:, d * rw_w:(d + 1) * rw_w]
        kb_o[d, 0] = jnp.concatenate([k_d, kkn * a_d], axis=1).astype(BF16)
        acc = acc + rk * k_d
    bonus_o[0] = _mm_exact_rhs(acc, e) * v


def _head_indicator(w, head):
    hid = jnp.arange(w) // head
    return (hid[:, None] == hid[None, :]).astype(BF16)


def rwkv_features(p, mu, w0, w2, a0, a2, g2, k_k, k_a, r_k):
    bsz, T, cols = p.shape
    rw_w = g2.shape[1]
    tm = _row_tile(T, 256)
    w2_bd = jnp.zeros((2 * RW_DECAY_LORA, 2 * rw_w), F32)
    w2_bd = w2_bd.at[:RW_DECAY_LORA, :rw_w].set(w2[0]).at[RW_DECAY_LORA:, rw_w:].set(w2[1]).astype(BF16)
    a2_bd = jnp.zeros((2 * RW_AAA_LORA, 2 * rw_w), F32)
    a2_bd = a2_bd.at[:RW_AAA_LORA, :rw_w].set(a2[0]).at[RW_AAA_LORA:, rw_w:].set(a2[1]).astype(BF16)
    vec = lambda t: t.reshape(1, -1).astype(F32)
    full = lambda shape: pl.BlockSpec(shape, lambda b, i: (0,) * len(shape))
    tok = lambda n: pl.BlockSpec((1, tm, n * rw_w), lambda b, i: (b, i, 0))
    tok2 = lambda n: pl.BlockSpec((2, 1, tm, n * rw_w), lambda b, i: (0, b, i, 0))
    s1 = lambda n, t: jax.ShapeDtypeStruct((bsz, T, n * rw_w), t)
    s2 = lambda n, t: jax.ShapeDtypeStruct((2, bsz, T, n * rw_w), t)
    return pl.pallas_call(
        functools.partial(_rwkv_feat_kernel, rw_w=rw_w),
        grid=(bsz, T // tm),
        in_specs=_halo_specs(tm, T, cols) + [
            full((1, cols)), full((1, 2 * rw_w)), full(w2_bd.shape), full((1, 2 * rw_w)),
            full(a2_bd.shape), full(g2.shape), full((1, rw_w)), full((1, rw_w)), full((1, rw_w)),
            full((rw_w, rw_w))],
        out_specs=[tok(3), tok(1), tok(1), tok2(1), tok2(2)],
        out_shape=[s1(3, BF16), s1(1, F32), s1(1, F32), s2(1, F32), s2(2, BF16)],
        compiler_params=_params("parallel", "parallel"),
        name="rwkv_features",
    )(p, p, p, vec(mu), vec(w0), w2_bd, vec(a0), a2_bd, g2.astype(BF16), vec(k_k), vec(k_a), vec(r_k),
      _head_indicator(rw_w, RW_HEAD))


def _rwkv_scan_kernel(rvkc, lwc, kbc, rvkl, lwl, kbl, y_ref, s_ref, *, n_ctx, bb):
    d = pl.program_id(1)
    c = pl.program_id(2)
    C = CHUNK
    n_pairs = s_ref.shape[0] // bb

    @pl.when(c == 0)
    def _():
        s_ref[...] = jnp.zeros_like(s_ref)

    is_ctx = c < n_ctx
    pick = lambda a, b: jnp.where(is_ctx, a, b)
    rev = d == 1
    ti = lax.broadcasted_iota(jnp.int32, (C, C), 0)
    tj = lax.broadcasted_iota(jnp.int32, (C, C), 1)
    tri = jnp.where(jnp.where(rev, tj - ti, ti - tj) >= 0, 1.0, 0.0).astype(BF16)

    def features(bi):
        w = lwc.shape[-1]
        rvk = pick(rvkc[bi], rvkl[bi])
        r, v, kk = rvk[:, :w], rvk[:, w:2 * w], rvk[:, 2 * w:]
        lw = pick(lwc[0, bi], lwl[0, bi])
        kb = pick(kbc[0, bi], kbl[0, bi])
        kd, bd = kb[:, :w], kb[:, w:]
        linc = _mm_exact_lhs(tri, lw)
        ltot = jnp.sum(lw, axis=0, keepdims=True)
        g_inv = jnp.exp(-linc)
        g_end = jnp.exp(ltot - linc)
        return (kk * jnp.exp(linc - lw), r * jnp.exp(linc), kd * g_inv, bd * g_inv, kd * g_end, bd * g_end, v,
                jnp.exp(ltot))

    feats = [features(bi) for bi in range(bb)]

    n = 2 * C
    rid = lax.broadcasted_iota(jnp.int32, (n, n), 0)
    cid = lax.broadcasted_iota(jnp.int32, (n, n), 1)
    same = (rid >> 6) == (cid >> 6)
    diff = jnp.where(rev, cid - rid, rid - cid)
    strict = lambda m: jnp.where(same, jnp.where(diff > 0, m, 0.0), 0.0)
    incl = lambda m: jnp.where(same, jnp.where(diff >= 0, m, 0.0), 0.0)
    first = lax.broadcasted_iota(jnp.int32, (1, LANE), 1) < RW_HEAD

    def stack(x):
        return jnp.concatenate([jnp.where(first, x, 0.0), jnp.where(first, 0.0, x)], axis=0)

    prob = [(bi, slice(p * LANE, (p + 1) * LANE)) for bi in range(bb) for p in range(n_pairs)]
    qs = range(len(prob))
    stk = [[stack(t[:, sl]) for t in feats[bi][:7]] for bi, sl in prob]
    kks, rs, kis, bis, kes, bes, vs = ([stk[q][i] for q in qs] for i in range(7))
    m = [_mm_nt(jnp.concatenate([kks[q], rs[q]], axis=0), jnp.concatenate([kis[q], bis[q]], axis=0)) for q in qs]
    a_k = [strict(t[:n, :n]) for t in m]
    a_b = [strict(t[:n, n:]) for t in m]
    b_k = [incl(t[n:, :n]) for t in m]
    b_b = [incl(t[n:, n:]) for t in m]
    t_inv = _unit_tri_inverse(a_b, rid, cid, C)
    abv = [_mm(jnp.concatenate([a_k[q], b_k[q]], axis=0), vs[q]) for q in qs]
    wu = [_mm(t_inv[q], jnp.concatenate([kks[q], abv[q][:n]], axis=1)) for q in qs]
    w_s = [t[:, :LANE] for t in wu]
    u0_s = [t[:, LANE:] for t in wu]
    vt = [t.T for t in vs]
    u0_t = [t.T for t in u0_s]
    st = [s_ref[q] for q in qs]
    wr = [_mm_nt(jnp.concatenate([w_s[q], rs[q]], axis=0), st[q]) for q in qs]
    u_s = [wr[q][:n] + u0_s[q] for q in qs]
    u_t = [_mm_nt(st[q], w_s[q]) + u0_t[q] for q in qs]
    y_s = [wr[q][n:] + abv[q][n:] - _mm(b_b[q], u_s[q]) for q in qs]
    for q, (bi, sl) in enumerate(prob):
        s_ref[q] = st[q] * feats[bi][7][:, sl] + _mm(jnp.concatenate([vt[q], -u_t[q]], axis=1),
                                                    jnp.concatenate([kes[q], bes[q]], axis=0))
        y_ref[0, bi, :, sl] = y_s[q][:C] + y_s[q][C:]


def rwkv_scan(f_ctx, f_lat):
    rvkc, lwc, kbc = f_ctx
    rvkl, lwl, kbl = f_lat
    bsz, Lc, w = lwc.shape[1:]
    L = lwl.shape[2]
    C = CHUNK
    assert Lc % C == 0 and L % C == 0 and w % LANE == 0
    nc, nl = Lc // C, L // C
    bb = _row_tile(bsz, RW_SCAN_BATCH)
    ctx_idx, lat_idx = _scan_index_maps(nc, nl)
    shared = lambda idx: pl.BlockSpec((bb, C, 3 * w), lambda b, d, c: (b, idx(d, c), 0))
    directed = lambda idx, n: pl.BlockSpec((1, bb, C, n * w), lambda b, d, c: (d, b, idx(d, c), 0))
    return pl.pallas_call(
        functools.partial(_rwkv_scan_kernel, n_ctx=nc, bb=bb),
        grid=(bsz // bb, 2, nc + nl),
        in_specs=[shared(ctx_idx), directed(ctx_idx, 1), directed(ctx_idx, 2),
                  shared(lat_idx), directed(lat_idx, 1), directed(lat_idx, 2)],
        out_specs=directed(lat_idx, 1),
        out_shape=jax.ShapeDtypeStruct((2, bsz, L, w), F32),
        scratch_shapes=[pltpu.VMEM((bb * (w // LANE), LANE, LANE), F32)],
        compiler_params=_params("parallel", "parallel", "arbitrary"),
        name="rwkv_scan",
    )(rvkc, lwc, kbc, rvkl, lwl, kbl)


def _rwkv_head_out(y, bonus, g, gn_g, gn_b, e):
    inv_n = 1.0 / RW_HEAD
    mu = _mm_exact_rhs(y, e) * inv_n
    yc = y - mu
    var = _mm_exact_rhs(yc * yc, e) * inv_n
    return (yc * lax.rsqrt(var + RW_GN_EPS) * gn_g + gn_b + bonus) * g


def rwkv7_scan_lat(p_lat, p_ctx, mu, w0, w2, a0, a2, g2, k_k, k_a, r_k):
    args = (mu, w0, w2, a0, a2, g2, k_k, k_a, r_k)
    rvk_l, g_l, bonus_l, lw_l, kb_l = rwkv_features(p_lat, *args)
    rvk_c, _, _, lw_c, kb_c = rwkv_features(p_ctx, *args)
    return rwkv_scan((rvk_c, lw_c, kb_c), (rvk_l, lw_l, kb_l)), bonus_l, g_l


def _gdn_feat_kernel(p_ref, hp_ref, hn_ref, g_ref, cw_ref, al_ref, dt_ref, qkv_o, gb_o, *, dn_w):
    x = p_ref[0]
    x_prev, x_next = _neighbours(x, *_halo_rows(hp_ref, hn_ref))
    cw = cw_ref[...]
    z = x_prev * cw[0:1] + x * cw[1:2] + x_next * cw[2:3]
    z = z * _sigmoid(z)
    for h in range(dn_w // DN_HEAD):
        sq = slice(h * DN_HEAD, (h + 1) * DN_HEAD)
        sk = slice(dn_w + h * DN_HEAD, dn_w + (h + 1) * DN_HEAD)
        q = z[:, sq]
        k = z[:, sk]
        qn = q * (lax.rsqrt(jnp.sum(q * q, -1, keepdims=True) + 1e-6) * (DN_HEAD ** -0.5))
        qkv_o[0, :, sq] = qn.astype(BF16)
        qkv_o[0, :, sk] = (k * lax.rsqrt(jnp.sum(k * k, -1, keepdims=True) + 1e-6)).astype(BF16)
    qkv_o[0, :, 2 * dn_w:] = z[:, 2 * dn_w:].astype(BF16)
    g = g_ref[0]
    gb_o[0, :, :LANE] = -jnp.exp(al_ref[...]) * _softplus(g[:, :LANE] + dt_ref[...])
    gb_o[0, :, LANE:] = _sigmoid(g[:, LANE:])


def gdn_features(p, conv_w, A_log, dt_bias):
    bsz, T, cols = p.shape
    dn_w = conv_w.shape[1] // 3
    tm = _row_tile(T, 256)
    c3 = 3 * dn_w
    assert (4 * dn_w) % (2 * LANE) == 0
    gcol = (4 * dn_w) // (2 * LANE)
    pad = lambda t: jnp.pad(t.reshape(1, -1).astype(F32), ((0, 0), (0, LANE - t.size)))
    tok = lambda n: pl.BlockSpec((1, tm, n), lambda b, i: (b, i, 0))
    row = pl.BlockSpec((1, LANE), lambda b, i: (0, 0))
    return pl.pallas_call(
        functools.partial(_gdn_feat_kernel, dn_w=dn_w),
        grid=(bsz, T // tm),
        in_specs=_halo_specs(tm, T, c3) + [pl.BlockSpec((1, tm, 2 * LANE), lambda b, i: (b, i, gcol)),
                                          pl.BlockSpec((8, c3), lambda b, i: (0, 0)), row, row],
        out_specs=[tok(c3), tok(2 * LANE)],
        out_shape=[jax.ShapeDtypeStruct((bsz, T, c3), BF16), jax.ShapeDtypeStruct((bsz, T, 2 * LANE), F32)],
        compiler_params=_params("parallel", "parallel"),
        name="gdn_features",
    )(p, p, p, p, jnp.pad(conv_w, ((0, 8 - conv_w.shape[0]), (0, 0))), pad(A_log), pad(dt_bias))


def _gdn_scan_kernel(qkvc, gbc, qkvl, gbl, o_ref, s_ref, *, n_ctx, bb):
    d = pl.program_id(1)
    c = pl.program_id(2)
    C = CHUNK
    heads = s_ref.shape[0] // bb

    @pl.when(c == 0)
    def _():
        s_ref[...] = jnp.zeros_like(s_ref)

    is_ctx = c < n_ctx
    pick = lambda a, b, bi: jnp.where(is_ctx, a[bi], b[bi])
    rev = d == 1

    ti = lax.broadcasted_iota(jnp.int32, (C, C), 0)
    tj = lax.broadcasted_iota(jnp.int32, (C, C), 1)
    tri = jnp.where(jnp.where(rev, tj - ti, ti - tj) >= 0, 1.0, 0.0).astype(BF16)
    lane = lax.broadcasted_iota(jnp.int32, (1, LANE), 1)

    def col(x, h):
        return jnp.sum(jnp.where(lane == d * heads + h, x, 0.0), axis=1, keepdims=True)

    n = 2 * C
    rid = lax.broadcasted_iota(jnp.int32, (n, n), 0)
    cid = lax.broadcasted_iota(jnp.int32, (n, n), 1)
    same = (rid >> 6) == (cid >> 6)
    diff = jnp.where(rev, cid - rid, rid - cid)
    keep_incl = jnp.where(same, jnp.where(diff >= 0, 1.0, 0.0), 0.0)
    eye = jnp.where(rid == cid, 1.0, 0.0)

    hs = lambda h: slice(h * DN_HEAD, (h + 1) * DN_HEAD)

    prob = [(bi, p) for bi in range(bb) for p in range(heads // 2)]
    qs = range(len(prob))
    k_s, q_s, v_s, g_col, b_col, ge_col, g_end = [], [], [], [], [], [], []
    for bi in range(bb):
        w = heads * DN_HEAD
        qkv = pick(qkvc, qkvl, bi)
        q, k, v = qkv[:, :w], qkv[:, w:2 * w], qkv[:, 2 * w:]
        gb = pick(gbc, gbl, bi)
        gl, be = gb[:, :LANE], gb[:, LANE:]
        g_cum = _mm_exact_lhs(tri, gl)
        g_tot = jnp.sum(gl, axis=0, keepdims=True)
        for p in range(heads // 2):
            h0, h1 = 2 * p, 2 * p + 1
            stack = lambda x: jnp.concatenate([x[:, hs(h0)], x[:, hs(h1)]], axis=0)
            k_s.append(stack(k))
            q_s.append(stack(q))
            v_s.append(stack(v))
            g_col.append(jnp.concatenate([col(g_cum, h0), col(g_cum, h1)], axis=0))
            b_col.append(jnp.concatenate([col(be, h0), col(be, h1)], axis=0))
            gt = [col(g_tot, h0), col(g_tot, h1)]
            ge_col.append(jnp.concatenate([jnp.broadcast_to(t, (C, 1)) for t in gt], axis=0))
            g_end.append([jnp.exp(t) for t in gt])
    g_row = [jnp.broadcast_to(g_col[i], (n, n)).T for i in qs]
    decay = [keep_incl * jnp.exp(keep_incl * (g_col[i] - g_row[i])) for i in qs]
    kq = [_mm_nt(jnp.concatenate([k_s[i], q_s[i]], axis=0), k_s[i]) for i in qs]
    a = [(1.0 - eye) * b_col[i] * kq[i][:n] * decay[i] for i in qs]
    qk = [kq[i][n:] * decay[i] for i in qs]
    t_inv = _unit_tri_inverse(a, rid, cid, C)
    eg = [jnp.exp(g_col[i]) for i in qs]
    uw = [_mm(t_inv[i], jnp.concatenate([b_col[i] * v_s[i], (b_col[i] * eg[i]) * k_s[i]], axis=1)) for i in qs]
    u0 = [t[:, :DN_HEAD] for t in uw]
    w = [t[:, DN_HEAD:] for t in uw]
    q_dec = [q_s[i] * eg[i] for i in qs]
    k_dec_t = [_mm_nt(eye, k_s[i] * jnp.exp(ge_col[i] - g_col[i])) for i in qs]
    head_cols = lax.broadcasted_iota(jnp.int32, (1, n), 1) >> 6

    st = [[s_ref[2 * i + t] for t in range(2)] for i in qs]
    ws = [[_mm(jnp.concatenate([w[i][t * C:(t + 1) * C], q_dec[i][t * C:(t + 1) * C]], axis=0), st[i][t])
           for t in range(2)] for i in qs]
    u = [jnp.concatenate([u0[i][:C] - ws[i][0][:C], u0[i][C:] - ws[i][1][:C]], axis=0) for i in qs]
    o = [jnp.concatenate([ws[i][0][C:], ws[i][1][C:]], axis=0) + _mm(qk[i], u[i]) for i in qs]
    for i, (bi, p) in enumerate(prob):
        for t in range(2):
            kd = jnp.where(head_cols == t, k_dec_t[i], 0.0)
            s_ref[2 * i + t] = g_end[i][t] * st[i][t] + _mm(kd, u[i])
            o_ref[0, bi, :, hs(2 * p + t)] = o[i][t * C:(t + 1) * C]


def gdn_scan(f_ctx, f_lat):
    qkvc, gbc = f_ctx
    qkvl, gbl = f_lat
    bsz, Lc, w3 = qkvc.shape
    L = qkvl.shape[1]
    C = CHUNK
    w = w3 // 3
    heads = w // DN_HEAD
    assert Lc % C == 0 and L % C == 0 and heads % 2 == 0
    nc, nl = Lc // C, L // C
    bb = _row_tile(bsz, DN_SCAN_BATCH)
    ctx_idx, lat_idx = _scan_index_maps(nc, nl)
    spec = lambda idx, n: pl.BlockSpec((bb, C, n), lambda b, d, c: (b, idx(d, c), 0))
    return pl.pallas_call(
        functools.partial(_gdn_scan_kernel, n_ctx=nc, bb=bb),
        grid=(bsz // bb, 2, nc + nl),
        in_specs=[spec(ctx_idx, w3), spec(ctx_idx, 2 * LANE), spec(lat_idx, w3), spec(lat_idx, 2 * LANE)],
        out_specs=pl.BlockSpec((1, bb, C, w), lambda b, d, c: (d, b, lat_idx(d, c), 0)),
        out_shape=jax.ShapeDtypeStruct((2, bsz, L, w), F32),
        scratch_shapes=[pltpu.VMEM((bb * heads, DN_HEAD, DN_HEAD), F32)],
        compiler_params=_params("parallel", "parallel", "arbitrary"),
        name="gdn_scan",
    )(qkvc, gbc, qkvl, gbl)


def _gdn_head_out(o, z, norm_g):
    outs = []
    for h in range(o.shape[1] // DN_HEAD):
        sl = slice(h * DN_HEAD, (h + 1) * DN_HEAD)
        oh = o[:, sl]
        zh = z[:, sl]
        outs.append(oh * lax.rsqrt(jnp.mean(oh * oh, -1, keepdims=True) + 1e-6) * norm_g * (zh * _sigmoid(zh)))
    return jnp.concatenate(outs, axis=1)


def gdn_scan_lat(p_lat, p_ctx, conv_w, A_log, dt_bias):
    f_lat = gdn_features(p_lat, conv_w, A_log, dt_bias)
    f_ctx = gdn_features(p_ctx, conv_w, A_log, dt_bias)
    return gdn_scan(f_ctx, f_lat)


def _odd_out_ln_kernel(y_ref, bonus_ref, g_ref, gg_ref, gb_ref, e_ref, o_ref, z_ref, ng_ref, wa_ref, wb_ref,
                       x_ref, gate_ref, lg_ref, lb_ref, out_ref):
    y_c = _rwkv_head_out(y_ref[0, 0] + y_ref[1, 0], bonus_ref[0], g_ref[0], gg_ref[...], gb_ref[...], e_ref[...])
    y_d = _gdn_head_out(o_ref[0, 0] + o_ref[1, 0], z_ref[0], ng_ref[...])
    y = _mm(y_c, wa_ref[...]) + _mm(y_d, wb_ref[...])
    z = DEEPNORM_ALPHA * x_ref[0] + gate_ref[0] * y
    out_ref[0] = _layer_norm_rows(z, lg_ref[...], lb_ref[...])


def odd_out_proj_ln(y, bonus, g, gn_g, gn_b, o, p_dn, norm_g, w, x, gate, ln_g, ln_b):
    bsz, L, d = x.shape
    rw_w, dn_w = y.shape[-1], o.shape[-1]
    tm = _row_tile(L, 256)
    z_block = 3
    two = lambda w_: pl.BlockSpec((2, 1, tm, w_), lambda b, i: (0, b, i, 0))
    tok = lambda w_: pl.BlockSpec((1, tm, w_), lambda b, i: (b, i, 0))
    row = lambda w_: pl.BlockSpec((1, w_), lambda b, i: (0, 0))
    full = lambda a: pl.BlockSpec(a.shape, lambda b, i: (0, 0))
    wa, wb = w[:rw_w], w[rw_w:]
    e = _head_indicator(rw_w, RW_HEAD)
    return pl.pallas_call(
        _odd_out_ln_kernel,
        grid=(bsz, L // tm),
        in_specs=[two(rw_w), tok(rw_w), tok(rw_w), row(rw_w), row(rw_w), full(e), two(dn_w),
                  pl.BlockSpec((1, tm, dn_w), lambda b, i: (b, i, z_block)), row(DN_HEAD), full(wa), full(wb),
                  tok(d), pl.BlockSpec((1, 1, d), lambda b, i: (b, 0, 0)), row(d), row(d)],
        out_specs=tok(d),
        out_shape=jax.ShapeDtypeStruct((bsz, L, d), F32),
        compiler_params=_params("parallel", "parallel"),
        name="odd_out_proj_ln",
    )(y, bonus, g, gn_g.reshape(1, rw_w), gn_b.reshape(1, rw_w), e, o, p_dn, norm_g.reshape(1, DN_HEAD), wa, wb,
      x, gate, ln_g.reshape(1, d), ln_b.reshape(1, d))


def _gdn_in_weight(w_dn, dn_w, heads):
    padw = ((0, 0), (0, LANE - 2 * heads))
    g = w_dn[:, 4 * dn_w:]
    return jnp.concatenate([w_dn[:, :4 * dn_w], jnp.pad(g[:, :2 * heads], padw), jnp.pad(g[:, 2 * heads:], padw)], 1)


def kernel(x, c, ctx, c_ctx, mod_w, mod_b, ln_g, ln_b, mlp_w1, mlp_w2, e_w_in, e_w_out, hy_conv, hy_ffn_w1,
           hy_ffn_b1, hy_ffn_w2, hy_ffn_b2, hy_sin_freq, hy_ffn_w3, hy_decay, hy_bias, attn_sink, o_w_in,
           o_w_out, rw_mu, rw_w0, rw_w2, rw_a0, rw_a2, rw_g2, rw_kk, rw_ka, rw_rk, rw_lnx_g, rw_lnx_b,
           dn_conv, dn_A_log, dn_dt_bias, dn_norm_g):
    bsz, L, d = x.shape
    assert mod_w.shape[0] == DEPTH == 2
    cc = jnp.concatenate([c, c_ctx[None, :]], axis=0)
    cc = jnp.pad(cc, ((0, (-cc.shape[0]) % 8), (0, 0)))

    def modulation(i):
        m = mod_vectors(cc, mod_w[i], mod_b[i])
        m_lat = m[:bsz].reshape(bsz, 1, N_MOD, d)
        m_ctx = jnp.broadcast_to(m[bsz].reshape(1, 1, N_MOD, d), (bsz, 1, N_MOD, d))
        return [m_lat[:, :, t] for t in range(N_MOD)], [m_ctx[:, :, t] for t in range(N_MOD)]

    (sh1, sc1, g1, sh2, sc2, g2), (csh1, csc1, cg1, csh2, csc2, cg2) = modulation(0)
    hy_cols = hy_conv.shape[2]
    w_in = e_w_in[0].astype(BF16)
    ws = [w_in[:, :hy_cols], w_in[:, hy_cols:]]
    w_out = e_w_out[0].astype(BF16)
    w1, w2 = mlp_w1[0].astype(BF16), mlp_w2[0].astype(BF16)
    hy = (hy_conv[0], hy_ffn_w1[0], hy_ffn_b1[0], hy_ffn_w2[0], hy_ffn_b2[0], hy_sin_freq[0], hy_ffn_w3[0],
          hy_decay[0], hy_bias[0])
    p_hy, p_qkv = mod_linear(x, sc1, sh1, ws)
    pc_hy, pc_qkv = mod_linear(ctx, csc1, csh1, ws)
    y_a = hyena_mixer(p_hy, *hy)
    y_b = windowed_attention(p_qkv, pc_qkv, attn_sink[0])
    yc_a = hyena_mixer(pc_hy, *hy)
    yc_b = context_attention(pc_qkv, attn_sink[0])
    x = out_proj_ln(y_a, y_b, w_out, x, g1, ln_g[0, 0], ln_b[0, 0])
    x = mlp_ln(x, sc2, sh2, g2, w1, w2, ln_g[0, 1], ln_b[0, 1])
    ctx = out_proj_ln(yc_a, yc_b, w_out, ctx, cg1, ln_g[0, 0], ln_b[0, 0])
    ctx = mlp_ln(ctx, csc2, csh2, cg2, w1, w2, ln_g[0, 1], ln_b[0, 1])

    (sh1, sc1, g1, sh2, sc2, g2), (csh1, csc1, _, _, _, _) = modulation(1)
    rw_cols = rw_mu.shape[1]
    dn_w = dn_conv.shape[2] // 3
    w_in = o_w_in[0]
    ws = [w_in[:, :rw_cols].astype(BF16), _gdn_in_weight(w_in[:, rw_cols:], dn_w, dn_w // DN_HEAD).astype(BF16)]
    w_out = o_w_out[0].astype(BF16)
    w1, w2 = mlp_w1[1].astype(BF16), mlp_w2[1].astype(BF16)
    p_rw, p_dn = mod_linear(x, sc1, sh1, ws)
    pc_rw, pc_dn = mod_linear(ctx, csc1, csh1, ws)
    y_rw, bonus, g_rw = rwkv7_scan_lat(p_rw, pc_rw, rw_mu[0], rw_w0[0], rw_w2[0], rw_a0[0], rw_a2[0], rw_g2[0],
                                       rw_kk[0], rw_ka[0], rw_rk[0])
    o_dn = gdn_scan_lat(p_dn, pc_dn, dn_conv[0], dn_A_log[0], dn_dt_bias[0])
    x = odd_out_proj_ln(y_rw, bonus, g_rw, rw_lnx_g[0], rw_lnx_b[0], o_dn, p_dn, dn_norm_g[0], w_out, x, g1,
                        ln_g[1, 0], ln_b[1, 0])
    x = mlp_ln(x, sc2, sh2, g2, w1, w2, ln_g[1, 1], ln_b[1, 1])
    return x
```

```python
import functools
import math

import jax
import jax.numpy as jnp
from jax import lax
from jax.experimental import pallas as pl
from jax.experimental.pallas import tpu as pltpu

F32 = jnp.float32
BF16 = jnp.bfloat16

DEPTH = 2
N_MOD = 6
DEEPNORM_ALPHA = (2.0 * DEPTH) ** 0.25
LN_EPS = 1e-5

HY_BANDS = 16
HEAD_DIM = 64
SWA_KV_HEADS = 2
WINDOW = 128
SWA_BLOCK = 128
SWA_QUERY_TILE = 128
GRID_W = 64
ROPE_BASE = 10000.0

RW_HEAD = 64
RW_DECAY_LORA = 64
RW_AAA_LORA = 64
RW_GN_EPS = 64e-5

DN_HEAD = 128
CHUNK = 64
RW_SCAN_BATCH = 4
DN_SCAN_BATCH = 8

LANE = 128
MXU_WIDTH = 256
VMEM_LIMIT = 56 * 1024 * 1024


def _row_tile(n, cap):
    t = min(n, cap)
    while n % t:
        t //= 2
    return t


def _params(*sem):
    return pltpu.CompilerParams(dimension_semantics=sem, vmem_limit_bytes=VMEM_LIMIT)


def _mm(a, b):
    return jnp.dot(a.astype(BF16), b.astype(BF16), preferred_element_type=F32)


def _mm_nt(a, b):
    return lax.dot_general(a.astype(BF16), b.astype(BF16), (((1,), (1,)), ((), ())), preferred_element_type=F32)


def _split2(x):
    hi = x.astype(BF16)
    return hi, (x - hi.astype(F32)).astype(BF16)


def _split3(x):
    hi = x.astype(BF16)
    r = x - hi.astype(F32)
    mid = r.astype(BF16)
    return hi, mid, (r - mid.astype(F32)).astype(BF16)


def _mm_exact_lhs(a_bf16, x):
    return sum(jnp.dot(a_bf16, t, preferred_element_type=F32) for t in _split3(x))


def _head_sum(x, e_bf16):
    return sum(jnp.dot(t, e_bf16, preferred_element_type=F32) for t in _split2(x))


def _mm_f32(a, b):
    ah, al = _split2(a)
    bh, bl = _split2(b)
    dot = lambda x, y: jnp.dot(x, y, preferred_element_type=F32)
    return dot(ah, bh) + dot(ah, bl) + dot(al, bh)


def _sigmoid(x):
    return 1.0 / (1.0 + jnp.exp(-x))


def _softplus(x):
    return jnp.maximum(x, 0.0) + jnp.log(1.0 + jnp.exp(-jnp.abs(x)))


def _neighbours(x, prev_row, next_row):
    n = x.shape[0]
    rows = lax.broadcasted_iota(jnp.int32, (n, 1), 0)
    x_prev = jnp.where(rows == 0, prev_row, pltpu.roll(x, 1, axis=0))
    x_next = jnp.where(rows == n - 1, next_row, pltpu.roll(x, n - 1, axis=0))
    return x_prev, x_next


def _halo_rows(hp_ref, hn_ref):
    i = pl.program_id(1)
    nt = pl.num_programs(1)
    return jnp.where(i > 0, hp_ref[0][7:8, :], 0.0), jnp.where(i < nt - 1, hn_ref[0][0:1, :], 0.0)


def _halo_specs(tm, T, cols, col_block=0):
    hb = tm // 8
    return [pl.BlockSpec((1, tm, cols), lambda b, i: (b, i, col_block)),
            pl.BlockSpec((1, 8, cols), lambda b, i: (b, jnp.maximum(i * hb - 1, 0), col_block)),
            pl.BlockSpec((1, 8, cols), lambda b, i: (b, jnp.minimum((i + 1) * hb, T // 8 - 1), col_block))]


def _unit_tri_inverse(a_list, rid, cid, blk):
    eye = jnp.where(rid == cid, 1.0, 0.0)
    pair = (rid >> 1) == (cid >> 1)
    d_list = [eye - jnp.where(pair, a, 0.0) for a in a_list]
    s, sh = 2, 1
    while s < blk:
        join = jnp.where((rid >> (sh + 1)) == (cid >> (sh + 1)), jnp.where((rid >> sh) != (cid >> sh), 1.0, 0.0), 0.0)
        dq = [_mm(d, a * join) for d, a in zip(d_list, a_list)]
        d_list = [d - _mm(t, d) for t, d in zip(dq, d_list)]
        s, sh = 2 * s, sh + 1
    return d_list


def _scan_index_maps(nc, nl):
    def ctx_idx(d, c):
        cc = jnp.minimum(c, nc - 1)
        return jnp.where(d == 1, nc - 1 - cc, cc)

    def lat_idx(d, c):
        cl = jnp.maximum(c - nc, 0)
        return jnp.where(d == 1, nl - 1 - cl, cl)

    return ctx_idx, lat_idx


def _mod_kernel(c_ref, w_ref, b_ref, o_ref):
    c = c_ref[...]
    o_ref[...] = _mm(c * _sigmoid(c), w_ref[...]) + b_ref[...]


def mod_vectors(cc, w, b):
    r, d = cc.shape
    n = w.shape[1]
    tn = _row_tile(n, 1024)
    return pl.pallas_call(
        _mod_kernel,
        grid=(n // tn,),
        in_specs=[pl.BlockSpec((r, d), lambda j: (0, 0)),
                  pl.BlockSpec((d, tn), lambda j: (0, j)),
                  pl.BlockSpec((1, tn), lambda j: (0, j))],
        out_specs=pl.BlockSpec((r, tn), lambda j: (0, j)),
        out_shape=jax.ShapeDtypeStruct((r, n), F32),
        name="mod_vectors",
    )(cc, w, b.reshape(1, n))


def _mod_linear_kernel(x_ref, sc_ref, sh_ref, *refs):
    n = len(refs) // 2
    h = (x_ref[0] * (1.0 + sc_ref[0]) + sh_ref[0]).astype(BF16)
    for w_ref, o_ref in zip(refs[:n], refs[n:]):
        o_ref[0] = jnp.dot(h, w_ref[...], preferred_element_type=F32)


def mod_linear(x, sc, sh, ws):
    bsz, L, d = x.shape
    tm = _row_tile(L, 512)
    vec = pl.BlockSpec((1, 1, d), lambda b, i: (b, 0, 0))
    return pl.pallas_call(
        _mod_linear_kernel,
        grid=(bsz, L // tm),
        in_specs=[pl.BlockSpec((1, tm, d), lambda b, i: (b, i, 0)), vec, vec]
        + [pl.BlockSpec(w.shape, lambda b, i: (0, 0)) for w in ws],
        out_specs=[pl.BlockSpec((1, tm, w.shape[1]), lambda b, i: (b, i, 0)) for w in ws],
        out_shape=[jax.ShapeDtypeStruct((bsz, L, w.shape[1]), F32) for w in ws],
        compiler_params=_params("parallel", "parallel"),
        name="mod_linear",
    )(x, sc, sh, *ws)


def _layer_norm_rows(z, g, b):
    mu = jnp.mean(z, -1, keepdims=True)
    zc = z - mu
    var = jnp.mean(zc * zc, -1, keepdims=True)
    return zc * lax.rsqrt(var + LN_EPS) * g + b


def _out_ln_kernel(ya_ref, yb_ref, wa_ref, wb_ref, x_ref, g_ref, lg_ref, lb_ref, o_ref):
    y = _mm(ya_ref[0], wa_ref[...]) + _mm(yb_ref[0], wb_ref[...])
    z = DEEPNORM_ALPHA * x_ref[0] + g_ref[0] * y
    o_ref[0] = _layer_norm_rows(z, lg_ref[...], lb_ref[...])


def out_proj_ln(ya, yb, w, x, gate, ln_g, ln_b):
    bsz, L, d = x.shape
    ka, kb = ya.shape[-1], yb.shape[-1]
    tm = _row_tile(L, 512)
    row = pl.BlockSpec((1, d), lambda b, i: (0, 0))
    return pl.pallas_call(
        _out_ln_kernel,
        grid=(bsz, L // tm),
        in_specs=[pl.BlockSpec((1, tm, ka), lambda b, i: (b, i, 0)),
                  pl.BlockSpec((1, tm, kb), lambda b, i: (b, i, 0)),
                  pl.BlockSpec((ka, d), lambda b, i: (0, 0)),
                  pl.BlockSpec((kb, d), lambda b, i: (0, 0)),
                  pl.BlockSpec((1, tm, d), lambda b, i: (b, i, 0)),
                  pl.BlockSpec((1, 1, d), lambda b, i: (b, 0, 0)), row, row],
        out_specs=pl.BlockSpec((1, tm, d), lambda b, i: (b, i, 0)),
        out_shape=jax.ShapeDtypeStruct((bsz, L, d), F32),
        compiler_params=_params("parallel", "parallel"),
        name="out_proj_ln",
    )(ya, yb, w[:ka], w[ka:], x, gate, ln_g.reshape(1, d), ln_b.reshape(1, d))


def _mlp_ln_kernel(x_ref, sc_ref, sh_ref, g_ref, w1_ref, w2_ref, lg_ref, lb_ref, o_ref, *, ff_tile):
    x = x_ref[0]
    h = (x * (1.0 + sc_ref[0]) + sh_ref[0]).astype(BF16)
    d_ff = w1_ref.shape[1]
    acc = jnp.zeros(x.shape, F32)
    for c in range(d_ff // ff_tile):
        a = jnp.dot(h, w1_ref[:, c * ff_tile:(c + 1) * ff_tile], preferred_element_type=F32)
        a = jnp.maximum(a, 0.0)
        a = (a * a).astype(BF16)
        acc = acc + jnp.dot(a, w2_ref[c * ff_tile:(c + 1) * ff_tile, :], preferred_element_type=F32)
    z = DEEPNORM_ALPHA * x + g_ref[0] * acc
    o_ref[0] = _layer_norm_rows(z, lg_ref[...], lb_ref[...])


def mlp_ln(x, sc, sh, gate, w1, w2, ln_g, ln_b):
    bsz, L, d = x.shape
    d_ff = w1.shape[1]
    tm = _row_tile(L, 512)
    ff_tile = _row_tile(d_ff, 1024)
    once = dict(pipeline_mode=pl.Buffered(1))
    vec = pl.BlockSpec((1, 1, d), lambda b, i: (b, 0, 0))
    row = pl.BlockSpec((1, d), lambda b, i: (0, 0))
    return pl.pallas_call(
        functools.partial(_mlp_ln_kernel, ff_tile=ff_tile),
        grid=(bsz, L // tm),
        in_specs=[pl.BlockSpec((1, tm, d), lambda b, i: (b, i, 0)), vec, vec, vec,
                  pl.BlockSpec((d, d_ff), lambda b, i: (0, 0), **once),
                  pl.BlockSpec((d_ff, d), lambda b, i: (0, 0), **once), row, row],
        out_specs=pl.BlockSpec((1, tm, d), lambda b, i: (b, i, 0)),
        out_shape=jax.ShapeDtypeStruct((bsz, L, d), F32),
        compiler_params=_params("parallel", "parallel"),
        name="mlp_ln",
    )(x, sc, sh, gate, w1, w2, ln_g.reshape(1, d), ln_b.reshape(1, d))


def dft_tables(T):
    n = 2 * T
    f = jnp.arange(T, dtype=jnp.int32)
    ang = ((f[:, None] * f[None, :]) % n).astype(F32) * (2.0 * math.pi / n)
    return jnp.cos(ang).astype(BF16), (-jnp.sin(ang)).astype(BF16)


def hyena_pos_features(T):
    t = jnp.arange(T, dtype=F32)
    t_norm = t / max(T - 1, 1)
    bands = jnp.linspace(1e-4, HY_BANDS - 1, HY_BANDS, dtype=F32)
    ang = 2.0 * math.pi * t[:, None] * bands[None, :] / T
    pe = jnp.concatenate([t_norm[:, None], jnp.cos(ang), -jnp.sin(ang)], axis=-1)
    return jnp.pad(pe, ((0, 0), (0, (-pe.shape[1]) % 8)))


def _hy_filter_kernel(pe_ref, w1_ref, b1_ref, w2_ref, b2_ref, fr_ref, w3f_ref, w3b_ref, df_ref, db_ref,
                      cm_ref, sm_ref, kr_o, ki_o, kn_o):
    T = pe_ref.shape[0]
    pe = pe_ref[...]
    fr = fr_ref[...]
    h = jnp.sin(fr * (_mm_f32(pe, w1_ref[...]) + b1_ref[...]))
    h = jnp.sin(fr * (_mm_f32(h, w2_ref[...]) + b2_ref[...]))
    t_norm = pe[:, 0:1]
    h_f = _mm_f32(h, w3f_ref[...]) * jnp.exp(-t_norm * jnp.abs(df_ref[...]))
    h_b = _mm_f32(h, w3b_ref[...]) * jnp.exp(-t_norm * jnp.abs(db_ref[...]))
    row = lax.broadcasted_iota(jnp.int32, (T, 1), 0)
    h_b = jnp.where(row == 0, 0.0, h_b)
    scale = jnp.where(row == 0, 1.0, 2.0) * (1.0 / (2 * T))
    kr_o[...] = _mm(cm_ref[...], h_f + h_b) * scale
    ki_o[...] = _mm(sm_ref[...], h_f - h_b) * scale
    sign = jnp.where((row & 1) == 0, 1.0, -1.0)
    kn = jnp.sum(sign * (h_f + h_b), axis=0, keepdims=True) * (1.0 / (2 * T))
    kn_o[...] = jnp.broadcast_to(kn, kn_o.shape)


def hyena_spectrum(T, cm, sm, w1, b1, w2, b2, freq, w3, decay):
    hy_w = w3.shape[1] // 2
    ct = MXU_WIDTH
    nj = hy_w // ct
    pe = hyena_pos_features(T)
    w1p = jnp.pad(w1, ((0, pe.shape[1] - w1.shape[0]), (0, 0)))
    row = lambda t: t.reshape(1, -1)
    full = lambda a: pl.BlockSpec(a.shape, lambda j: (0,) * a.ndim)
    ins = (pe, w1p, row(b1), w2, row(b2), row(freq))
    return pl.pallas_call(
        _hy_filter_kernel,
        grid=(nj,),
        in_specs=[full(a) for a in ins] + [
            pl.BlockSpec((w3.shape[0], ct), lambda j: (0, j)), pl.BlockSpec((w3.shape[0], ct), lambda j: (0, nj + j)),
            pl.BlockSpec((1, ct), lambda j: (0, j)), pl.BlockSpec((1, ct), lambda j: (0, nj + j)),
            pl.BlockSpec((T, T), lambda j: (0, 0)), pl.BlockSpec((T, T), lambda j: (0, 0))],
        out_specs=[pl.BlockSpec((T, ct), lambda j: (0, j)), pl.BlockSpec((T, ct), lambda j: (0, j)),
                   pl.BlockSpec((8, ct), lambda j: (0, j))],
        out_shape=[jax.ShapeDtypeStruct((T, hy_w), F32), jax.ShapeDtypeStruct((T, hy_w), F32),
                   jax.ShapeDtypeStruct((8, hy_w), F32)],
        compiler_params=_params("arbitrary"),
        name="hyena_spectrum",
    )(*ins, w3, w3, row(decay), row(decay), cm, sm)


def _hy_conv_kernel(x0_ref, x1_ref, xv_ref, c0_ref, c1_ref, cv_ref, bias_ref, kr_ref, ki_ref, kn_ref,
                    cm_ref, sm_ref, o_ref):
    T = x0_ref.shape[1]
    row = lax.broadcasted_iota(jnp.int32, (T, 1), 0)
    zero = jnp.zeros((1, x0_ref.shape[2]), F32)

    def short_conv(x_ref, c_ref):
        x = x_ref[0]
        c = c_ref[...]
        x_prev, x_next = _neighbours(x, zero, zero)
        return x_prev * c[0:1] + x * c[1:2] + x_next * c[2:3]

    u = short_conv(x1_ref, c1_ref) * short_conv(xv_ref, cv_ref)
    ub = u.astype(BF16)
    sign = jnp.where((row & 1) == 0, 1.0, -1.0)
    un = jnp.sum(sign * u, axis=0, keepdims=True)
    y = sign * (un * kn_ref[0:1, :]) + u * bias_ref[...]
    ft = min(T, 512)
    yr, yi = [], []
    for f0 in range(0, T, ft):
        fs = slice(f0, f0 + ft)
        ur = jnp.dot(cm_ref[fs, :], ub, preferred_element_type=F32)
        ui = jnp.dot(sm_ref[fs, :], ub, preferred_element_type=F32)
        kr = kr_ref[fs, :]
        ki = ki_ref[fs, :]
        yr.append((ur * kr - ui * ki).astype(BF16))
        yi.append((ur * ki + ui * kr).astype(BF16))
    y = y + jnp.dot(cm_ref[...], jnp.concatenate(yr, axis=0), preferred_element_type=F32)
    y = y + jnp.dot(sm_ref[...], jnp.concatenate(yi, axis=0), preferred_element_type=F32)
    o_ref[0] = short_conv(x0_ref, c0_ref) * y


def hyena_mixer(p, conv_w, w1, b1, w2, b2, freq, w3, decay, bias):
    bsz, T, _ = p.shape
    hy_w = bias.shape[0]
    ct = MXU_WIDTH
    nj = hy_w // ct
    cm, sm = dft_tables(T)
    kr, ki, kn = hyena_spectrum(T, cm, sm, w1, b1, w2, b2, freq, w3, decay)
    cw = jnp.pad(conv_w, ((0, 8 - conv_w.shape[0]), (0, 0)))
    once = dict(pipeline_mode=pl.Buffered(1))
    xs = lambda part: pl.BlockSpec((1, T, ct), lambda j, b: (b, 0, part * nj + j))
    cs = lambda part: pl.BlockSpec((8, ct), lambda j, b: (0, part * nj + j))
    return pl.pallas_call(
        _hy_conv_kernel,
        grid=(nj, bsz),
        in_specs=[xs(0), xs(1), xs(2), cs(0), cs(1), cs(2),
                  pl.BlockSpec((1, ct), lambda j, b: (0, j)),
                  pl.BlockSpec((T, ct), lambda j, b: (0, j), **once),
                  pl.BlockSpec((T, ct), lambda j, b: (0, j), **once),
                  pl.BlockSpec((8, ct), lambda j, b: (0, j)),
                  pl.BlockSpec((T, T), lambda j, b: (0, 0), **once),
                  pl.BlockSpec((T, T), lambda j, b: (0, 0), **once)],
        out_specs=pl.BlockSpec((1, T, ct), lambda j, b: (b, 0, j)),
        out_shape=jax.ShapeDtypeStruct((bsz, T, hy_w), F32),
        compiler_params=_params("parallel", "parallel"),
        name="hyena_conv",
    )(p, p, p, cw, cw, cw, bias.reshape(1, hy_w), kr, ki, kn, cm, sm)


def rope_tables(L):
    quarter = HEAD_DIM // 4
    pos = jnp.arange(L, dtype=jnp.int32)
    inv_freq = ROPE_BASE ** (-jnp.arange(quarter, dtype=F32) / quarter)
    a_row = (pos // GRID_W).astype(F32)[:, None] * inv_freq[None, :]
    a_col = (pos % GRID_W).astype(F32)[:, None] * inv_freq[None, :]
    ang = jnp.concatenate([a_row, a_row, a_col, a_col], axis=-1)
    sgn = jnp.tile(jnp.concatenate([-jnp.ones(quarter, F32), jnp.ones(quarter, F32)]), 2)
    return jnp.tile(jnp.cos(ang), (1, 2)), jnp.tile(jnp.sin(ang) * sgn, (1, 2))


def _rope(t, cos, sin, first_quarter):
    w = t.shape[1]
    partner = jnp.where(first_quarter, pltpu.roll(t, w - 16, axis=1), pltpu.roll(t, 16, axis=1))
    return t * cos + partner * sin


def _attn_kernel(*refs, has_local, L, span, group):
    if has_local:
        q_ref, kv_ref, kvc_ref, cos_ref, sin_ref, sink_ref, o_ref = refs
    else:
        q_ref, kvc_ref, sink_ref, o_ref = refs
    i = pl.program_id(1)
    tq = q_ref.shape[1]
    lane = lax.broadcasted_iota(jnp.int32, (1, LANE), 1)
    low = lane < HEAD_DIM
    q = q_ref[0]
    n_heads = q.shape[1] // HEAD_DIM
    kvc = kvc_ref[0]
    kc, vc = kvc[:, :LANE], kvc[:, LANE:]
    if has_local:
        fq = (lane & 31) < 16
        q0 = pl.multiple_of(i * tq, tq)
        cos_q = cos_ref[pl.ds(q0, tq), :]
        sin_q = sin_ref[pl.ds(q0, tq), :]
        k0 = pl.multiple_of(jnp.clip(i * tq - WINDOW, 0, L - span), SWA_BLOCK)
        kv = kv_ref[0, pl.ds(k0, span), :]
        k = _rope(kv[:, :LANE], cos_ref[pl.ds(k0, span), :], sin_ref[pl.ds(k0, span), :], fq)
        v = kv[:, LANE:]
        qpos = q0 + lax.broadcasted_iota(jnp.int32, (tq, span), 0)
        kpos = k0 + lax.broadcasted_iota(jnp.int32, (tq, span), 1)
        ok1 = jnp.abs(qpos - kpos) <= WINDOW
        ok = jnp.concatenate([ok1, ok1], axis=0)
    scale = HEAD_DIM ** -0.5

    def dup(x, g):
        sw = pltpu.roll(x, HEAD_DIM, axis=1)
        return jnp.where(low, x, sw) if g == 0 else jnp.where(low, sw, x)

    def with_ones(x, g):
        return jnp.where(low, x if g == 0 else pltpu.roll(x, HEAD_DIM, axis=1), 1.0)

    js = range(n_heads // 2)
    kvg = [(2 * j) // group for j in js]
    qs, sk = [], []
    for j in js:
        q2 = q[:, j * LANE:(j + 1) * LANE]
        if has_local:
            q2 = _rope(q2, cos_q, sin_q, fq)
        q2 = q2 * scale
        qs.append(jnp.concatenate([jnp.where(low, q2, 0.0), jnp.where(low, 0.0, q2)], axis=0))
        sk.append(jnp.concatenate([jnp.broadcast_to(sink_ref[h:h + 1, 0:1], (tq, 1)) for h in (2 * j, 2 * j + 1)],
                                  axis=0))
    kc_g = [dup(kc, g) for g in range(SWA_KV_HEADS)]
    vc_g = [with_ones(vc, g) for g in range(SWA_KV_HEADS)]
    s_ctx = [_mm_nt(qs[j], kc_g[kvg[j]]) for j in js]
    m = [jnp.maximum(jnp.max(s_ctx[j], -1, keepdims=True), sk[j]) for j in js]
    if has_local:
        k_g = [dup(k, g) for g in range(SWA_KV_HEADS)]
        v_g = [with_ones(v, g) for g in range(SWA_KV_HEADS)]
        s_loc = [jnp.where(ok, _mm_nt(qs[j], k_g[kvg[j]]), -jnp.inf) for j in js]
        m = [jnp.maximum(m[j], jnp.max(s_loc[j], -1, keepdims=True)) for j in js]
    acc = [_mm(jnp.exp(s_ctx[j] - m[j]), vc_g[kvg[j]]) for j in js]
    if has_local:
        acc = [acc[j] + _mm(jnp.exp(s_loc[j] - m[j]), v_g[kvg[j]]) for j in js]
    den = [pltpu.roll(acc[j], HEAD_DIM, axis=1) + jnp.exp(sk[j] - m[j]) for j in js]
    for j in js:
        o = acc[j] / den[j]
        o_ref[0, :, j * LANE:(j + 1) * LANE] = jnp.where(low, o[:tq], pltpu.roll(o[tq:], HEAD_DIM, axis=1))


def _sink_rows(sink):
    return jnp.broadcast_to(sink.astype(F32)[:, None], (sink.shape[0], LANE))


def windowed_attention(p_lat, p_ctx, sink):
    bsz, L, cols = p_lat.shape
    Lc = p_ctx.shape[1]
    q_cols = cols - 2 * LANE
    n_heads = q_cols // HEAD_DIM
    tq = _row_tile(L, SWA_QUERY_TILE)
    span = min(L, tq + 2 * WINDOW)
    assert q_cols % (2 * LANE) == 0 and tq % SWA_BLOCK == 0
    kvb = q_cols // (2 * LANE)
    cos, sin = rope_tables(L)
    return pl.pallas_call(
        functools.partial(_attn_kernel, has_local=True, L=L, span=span, group=n_heads // SWA_KV_HEADS),
        grid=(bsz, L // tq),
        in_specs=[pl.BlockSpec((1, tq, q_cols), lambda b, i: (b, i, 0)),
                  pl.BlockSpec((1, L, 2 * LANE), lambda b, i: (b, 0, kvb)),
                  pl.BlockSpec((1, Lc, 2 * LANE), lambda b, i: (b, 0, kvb)),
                  pl.BlockSpec((L, LANE), lambda b, i: (0, 0)),
                  pl.BlockSpec((L, LANE), lambda b, i: (0, 0)),
                  pl.BlockSpec((n_heads, LANE), lambda b, i: (0, 0))],
        out_specs=pl.BlockSpec((1, tq, q_cols), lambda b, i: (b, i, 0)),
        out_shape=jax.ShapeDtypeStruct((bsz, L, q_cols), F32),
        compiler_params=_params("parallel", "parallel"),
        name="windowed_attention",
    )(p_lat, p_lat, p_ctx, cos, sin, _sink_rows(sink))


def context_attention(p_ctx, sink):
    bsz, Lc, cols = p_ctx.shape
    q_cols = cols - 2 * LANE
    n_heads = q_cols // HEAD_DIM
    tq = min(Lc, SWA_BLOCK)
    kvb = q_cols // (2 * LANE)
    return pl.pallas_call(
        functools.partial(_attn_kernel, has_local=False, L=Lc, span=0, group=n_heads // SWA_KV_HEADS),
        grid=(bsz, Lc // tq),
        in_specs=[pl.BlockSpec((1, tq, q_cols), lambda b, i: (b, i, 0)),
                  pl.BlockSpec((1, Lc, 2 * LANE), lambda b, i: (b, 0, kvb)),
                  pl.BlockSpec((n_heads, LANE), lambda b, i: (0, 0))],
        out_specs=pl.BlockSpec((1, tq, q_cols), lambda b, i: (b, i, 0)),
        out_shape=jax.ShapeDtypeStruct((bsz, Lc, q_cols), F32),
        compiler_params=_params("parallel", "parallel"),
        name="context_attention",
    )(p_ctx, p_ctx, _sink_rows(sink))


def _rwkv_feat_kernel(p_ref, hp_ref, hn_ref, mu_ref, w0_ref, w2_ref, a0_ref, a2_ref, g2_ref, kk_ref, ka_ref,
                      rk_ref, e_ref, rvk_o, g_o, bonus_o, lw_o, kb_o, *, rw_w):
    p = p_ref[0]
    p_prev, p_next = _neighbours(p, *_halo_rows(hp_ref, hn_ref))
    mu = mu_ref[...]
    p = p * (1.0 - mu) + (p_prev + p_next) * (0.5 * mu)
    o1, o2, o3 = rw_w, 2 * rw_w, 3 * rw_w
    o4 = o3 + 2 * RW_DECAY_LORA
    o5 = o4 + 2 * RW_AAA_LORA
    r, k, v = p[:, :o1], p[:, o1:o2], p[:, o2:o3]
    wd, ad, gd = p[:, o3:o4], p[:, o4:o5], p[:, o5:]
    w_log = -_softplus(-(w0_ref[...] + _mm(jnp.tanh(wd), w2_ref[...]))) - 0.5
    lw = -jnp.exp(w_log)
    a = _sigmoid(a0_ref[...] + _mm(ad, a2_ref[...]))
    e = e_ref[...]
    kq = k * kk_ref[...]
    kkn = kq * lax.rsqrt(_head_sum(kq * kq, e) + 1e-6)
    rvk_o[0] = jnp.concatenate([r, v, kkn], axis=1).astype(BF16)
    g_o[0] = _mm(_sigmoid(gd), g2_ref[...])
    ka = ka_ref[...]
    rk = r * rk_ref[...]
    acc = jnp.zeros_like(r)
    for d in range(2):
        a_d = a[:, d * rw_w:(d + 1) * rw_w]
        k_d = k * (1.0 + (a_d - 1.0) * ka)
        lw_o[d, 0] = lw[:, d * rw_w:(d + 1) * rw_w]
        kb_o[d, 0] = jnp.concatenate([k_d, kkn * a_d], axis=1).astype(BF16)
        acc = acc + rk * k_d
    bonus_o[0] = _head_sum(acc, e) * v


def _head_indicator(w, head):
    hid = jnp.arange(w) // head
    return (hid[:, None] == hid[None, :]).astype(BF16)


def rwkv_features(p, mu, w0, w2, a0, a2, g2, k_k, k_a, r_k):
    bsz, T, cols = p.shape
    rw_w = g2.shape[1]
    tm = _row_tile(T, 256)
    w2_bd = jnp.zeros((2 * RW_DECAY_LORA, 2 * rw_w), F32)
    w2_bd = w2_bd.at[:RW_DECAY_LORA, :rw_w].set(w2[0]).at[RW_DECAY_LORA:, rw_w:].set(w2[1]).astype(BF16)
    a2_bd = jnp.zeros((2 * RW_AAA_LORA, 2 * rw_w), F32)
    a2_bd = a2_bd.at[:RW_AAA_LORA, :rw_w].set(a2[0]).at[RW_AAA_LORA:, rw_w:].set(a2[1]).astype(BF16)
    vec = lambda t: t.reshape(1, -1).astype(F32)
    full = lambda shape: pl.BlockSpec(shape, lambda b, i: (0,) * len(shape))
    tok = lambda n: pl.BlockSpec((1, tm, n * rw_w), lambda b, i: (b, i, 0))
    tok2 = lambda n: pl.BlockSpec((2, 1, tm, n * rw_w), lambda b, i: (0, b, i, 0))
    s1 = lambda n, t: jax.ShapeDtypeStruct((bsz, T, n * rw_w), t)
    s2 = lambda n, t: jax.ShapeDtypeStruct((2, bsz, T, n * rw_w), t)
    return pl.pallas_call(
        functools.partial(_rwkv_feat_kernel, rw_w=rw_w),
        grid=(bsz, T // tm),
        in_specs=_halo_specs(tm, T, cols) + [
            full((1, cols)), full((1, 2 * rw_w)), full(w2_bd.shape), full((1, 2 * rw_w)),
            full(a2_bd.shape), full(g2.shape), full((1, rw_w)), full((1, rw_w)), full((1, rw_w)),
            full((rw_w, rw_w))],
        out_specs=[tok(3), tok(1), tok(1), tok2(1), tok2(2)],
        out_shape=[s1(3, BF16), s1(1, F32), s1(1, F32), s2(1, F32), s2(2, BF16)],
        compiler_params=_params("parallel", "parallel"),
        name="rwkv_features",
    )(p, p, p, vec(mu), vec(w0), w2_bd, vec(a0), a2_bd, g2.astype(BF16), vec(k_k), vec(k_a), vec(r_k),
      _head_indicator(rw_w, RW_HEAD))


def _rwkv_scan_kernel(rvkc, lwc, kbc, rvkl, lwl, kbl, y_ref, s_ref, *, n_ctx, bb):
    d = pl.program_id(1)
    c = pl.program_id(2)
    C = CHUNK
    n_pairs = s_ref.shape[0] // bb

    @pl.when(c == 0)
    def _():
        s_ref[...] = jnp.zeros_like(s_ref)

    is_ctx = c < n_ctx
    pick = lambda a, b: jnp.where(is_ctx, a, b)
    rev = d == 1
    ti = lax.broadcasted_iota(jnp.int32, (C, C), 0)
    tj = lax.broadcasted_iota(jnp.int32, (C, C), 1)
    tri = jnp.where(jnp.where(rev, tj - ti, ti - tj) >= 0, 1.0, 0.0).astype(BF16)

    def features(bi):
        w = lwc.shape[-1]
        rvk = pick(rvkc[bi], rvkl[bi])
        r, v, kk = rvk[:, :w], rvk[:, w:2 * w], rvk[:, 2 * w:]
        lw = pick(lwc[0, bi], lwl[0, bi])
        kb = pick(kbc[0, bi], kbl[0, bi])
        kd, bd = kb[:, :w], kb[:, w:]
        linc = _mm_exact_lhs(tri, lw)
        ltot = jnp.sum(lw, axis=0, keepdims=True)
        g_inv = jnp.exp(-linc)
        g_end = jnp.exp(ltot - linc)
        return (kk * jnp.exp(linc - lw), r * jnp.exp(linc), kd * g_inv, bd * g_inv, kd * g_end, bd * g_end, v,
                jnp.exp(ltot))

    feats = [features(bi) for bi in range(bb)]

    n = 2 * C
    rid = lax.broadcasted_iota(jnp.int32, (n, n), 0)
    cid = lax.broadcasted_iota(jnp.int32, (n, n), 1)
    same = (rid >> 6) == (cid >> 6)
    diff = jnp.where(rev, cid - rid, rid - cid)
    strict = lambda m: jnp.where(same, jnp.where(diff > 0, m, 0.0), 0.0)
    incl = lambda m: jnp.where(same, jnp.where(diff >= 0, m, 0.0), 0.0)
    first = lax.broadcasted_iota(jnp.int32, (1, LANE), 1) < RW_HEAD

    def stack(x):
        return jnp.concatenate([jnp.where(first, x, 0.0), jnp.where(first, 0.0, x)], axis=0)

    prob = [(bi, slice(p * LANE, (p + 1) * LANE)) for bi in range(bb) for p in range(n_pairs)]
    qs = range(len(prob))
    stk = [[stack(t[:, sl]) for t in feats[bi][:7]] for bi, sl in prob]
    kks, rs, kis, bis, kes, bes, vs = ([stk[q][i] for q in qs] for i in range(7))
    m = [_mm_nt(jnp.concatenate([kks[q], rs[q]], axis=0), jnp.concatenate([kis[q], bis[q]], axis=0)) for q in qs]
    a_k = [strict(t[:n, :n]) for t in m]
    a_b = [strict(t[:n, n:]) for t in m]
    b_k = [incl(t[n:, :n]) for t in m]
    b_b = [incl(t[n:, n:]) for t in m]
    t_inv = _unit_tri_inverse(a_b, rid, cid, C)
    abv = [_mm(jnp.concatenate([a_k[q], b_k[q]], axis=0), vs[q]) for q in qs]
    wu = [_mm(t_inv[q], jnp.concatenate([kks[q], abv[q][:n]], axis=1)) for q in qs]
    w_s = [t[:, :LANE] for t in wu]
    u0_s = [t[:, LANE:] for t in wu]
    vt = [t.T for t in vs]
    u0_t = [t.T for t in u0_s]
    st = [s_ref[q] for q in qs]
    wr = [_mm_nt(jnp.concatenate([w_s[q], rs[q]], axis=0), st[q]) for q in qs]
    u_s = [wr[q][:n] + u0_s[q] for q in qs]
    u_t = [_mm_nt(st[q], w_s[q]) + u0_t[q] for q in qs]
    y_s = [wr[q][n:] + abv[q][n:] - _mm(b_b[q], u_s[q]) for q in qs]
    for q, (bi, sl) in enumerate(prob):
        s_ref[q] = st[q] * feats[bi][7][:, sl] + _mm(jnp.concatenate([vt[q], -u_t[q]], axis=1),
                                                    jnp.concatenate([kes[q], bes[q]], axis=0))
        y_ref[0, bi, :, sl] = (y_s[q][:C] + y_s[q][C:]).astype(BF16)


def rwkv_scan(f_ctx, f_lat):
    rvkc, lwc, kbc = f_ctx
    rvkl, lwl, kbl = f_lat
    bsz, Lc, w = lwc.shape[1:]
    L = lwl.shape[2]
    C = CHUNK
    assert Lc % C == 0 and L % C == 0 and w % LANE == 0
    nc, nl = Lc // C, L // C
    bb = _row_tile(bsz, RW_SCAN_BATCH)
    ctx_idx, lat_idx = _scan_index_maps(nc, nl)
    shared = lambda idx: pl.BlockSpec((bb, C, 3 * w), lambda b, d, c: (b, idx(d, c), 0))
    directed = lambda idx, n: pl.BlockSpec((1, bb, C, n * w), lambda b, d, c: (d, b, idx(d, c), 0))
    return pl.pallas_call(
        functools.partial(_rwkv_scan_kernel, n_ctx=nc, bb=bb),
        grid=(bsz // bb, 2, nc + nl),
        in_specs=[shared(ctx_idx), directed(ctx_idx, 1), directed(ctx_idx, 2),
                  shared(lat_idx), directed(lat_idx, 1), directed(lat_idx, 2)],
        out_specs=directed(lat_idx, 1),
        out_shape=jax.ShapeDtypeStruct((2, bsz, L, w), BF16),
        scratch_shapes=[pltpu.VMEM((bb * (w // LANE), LANE, LANE), F32)],
        compiler_params=_params("parallel", "parallel", "arbitrary"),
        name="rwkv_scan",
    )(rvkc, lwc, kbc, rvkl, lwl, kbl)


def _rwkv_head_out(y, bonus, g, gn_g, gn_b, e):
    inv_n = 1.0 / RW_HEAD
    mu = _head_sum(y, e) * inv_n
    yc = y - mu
    var = _head_sum(yc * yc, e) * inv_n
    return (yc * lax.rsqrt(var + RW_GN_EPS) * gn_g + gn_b + bonus) * g


def rwkv7_scan_lat(p_lat, p_ctx, mu, w0, w2, a0, a2, g2, k_k, k_a, r_k):
    args = (mu, w0, w2, a0, a2, g2, k_k, k_a, r_k)
    rvk_l, g_l, bonus_l, lw_l, kb_l = rwkv_features(p_lat, *args)
    rvk_c, _, _, lw_c, kb_c = rwkv_features(p_ctx, *args)
    return rwkv_scan((rvk_c, lw_c, kb_c), (rvk_l, lw_l, kb_l)), bonus_l, g_l


def _gdn_feat_kernel(p_ref, hp_ref, hn_ref, g_ref, cw_ref, al_ref, dt_ref, qkv_o, gb_o, *, dn_w):
    x = p_ref[0]
    x_prev, x_next = _neighbours(x, *_halo_rows(hp_ref, hn_ref))
    cw = cw_ref[...]
    z = x_prev * cw[0:1] + x * cw[1:2] + x_next * cw[2:3]
    z = z * _sigmoid(z)
    for h in range(dn_w // DN_HEAD):
        sq = slice(h * DN_HEAD, (h + 1) * DN_HEAD)
        sk = slice(dn_w + h * DN_HEAD, dn_w + (h + 1) * DN_HEAD)
        q = z[:, sq]
        k = z[:, sk]
        qn = q * (lax.rsqrt(jnp.sum(q * q, -1, keepdims=True) + 1e-6) * (DN_HEAD ** -0.5))
        qkv_o[0, :, sq] = qn.astype(BF16)
        qkv_o[0, :, sk] = (k * lax.rsqrt(jnp.sum(k * k, -1, keepdims=True) + 1e-6)).astype(BF16)
    qkv_o[0, :, 2 * dn_w:] = z[:, 2 * dn_w:].astype(BF16)
    g = g_ref[0]
    gb_o[0, :, :LANE] = -jnp.exp(al_ref[...]) * _softplus(g[:, :LANE] + dt_ref[...])
    gb_o[0, :, LANE:] = _sigmoid(g[:, LANE:])


def gdn_features(p, conv_w, A_log, dt_bias):
    bsz, T, cols = p.shape
    dn_w = conv_w.shape[1] // 3
    tm = _row_tile(T, 256)
    c3 = 3 * dn_w
    assert (4 * dn_w) % (2 * LANE) == 0
    gcol = (4 * dn_w) // (2 * LANE)
    pad = lambda t: jnp.pad(t.reshape(1, -1).astype(F32), ((0, 0), (0, LANE - t.size)))
    tok = lambda n: pl.BlockSpec((1, tm, n), lambda b, i: (b, i, 0))
    row = pl.BlockSpec((1, LANE), lambda b, i: (0, 0))
    return pl.pallas_call(
        functools.partial(_gdn_feat_kernel, dn_w=dn_w),
        grid=(bsz, T // tm),
        in_specs=_halo_specs(tm, T, c3) + [pl.BlockSpec((1, tm, 2 * LANE), lambda b, i: (b, i, gcol)),
                                          pl.BlockSpec((8, c3), lambda b, i: (0, 0)), row, row],
        out_specs=[tok(c3), tok(2 * LANE)],
        out_shape=[jax.ShapeDtypeStruct((bsz, T, c3), BF16), jax.ShapeDtypeStruct((bsz, T, 2 * LANE), F32)],
        compiler_params=_params("parallel", "parallel"),
        name="gdn_features",
    )(p, p, p, p, jnp.pad(conv_w, ((0, 8 - conv_w.shape[0]), (0, 0))), pad(A_log), pad(dt_bias))


def _gdn_scan_kernel(qkvc, gbc, qkvl, gbl, o_ref, s_ref, *, n_ctx, bb):
    d = pl.program_id(1)
    c = pl.program_id(2)
    C = CHUNK
    heads = s_ref.shape[0] // bb

    @pl.when(c == 0)
    def _():
        s_ref[...] = jnp.zeros_like(s_ref)

    is_ctx = c < n_ctx
    pick = lambda a, b, bi: jnp.where(is_ctx, a[bi], b[bi])
    rev = d == 1

    ti = lax.broadcasted_iota(jnp.int32, (C, C), 0)
    tj = lax.broadcasted_iota(jnp.int32, (C, C), 1)
    tri = jnp.where(jnp.where(rev, tj - ti, ti - tj) >= 0, 1.0, 0.0).astype(BF16)
    lane = lax.broadcasted_iota(jnp.int32, (1, LANE), 1)

    def col(x, h):
        return jnp.sum(jnp.where(lane == d * heads + h, x, 0.0), axis=1, keepdims=True)

    n = 2 * C
    rid = lax.broadcasted_iota(jnp.int32, (n, n), 0)
    cid = lax.broadcasted_iota(jnp.int32, (n, n), 1)
    same = (rid >> 6) == (cid >> 6)
    diff = jnp.where(rev, cid - rid, rid - cid)
    keep_incl = jnp.where(same, jnp.where(diff >= 0, 1.0, 0.0), 0.0)
    eye = jnp.where(rid == cid, 1.0, 0.0)

    hs = lambda h: slice(h * DN_HEAD, (h + 1) * DN_HEAD)

    prob = [(bi, p) for bi in range(bb) for p in range(heads // 2)]
    qs = range(len(prob))
    k_s, q_s, v_s, g_col, b_col, ge_col, g_end = [], [], [], [], [], [], []
    for bi in range(bb):
        w = heads * DN_HEAD
        qkv = pick(qkvc, qkvl, bi)
        q, k, v = qkv[:, :w], qkv[:, w:2 * w], qkv[:, 2 * w:]
        gb = pick(gbc, gbl, bi)
        gl, be = gb[:, :LANE], gb[:, LANE:]
        g_cum = _mm_exact_lhs(tri, gl)
        g_tot = jnp.sum(gl, axis=0, keepdims=True)
        for p in range(heads // 2):
            h0, h1 = 2 * p, 2 * p + 1
            stack = lambda x: jnp.concatenate([x[:, hs(h0)], x[:, hs(h1)]], axis=0)
            k_s.append(stack(k))
            q_s.append(stack(q))
            v_s.append(stack(v))
            g_col.append(jnp.concatenate([col(g_cum, h0), col(g_cum, h1)], axis=0))
            b_col.append(jnp.concatenate([col(be, h0), col(be, h1)], axis=0))
            gt = [col(g_tot, h0), col(g_tot, h1)]
            ge_col.append(jnp.concatenate([jnp.broadcast_to(t, (C, 1)) for t in gt], axis=0))
            g_end.append([jnp.exp(t) for t in gt])
    g_row = [jnp.broadcast_to(g_col[i], (n, n)).T for i in qs]
    decay = [keep_incl * jnp.exp(keep_incl * (g_col[i] - g_row[i])) for i in qs]
    kq = [_mm_nt(jnp.concatenate([k_s[i], q_s[i]], axis=0), k_s[i]) for i in qs]
    a = [(1.0 - eye) * b_col[i] * kq[i][:n] * decay[i] for i in qs]
    qk = [kq[i][n:] * decay[i] for i in qs]
    t_inv = _unit_tri_inverse(a, rid, cid, C)
    eg = [jnp.exp(g_col[i]) for i in qs]
    uw = [_mm(t_inv[i], jnp.concatenate([b_col[i] * v_s[i], (b_col[i] * eg[i]) * k_s[i]], axis=1)) for i in qs]
    u0 = [t[:, :DN_HEAD] for t in uw]
    w = [t[:, DN_HEAD:] for t in uw]
    q_dec = [q_s[i] * eg[i] for i in qs]
    k_dec_t = [_mm_nt(eye, k_s[i] * jnp.exp(ge_col[i] - g_col[i])) for i in qs]
    head_cols = lax.broadcasted_iota(jnp.int32, (1, n), 1) >> 6

    st = [[s_ref[2 * i + t] for t in range(2)] for i in qs]
    ws = [[_mm(jnp.concatenate([w[i][t * C:(t + 1) * C], q_dec[i][t * C:(t + 1) * C]], axis=0), st[i][t])
           for t in range(2)] for i in qs]
    u = [jnp.concatenate([u0[i][:C] - ws[i][0][:C], u0[i][C:] - ws[i][1][:C]], axis=0) for i in qs]
    o = [jnp.concatenate([ws[i][0][C:], ws[i][1][C:]], axis=0) + _mm(qk[i], u[i]) for i in qs]
    for i, (bi, p) in enumerate(prob):
        for t in range(2):
            kd = jnp.where(head_cols == t, k_dec_t[i], 0.0)
            s_ref[2 * i + t] = g_end[i][t] * st[i][t] + _mm(kd, u[i])
            o_ref[0, bi, :, hs(2 * p + t)] = o[i][t * C:(t + 1) * C].astype(BF16)


def gdn_scan(f_ctx, f_lat):
    qkvc, gbc = f_ctx
    qkvl, gbl = f_lat
    bsz, Lc, w3 = qkvc.shape
    L = qkvl.shape[1]
    C = CHUNK
    w = w3 // 3
    heads = w // DN_HEAD
    assert Lc % C == 0 and L % C == 0 and heads % 2 == 0
    nc, nl = Lc // C, L // C
    bb = _row_tile(bsz, DN_SCAN_BATCH)
    ctx_idx, lat_idx = _scan_index_maps(nc, nl)
    spec = lambda idx, n: pl.BlockSpec((bb, C, n), lambda b, d, c: (b, idx(d, c), 0))
    return pl.pallas_call(
        functools.partial(_gdn_scan_kernel, n_ctx=nc, bb=bb),
        grid=(bsz // bb, 2, nc + nl),
        in_specs=[spec(ctx_idx, w3), spec(ctx_idx, 2 * LANE), spec(lat_idx, w3), spec(lat_idx, 2 * LANE)],
        out_specs=pl.BlockSpec((1, bb, C, w), lambda b, d, c: (d, b, lat_idx(d, c), 0)),
        out_shape=jax.ShapeDtypeStruct((2, bsz, L, w), BF16),
        scratch_shapes=[pltpu.VMEM((bb * heads, DN_HEAD, DN_HEAD), F32)],
        compiler_params=_params("parallel", "parallel", "arbitrary"),
        name="gdn_scan",
    )(qkvc, gbc, qkvl, gbl)


def _gdn_head_out(o, z, norm_g):
    outs = []
    for h in range(o.shape[1] // DN_HEAD):
        sl = slice(h * DN_HEAD, (h + 1) * DN_HEAD)
        oh = o[:, sl]
        zh = z[:, sl]
        outs.append(oh * lax.rsqrt(jnp.mean(oh * oh, -1, keepdims=True) + 1e-6) * norm_g * (zh * _sigmoid(zh)))
    return jnp.concatenate(outs, axis=1)


def gdn_scan_lat(p_lat, p_ctx, conv_w, A_log, dt_bias):
    f_lat = gdn_features(p_lat, conv_w, A_log, dt_bias)
    f_ctx = gdn_features(p_ctx, conv_w, A_log, dt_bias)
    return gdn_scan(f_ctx, f_lat)


def _odd_out_ln_kernel(y_ref, bonus_ref, g_ref, gg_ref, gb_ref, e_ref, o_ref, z_ref, ng_ref, wa_ref, wb_ref,
                       x_ref, gate_ref, lg_ref, lb_ref, out_ref):
    both = lambda ref: ref[0, 0].astype(F32) + ref[1, 0].astype(F32)
    y_c = _rwkv_head_out(both(y_ref), bonus_ref[0], g_ref[0], gg_ref[...], gb_ref[...], e_ref[...])
    y_d = _gdn_head_out(both(o_ref), z_ref[0], ng_ref[...])
    y = _mm(y_c, wa_ref[...]) + _mm(y_d, wb_ref[...])
    z = DEEPNORM_ALPHA * x_ref[0] + gate_ref[0] * y
    out_ref[0] = _layer_norm_rows(z, lg_ref[...], lb_ref[...])


def odd_out_proj_ln(y, bonus, g, gn_g, gn_b, o, p_dn, norm_g, w, x, gate, ln_g, ln_b):
    bsz, L, d = x.shape
    rw_w, dn_w = y.shape[-1], o.shape[-1]
    tm = _row_tile(L, 256)
    z_block = 3
    two = lambda w_: pl.BlockSpec((2, 1, tm, w_), lambda b, i: (0, b, i, 0))
    tok = lambda w_: pl.BlockSpec((1, tm, w_), lambda b, i: (b, i, 0))
    row = lambda w_: pl.BlockSpec((1, w_), lambda b, i: (0, 0))
    full = lambda a: pl.BlockSpec(a.shape, lambda b, i: (0, 0))
    wa, wb = w[:rw_w], w[rw_w:]
    e = _head_indicator(rw_w, RW_HEAD)
    return pl.pallas_call(
        _odd_out_ln_kernel,
        grid=(bsz, L // tm),
        in_specs=[two(rw_w), tok(rw_w), tok(rw_w), row(rw_w), row(rw_w), full(e), two(dn_w),
                  pl.BlockSpec((1, tm, dn_w), lambda b, i: (b, i, z_block)), row(DN_HEAD), full(wa), full(wb),
                  tok(d), pl.BlockSpec((1, 1, d), lambda b, i: (b, 0, 0)), row(d), row(d)],
        out_specs=tok(d),
        out_shape=jax.ShapeDtypeStruct((bsz, L, d), F32),
        compiler_params=_params("parallel", "parallel"),
        name="odd_out_proj_ln",
    )(y, bonus, g, gn_g.reshape(1, rw_w), gn_b.reshape(1, rw_w), e, o, p_dn, norm_g.reshape(1, DN_HEAD), wa, wb,
      x, gate, ln_g.reshape(1, d), ln_b.reshape(1, d))


def _gdn_in_weight(w_dn, dn_w, heads):
    padw = ((0, 0), (0, LANE - 2 * heads))
    g = w_dn[:, 4 * dn_w:]
    return jnp.concatenate([w_dn[:, :4 * dn_w], jnp.pad(g[:, :2 * heads], padw), jnp.pad(g[:, 2 * heads:], padw)], 1)


def kernel(x, c, ctx, c_ctx, mod_w, mod_b, ln_g, ln_b, mlp_w1, mlp_w2, e_w_in, e_w_out, hy_conv, hy_ffn_w1,
           hy_ffn_b1, hy_ffn_w2, hy_ffn_b2, hy_sin_freq, hy_ffn_w3, hy_decay, hy_bias, attn_sink, o_w_in,
           o_w_out, rw_mu, rw_w0, rw_w2, rw_a0, rw_a2, rw_g2, rw_kk, rw_ka, rw_rk, rw_lnx_g, rw_lnx_b,
           dn_conv, dn_A_log, dn_dt_bias, dn_norm_g):
    bsz, L, d = x.shape
    assert mod_w.shape[0] == DEPTH == 2
    cc = jnp.concatenate([c, c_ctx[None, :]], axis=0)
    cc = jnp.pad(cc, ((0, (-cc.shape[0]) % 8), (0, 0)))

    def modulation(i):
        m = mod_vectors(cc, mod_w[i], mod_b[i])
        m_lat = m[:bsz].reshape(bsz, 1, N_MOD, d)
        m_ctx = jnp.broadcast_to(m[bsz].reshape(1, 1, N_MOD, d), (bsz, 1, N_MOD, d))
        return [m_lat[:, :, t] for t in range(N_MOD)], [m_ctx[:, :, t] for t in range(N_MOD)]

    (sh1, sc1, g1, sh2, sc2, g2), (csh1, csc1, cg1, csh2, csc2, cg2) = modulation(0)
    hy_cols = hy_conv.shape[2]
    w_in = e_w_in[0].astype(BF16)
    ws = [w_in[:, :hy_cols], w_in[:, hy_cols:]]
    w_out = e_w_out[0].astype(BF16)
    w1, w2 = mlp_w1[0].astype(BF16), mlp_w2[0].astype(BF16)
    hy = (hy_conv[0], hy_ffn_w1[0], hy_ffn_b1[0], hy_ffn_w2[0], hy_ffn_b2[0], hy_sin_freq[0], hy_ffn_w3[0],
          hy_decay[0], hy_bias[0])
    p_hy, p_qkv = mod_linear(x, sc1, sh1, ws)
    pc_hy, pc_qkv = mod_linear(ctx, csc1, csh1, ws)
    y_a = hyena_mixer(p_hy, *hy)
    y_b = windowed_attention(p_qkv, pc_qkv, attn_sink[0])
    yc_a = hyena_mixer(pc_hy, *hy)
    yc_b = context_attention(pc_qkv, attn_sink[0])
    x = out_proj_ln(y_a, y_b, w_out, x, g1, ln_g[0, 0], ln_b[0, 0])
    x = mlp_ln(x, sc2, sh2, g2, w1, w2, ln_g[0, 1], ln_b[0, 1])
    ctx = out_proj_ln(yc_a, yc_b, w_out, ctx, cg1, ln_g[0, 0], ln_b[0, 0])
    ctx = mlp_ln(ctx, csc2, csh2, cg2, w1, w2, ln_g[0, 1], ln_b[0, 1])

    (sh1, sc1, g1, sh2, sc2, g2), (csh1, csc1, _, _, _, _) = modulation(1)
    rw_cols = rw_mu.shape[1]
    dn_w = dn_conv.shape[2] // 3
    w_in = o_w_in[0]
    ws = [w_in[:, :rw_cols].astype(BF16), _gdn_in_weight(w_in[:, rw_cols:], dn_w, dn_w // DN_HEAD).astype(BF16)]
    w_out = o_w_out[0].astype(BF16)
    w1, w2 = mlp_w1[1].astype(BF16), mlp_w2[1].astype(BF16)
    p_rw, p_dn = mod_linear(x, sc1, sh1, ws)
    pc_rw, pc_dn = mod_linear(ctx, csc1, csh1, ws)
    y_rw, bonus, g_rw = rwkv7_scan_lat(p_rw, pc_rw, rw_mu[0], rw_w0[0], rw_w2[0], rw_a0[0], rw_a2[0], rw_g2[0],
                                       rw_kk[0], rw_ka[0], rw_rk[0])
    o_dn = gdn_scan_lat(p_dn, pc_dn, dn_conv[0], dn_A_log[0], dn_dt_bias[0])
    x = odd_out_proj_ln(y_rw, bonus, g_rw, rw_lnx_g[0], rw_lnx_b[0], o_dn, p_dn, dn_norm_g[0], w_out, x, g1,
                        ln_g[1, 0], ln_b[1, 0])
    x = mlp_ln(x, sc2, sh2, g2, w1, w2, ln_g[1, 1], ln_b[1, 1])
    return x
```

```python
import functools
import math

import jax
import jax.numpy as jnp
from jax import lax
from jax.experimental import pallas as pl
from jax.experimental.pallas import tpu as pltpu

F32 = jnp.float32
BF16 = jnp.bfloat16

DEPTH = 2
N_MOD = 6
DEEPNORM_ALPHA = (2.0 * DEPTH) ** 0.25
LN_EPS = 1e-5

HY_BANDS = 16
HEAD_DIM = 64
SWA_KV_HEADS = 2
WINDOW = 128
SWA_BLOCK = 128
SWA_QUERY_TILE = 128
GRID_W = 64
ROPE_BASE = 10000.0

RW_HEAD = 64
RW_DECAY_LORA = 64
RW_AAA_LORA = 64
RW_GN_EPS = 64e-5

DN_HEAD = 128
CHUNK = 64
RW_SCAN_BATCH = 8
DN_SCAN_BATCH = 16

LANE = 128
MXU_WIDTH = 256
VMEM_LIMIT = 56 * 1024 * 1024


def _row_tile(n, cap):
    t = min(n, cap)
    while n % t:
        t //= 2
    return t


def _params(*sem):
    return pltpu.CompilerParams(dimension_semantics=sem, vmem_limit_bytes=VMEM_LIMIT)


def _mm(a, b):
    return jnp.dot(a.astype(BF16), b.astype(BF16), preferred_element_type=F32)


def _mm_nt(a, b):
    return lax.dot_general(a.astype(BF16), b.astype(BF16), (((1,), (1,)), ((), ())), preferred_element_type=F32)


def _split2(x):
    hi = x.astype(BF16)
    return hi, (x - hi.astype(F32)).astype(BF16)


def _split3(x):
    hi = x.astype(BF16)
    r = x - hi.astype(F32)
    mid = r.astype(BF16)
    return hi, mid, (r - mid.astype(F32)).astype(BF16)


def _mm_exact_lhs(a_bf16, x):
    return sum(jnp.dot(a_bf16, t, preferred_element_type=F32) for t in _split3(x))


def _head_sum(x, e_bf16):
    return sum(jnp.dot(t, e_bf16, preferred_element_type=F32) for t in _split2(x))


def _mm_f32(a, b):
    ah, al = _split2(a)
    bh, bl = _split2(b)
    dot = lambda x, y: jnp.dot(x, y, preferred_element_type=F32)
    return dot(ah, bh) + dot(ah, bl) + dot(al, bh)


def _sigmoid(x):
    return 1.0 / (1.0 + jnp.exp(-x))


def _softplus(x):
    return jnp.maximum(x, 0.0) + jnp.log(1.0 + jnp.exp(-jnp.abs(x)))


def _neighbours(x, prev_row, next_row):
    n = x.shape[0]
    rows = lax.broadcasted_iota(jnp.int32, (n, 1), 0)
    x_prev = jnp.where(rows == 0, prev_row, pltpu.roll(x, 1, axis=0))
    x_next = jnp.where(rows == n - 1, next_row, pltpu.roll(x, n - 1, axis=0))
    return x_prev, x_next


def _halo_rows(hp_ref, hn_ref):
    i = pl.program_id(1)
    nt = pl.num_programs(1)
    return jnp.where(i > 0, hp_ref[0][7:8, :], 0.0), jnp.where(i < nt - 1, hn_ref[0][0:1, :], 0.0)


def _halo_specs(tm, T, cols, col_block=0):
    hb = tm // 8
    return [pl.BlockSpec((1, tm, cols), lambda b, i: (b, i, col_block)),
            pl.BlockSpec((1, 8, cols), lambda b, i: (b, jnp.maximum(i * hb - 1, 0), col_block)),
            pl.BlockSpec((1, 8, cols), lambda b, i: (b, jnp.minimum((i + 1) * hb, T // 8 - 1), col_block))]


def _unit_tri_inverse(a_list, rid, cid, blk):
    eye = jnp.where(rid == cid, 1.0, 0.0)
    pair = (rid >> 1) == (cid >> 1)
    d_list = [eye - jnp.where(pair, a, 0.0) for a in a_list]
    s, sh = 2, 1
    while s < blk:
        join = jnp.where((rid >> (sh + 1)) == (cid >> (sh + 1)), jnp.where((rid >> sh) != (cid >> sh), 1.0, 0.0), 0.0)
        dq = [_mm(d, a * join) for d, a in zip(d_list, a_list)]
        d_list = [d - _mm(t, d) for t, d in zip(dq, d_list)]
        s, sh = 2 * s, sh + 1
    return d_list


def _scan_index_maps(nc, nl):
    def ctx_idx(d, c):
        cc = jnp.minimum(c, nc - 1)
        return jnp.where(d == 1, nc - 1 - cc, cc)

    def lat_idx(d, c):
        cl = jnp.maximum(c - nc, 0)
        return jnp.where(d == 1, nl - 1 - cl, cl)

    return ctx_idx, lat_idx


def _mod_kernel(c_ref, w_ref, b_ref, o_ref):
    c = c_ref[...]
    o_ref[...] = _mm(c * _sigmoid(c), w_ref[...]) + b_ref[...]


def mod_vectors(cc, w, b):
    r, d = cc.shape
    n = w.shape[1]
    tn = _row_tile(n, 1024)
    return pl.pallas_call(
        _mod_kernel,
        grid=(n // tn,),
        in_specs=[pl.BlockSpec((r, d), lambda j: (0, 0)),
                  pl.BlockSpec((d, tn), lambda j: (0, j)),
                  pl.BlockSpec((1, tn), lambda j: (0, j))],
        out_specs=pl.BlockSpec((r, tn), lambda j: (0, j)),
        out_shape=jax.ShapeDtypeStruct((r, n), F32),
        name="mod_vectors",
    )(cc, w, b.reshape(1, n))


def _mod_linear_kernel(x_ref, sc_ref, sh_ref, *refs):
    n = len(refs) // 2
    h = (x_ref[0] * (1.0 + sc_ref[0]) + sh_ref[0]).astype(BF16)
    for w_ref, o_ref in zip(refs[:n], refs[n:]):
        o_ref[0] = jnp.dot(h, w_ref[...], preferred_element_type=F32)


def mod_linear(x, sc, sh, ws):
    bsz, L, d = x.shape
    tm = _row_tile(L, 512)
    vec = pl.BlockSpec((1, 1, d), lambda b, i: (b, 0, 0))
    return pl.pallas_call(
        _mod_linear_kernel,
        grid=(bsz, L // tm),
        in_specs=[pl.BlockSpec((1, tm, d), lambda b, i: (b, i, 0)), vec, vec]
        + [pl.BlockSpec(w.shape, lambda b, i: (0, 0)) for w in ws],
        out_specs=[pl.BlockSpec((1, tm, w.shape[1]), lambda b, i: (b, i, 0)) for w in ws],
        out_shape=[jax.ShapeDtypeStruct((bsz, L, w.shape[1]), F32) for w in ws],
        compiler_params=_params("parallel", "parallel"),
        name="mod_linear",
    )(x, sc, sh, *ws)


def _layer_norm_rows(z, g, b):
    mu = jnp.mean(z, -1, keepdims=True)
    zc = z - mu
    var = jnp.mean(zc * zc, -1, keepdims=True)
    return zc * lax.rsqrt(var + LN_EPS) * g + b


def _out_ln_kernel(ya_ref, yb_ref, wa_ref, wb_ref, x_ref, g_ref, lg_ref, lb_ref, o_ref):
    y = _mm(ya_ref[0], wa_ref[...]) + _mm(yb_ref[0], wb_ref[...])
    z = DEEPNORM_ALPHA * x_ref[0] + g_ref[0] * y
    o_ref[0] = _layer_norm_rows(z, lg_ref[...], lb_ref[...])


def out_proj_ln(ya, yb, w, x, gate, ln_g, ln_b):
    bsz, L, d = x.shape
    ka, kb = ya.shape[-1], yb.shape[-1]
    tm = _row_tile(L, 512)
    row = pl.BlockSpec((1, d), lambda b, i: (0, 0))
    return pl.pallas_call(
        _out_ln_kernel,
        grid=(bsz, L // tm),
        in_specs=[pl.BlockSpec((1, tm, ka), lambda b, i: (b, i, 0)),
                  pl.BlockSpec((1, tm, kb), lambda b, i: (b, i, 0)),
                  pl.BlockSpec((ka, d), lambda b, i: (0, 0)),
                  pl.BlockSpec((kb, d), lambda b, i: (0, 0)),
                  pl.BlockSpec((1, tm, d), lambda b, i: (b, i, 0)),
                  pl.BlockSpec((1, 1, d), lambda b, i: (b, 0, 0)), row, row],
        out_specs=pl.BlockSpec((1, tm, d), lambda b, i: (b, i, 0)),
        out_shape=jax.ShapeDtypeStruct((bsz, L, d), F32),
        compiler_params=_params("parallel", "parallel"),
        name="out_proj_ln",
    )(ya, yb, w[:ka], w[ka:], x, gate, ln_g.reshape(1, d), ln_b.reshape(1, d))


def _mlp_ln_kernel(x_ref, sc_ref, sh_ref, g_ref, w1_ref, w2_ref, lg_ref, lb_ref, o_ref, *, ff_tile):
    x = x_ref[0]
    h = (x * (1.0 + sc_ref[0]) + sh_ref[0]).astype(BF16)
    d_ff = w1_ref.shape[1]
    acc = jnp.zeros(x.shape, F32)
    for c in range(d_ff // ff_tile):
        a = jnp.dot(h, w1_ref[:, c * ff_tile:(c + 1) * ff_tile], preferred_element_type=F32)
        a = jnp.maximum(a, 0.0)
        a = (a * a).astype(BF16)
        acc = acc + jnp.dot(a, w2_ref[c * ff_tile:(c + 1) * ff_tile, :], preferred_element_type=F32)
    z = DEEPNORM_ALPHA * x + g_ref[0] * acc
    o_ref[0] = _layer_norm_rows(z, lg_ref[...], lb_ref[...])


def mlp_ln(x, sc, sh, gate, w1, w2, ln_g, ln_b):
    bsz, L, d = x.shape
    d_ff = w1.shape[1]
    tm = _row_tile(L, 512)
    ff_tile = _row_tile(d_ff, 1024)
    once = dict(pipeline_mode=pl.Buffered(1))
    vec = pl.BlockSpec((1, 1, d), lambda b, i: (b, 0, 0))
    row = pl.BlockSpec((1, d), lambda b, i: (0, 0))
    return pl.pallas_call(
        functools.partial(_mlp_ln_kernel, ff_tile=ff_tile),
        grid=(bsz, L // tm),
        in_specs=[pl.BlockSpec((1, tm, d), lambda b, i: (b, i, 0)), vec, vec, vec,
                  pl.BlockSpec((d, d_ff), lambda b, i: (0, 0), **once),
                  pl.BlockSpec((d_ff, d), lambda b, i: (0, 0), **once), row, row],
        out_specs=pl.BlockSpec((1, tm, d), lambda b, i: (b, i, 0)),
        out_shape=jax.ShapeDtypeStruct((bsz, L, d), F32),
        compiler_params=_params("parallel", "parallel"),
        name="mlp_ln",
    )(x, sc, sh, gate, w1, w2, ln_g.reshape(1, d), ln_b.reshape(1, d))


def dft_tables(T):
    n = 2 * T
    f = jnp.arange(T, dtype=jnp.int32)
    ang = ((f[:, None] * f[None, :]) % n).astype(F32) * (2.0 * math.pi / n)
    return jnp.cos(ang).astype(BF16), (-jnp.sin(ang)).astype(BF16)


def hyena_pos_features(T):
    t = jnp.arange(T, dtype=F32)
    t_norm = t / max(T - 1, 1)
    bands = jnp.linspace(1e-4, HY_BANDS - 1, HY_BANDS, dtype=F32)
    ang = 2.0 * math.pi * t[:, None] * bands[None, :] / T
    pe = jnp.concatenate([t_norm[:, None], jnp.cos(ang), -jnp.sin(ang)], axis=-1)
    return jnp.pad(pe, ((0, 0), (0, (-pe.shape[1]) % 8)))


def _hy_filter_kernel(pe_ref, w1_ref, b1_ref, w2_ref, b2_ref, fr_ref, w3f_ref, w3b_ref, df_ref, db_ref,
                      cm_ref, sm_ref, kr_o, ki_o, kn_o):
    T = pe_ref.shape[0]
    pe = pe_ref[...]
    fr = fr_ref[...]
    h = jnp.sin(fr * (_mm_f32(pe, w1_ref[...]) + b1_ref[...]))
    h = jnp.sin(fr * (_mm_f32(h, w2_ref[...]) + b2_ref[...]))
    t_norm = pe[:, 0:1]
    h_f = _mm_f32(h, w3f_ref[...]) * jnp.exp(-t_norm * jnp.abs(df_ref[...]))
    h_b = _mm_f32(h, w3b_ref[...]) * jnp.exp(-t_norm * jnp.abs(db_ref[...]))
    row = lax.broadcasted_iota(jnp.int32, (T, 1), 0)
    h_b = jnp.where(row == 0, 0.0, h_b)
    scale = jnp.where(row == 0, 1.0, 2.0) * (1.0 / (2 * T))
    kr_o[...] = _mm(cm_ref[...], h_f + h_b) * scale
    ki_o[...] = _mm(sm_ref[...], h_f - h_b) * scale
    sign = jnp.where((row & 1) == 0, 1.0, -1.0)
    kn = jnp.sum(sign * (h_f + h_b), axis=0, keepdims=True) * (1.0 / (2 * T))
    kn_o[...] = jnp.broadcast_to(kn, kn_o.shape)


def hyena_spectrum(T, cm, sm, w1, b1, w2, b2, freq, w3, decay):
    hy_w = w3.shape[1] // 2
    ct = MXU_WIDTH
    nj = hy_w // ct
    pe = hyena_pos_features(T)
    w1p = jnp.pad(w1, ((0, pe.shape[1] - w1.shape[0]), (0, 0)))
    row = lambda t: t.reshape(1, -1)
    full = lambda a: pl.BlockSpec(a.shape, lambda j: (0,) * a.ndim)
    ins = (pe, w1p, row(b1), w2, row(b2), row(freq))
    return pl.pallas_call(
        _hy_filter_kernel,
        grid=(nj,),
        in_specs=[full(a) for a in ins] + [
            pl.BlockSpec((w3.shape[0], ct), lambda j: (0, j)), pl.BlockSpec((w3.shape[0], ct), lambda j: (0, nj + j)),
            pl.BlockSpec((1, ct), lambda j: (0, j)), pl.BlockSpec((1, ct), lambda j: (0, nj + j)),
            pl.BlockSpec((T, T), lambda j: (0, 0)), pl.BlockSpec((T, T), lambda j: (0, 0))],
        out_specs=[pl.BlockSpec((T, ct), lambda j: (0, j)), pl.BlockSpec((T, ct), lambda j: (0, j)),
                   pl.BlockSpec((8, ct), lambda j: (0, j))],
        out_shape=[jax.ShapeDtypeStruct((T, hy_w), F32), jax.ShapeDtypeStruct((T, hy_w), F32),
                   jax.ShapeDtypeStruct((8, hy_w), F32)],
        compiler_params=_params("arbitrary"),
        name="hyena_spectrum",
    )(*ins, w3, w3, row(decay), row(decay), cm, sm)


def _hy_conv_kernel(x0_ref, x1_ref, xv_ref, c0_ref, c1_ref, cv_ref, bias_ref, kr_ref, ki_ref, kn_ref,
                    cm_ref, sm_ref, o_ref):
    T = x0_ref.shape[1]
    row = lax.broadcasted_iota(jnp.int32, (T, 1), 0)
    zero = jnp.zeros((1, x0_ref.shape[2]), F32)

    def short_conv(x_ref, c_ref):
        x = x_ref[0]
        c = c_ref[...]
        x_prev, x_next = _neighbours(x, zero, zero)
        return x_prev * c[0:1] + x * c[1:2] + x_next * c[2:3]

    u = short_conv(x1_ref, c1_ref) * short_conv(xv_ref, cv_ref)
    ub = u.astype(BF16)
    sign = jnp.where((row & 1) == 0, 1.0, -1.0)
    un = jnp.sum(sign * u, axis=0, keepdims=True)
    y = sign * (un * kn_ref[0:1, :]) + u * bias_ref[...]
    ft = min(T, 512)
    yr, yi = [], []
    for f0 in range(0, T, ft):
        fs = slice(f0, f0 + ft)
        ur = jnp.dot(cm_ref[fs, :], ub, preferred_element_type=F32)
        ui = jnp.dot(sm_ref[fs, :], ub, preferred_element_type=F32)
        kr = kr_ref[fs, :]
        ki = ki_ref[fs, :]
        yr.append((ur * kr - ui * ki).astype(BF16))
        yi.append((ur * ki + ui * kr).astype(BF16))
    y = y + jnp.dot(cm_ref[...], jnp.concatenate(yr, axis=0), preferred_element_type=F32)
    y = y + jnp.dot(sm_ref[...], jnp.concatenate(yi, axis=0), preferred_element_type=F32)
    o_ref[0] = short_conv(x0_ref, c0_ref) * y


def hyena_mixer(p, conv_w, w1, b1, w2, b2, freq, w3, decay, bias):
    bsz, T, _ = p.shape
    hy_w = bias.shape[0]
    ct = MXU_WIDTH
    nj = hy_w // ct
    cm, sm = dft_tables(T)
    kr, ki, kn = hyena_spectrum(T, cm, sm, w1, b1, w2, b2, freq, w3, decay)
    cw = jnp.pad(conv_w, ((0, 8 - conv_w.shape[0]), (0, 0)))
    once = dict(pipeline_mode=pl.Buffered(1))
    xs = lambda part: pl.BlockSpec((1, T, ct), lambda j, b: (b, 0, part * nj + j))
    cs = lambda part: pl.BlockSpec((8, ct), lambda j, b: (0, part * nj + j))
    return pl.pallas_call(
        _hy_conv_kernel,
        grid=(nj, bsz),
        in_specs=[xs(0), xs(1), xs(2), cs(0), cs(1), cs(2),
                  pl.BlockSpec((1, ct), lambda j, b: (0, j)),
                  pl.BlockSpec((T, ct), lambda j, b: (0, j), **once),
                  pl.BlockSpec((T, ct), lambda j, b: (0, j), **once),
                  pl.BlockSpec((8, ct), lambda j, b: (0, j)),
                  pl.BlockSpec((T, T), lambda j, b: (0, 0), **once),
                  pl.BlockSpec((T, T), lambda j, b: (0, 0), **once)],
        out_specs=pl.BlockSpec((1, T, ct), lambda j, b: (b, 0, j)),
        out_shape=jax.ShapeDtypeStruct((bsz, T, hy_w), F32),
        compiler_params=_params("parallel", "parallel"),
        name="hyena_conv",
    )(p, p, p, cw, cw, cw, bias.reshape(1, hy_w), kr, ki, kn, cm, sm)


def rope_tables(L):
    quarter = HEAD_DIM // 4
    pos = jnp.arange(L, dtype=jnp.int32)
    inv_freq = ROPE_BASE ** (-jnp.arange(quarter, dtype=F32) / quarter)
    a_row = (pos // GRID_W).astype(F32)[:, None] * inv_freq[None, :]
    a_col = (pos % GRID_W).astype(F32)[:, None] * inv_freq[None, :]
    ang = jnp.concatenate([a_row, a_row, a_col, a_col], axis=-1)
    sgn = jnp.tile(jnp.concatenate([-jnp.ones(quarter, F32), jnp.ones(quarter, F32)]), 2)
    return jnp.tile(jnp.cos(ang), (1, 2)), jnp.tile(jnp.sin(ang) * sgn, (1, 2))


def _rope(t, cos, sin, first_quarter):
    w = t.shape[1]
    partner = jnp.where(first_quarter, pltpu.roll(t, w - 16, axis=1), pltpu.roll(t, 16, axis=1))
    return t * cos + partner * sin


def _attn_kernel(*refs, has_local, L, span, group):
    if has_local:
        q_ref, kv_ref, kvc_ref, cos_ref, sin_ref, sink_ref, o_ref = refs
    else:
        q_ref, kvc_ref, sink_ref, o_ref = refs
    i = pl.program_id(1)
    tq = q_ref.shape[1]
    lane = lax.broadcasted_iota(jnp.int32, (1, LANE), 1)
    low = lane < HEAD_DIM
    q = q_ref[0]
    n_heads = q.shape[1] // HEAD_DIM
    kvc = kvc_ref[0]
    kc, vc = kvc[:, :LANE], kvc[:, LANE:]
    if has_local:
        fq = (lane & 31) < 16
        q0 = pl.multiple_of(i * tq, tq)
        cos_q = cos_ref[pl.ds(q0, tq), :]
        sin_q = sin_ref[pl.ds(q0, tq), :]
        k0 = pl.multiple_of(jnp.clip(i * tq - WINDOW, 0, L - span), SWA_BLOCK)
        kv = kv_ref[0, pl.ds(k0, span), :]
        k = _rope(kv[:, :LANE], cos_ref[pl.ds(k0, span), :], sin_ref[pl.ds(k0, span), :], fq)
        v = kv[:, LANE:]
        qpos = q0 + lax.broadcasted_iota(jnp.int32, (tq, span), 0)
        kpos = k0 + lax.broadcasted_iota(jnp.int32, (tq, span), 1)
        ok1 = jnp.abs(qpos - kpos) <= WINDOW
        ok = jnp.concatenate([ok1, ok1], axis=0)
    scale = HEAD_DIM ** -0.5

    def dup(x, g):
        sw = pltpu.roll(x, HEAD_DIM, axis=1)
        return jnp.where(low, x, sw) if g == 0 else jnp.where(low, sw, x)

    def with_ones(x, g):
        return jnp.where(low, x if g == 0 else pltpu.roll(x, HEAD_DIM, axis=1), 1.0)

    js = range(n_heads // 2)
    kvg = [(2 * j) // group for j in js]
    qs, sk = [], []
    for j in js:
        q2 = q[:, j * LANE:(j + 1) * LANE]
        if has_local:
            q2 = _rope(q2, cos_q, sin_q, fq)
        q2 = q2 * scale
        qs.append(jnp.concatenate([jnp.where(low, q2, 0.0), jnp.where(low, 0.0, q2)], axis=0))
        sk.append(jnp.concatenate([jnp.broadcast_to(sink_ref[h:h + 1, 0:1], (tq, 1)) for h in (2 * j, 2 * j + 1)],
                                  axis=0))
    kc_g = [dup(kc, g) for g in range(SWA_KV_HEADS)]
    vc_g = [with_ones(vc, g) for g in range(SWA_KV_HEADS)]
    s_ctx = [_mm_nt(qs[j], kc_g[kvg[j]]) for j in js]
    m = [jnp.maximum(jnp.max(s_ctx[j], -1, keepdims=True), sk[j]) for j in js]
    if has_local:
        k_g = [dup(k, g) for g in range(SWA_KV_HEADS)]
        v_g = [with_ones(v, g) for g in range(SWA_KV_HEADS)]
        s_loc = [jnp.where(ok, _mm_nt(qs[j], k_g[kvg[j]]), -jnp.inf) for j in js]
        m = [jnp.maximum(m[j], jnp.max(s_loc[j], -1, keepdims=True)) for j in js]
    acc = [_mm(jnp.exp(s_ctx[j] - m[j]), vc_g[kvg[j]]) for j in js]
    if has_local:
        acc = [acc[j] + _mm(jnp.exp(s_loc[j] - m[j]), v_g[kvg[j]]) for j in js]
    den = [pltpu.roll(acc[j], HEAD_DIM, axis=1) + jnp.exp(sk[j] - m[j]) for j in js]
    for j in js:
        o = acc[j] / den[j]
        o_ref[0, :, j * LANE:(j + 1) * LANE] = jnp.where(low, o[:tq], pltpu.roll(o[tq:], HEAD_DIM, axis=1))


def _sink_rows(sink):
    return jnp.broadcast_to(sink.astype(F32)[:, None], (sink.shape[0], LANE))


def windowed_attention(p_lat, p_ctx, sink):
    bsz, L, cols = p_lat.shape
    Lc = p_ctx.shape[1]
    q_cols = cols - 2 * LANE
    n_heads = q_cols // HEAD_DIM
    tq = _row_tile(L, SWA_QUERY_TILE)
    span = min(L, tq + 2 * WINDOW)
    assert q_cols % (2 * LANE) == 0 and tq % SWA_BLOCK == 0
    kvb = q_cols // (2 * LANE)
    cos, sin = rope_tables(L)
    return pl.pallas_call(
        functools.partial(_attn_kernel, has_local=True, L=L, span=span, group=n_heads // SWA_KV_HEADS),
        grid=(bsz, L // tq),
        in_specs=[pl.BlockSpec((1, tq, q_cols), lambda b, i: (b, i, 0)),
                  pl.BlockSpec((1, L, 2 * LANE), lambda b, i: (b, 0, kvb)),
                  pl.BlockSpec((1, Lc, 2 * LANE), lambda b, i: (b, 0, kvb)),
                  pl.BlockSpec((L, LANE), lambda b, i: (0, 0)),
                  pl.BlockSpec((L, LANE), lambda b, i: (0, 0)),
                  pl.BlockSpec((n_heads, LANE), lambda b, i: (0, 0))],
        out_specs=pl.BlockSpec((1, tq, q_cols), lambda b, i: (b, i, 0)),
        out_shape=jax.ShapeDtypeStruct((bsz, L, q_cols), F32),
        compiler_params=_params("parallel", "parallel"),
        name="windowed_attention",
    )(p_lat, p_lat, p_ctx, cos, sin, _sink_rows(sink))


def context_attention(p_ctx, sink):
    bsz, Lc, cols = p_ctx.shape
    q_cols = cols - 2 * LANE
    n_heads = q_cols // HEAD_DIM
    tq = min(Lc, SWA_BLOCK)
    kvb = q_cols // (2 * LANE)
    return pl.pallas_call(
        functools.partial(_attn_kernel, has_local=False, L=Lc, span=0, group=n_heads // SWA_KV_HEADS),
        grid=(bsz, Lc // tq),
        in_specs=[pl.BlockSpec((1, tq, q_cols), lambda b, i: (b, i, 0)),
                  pl.BlockSpec((1, Lc, 2 * LANE), lambda b, i: (b, 0, kvb)),
                  pl.BlockSpec((n_heads, LANE), lambda b, i: (0, 0))],
        out_specs=pl.BlockSpec((1, tq, q_cols), lambda b, i: (b, i, 0)),
        out_shape=jax.ShapeDtypeStruct((bsz, Lc, q_cols), F32),
        compiler_params=_params("parallel", "parallel"),
        name="context_attention",
    )(p_ctx, p_ctx, _sink_rows(sink))


def _rwkv_feat_kernel(p_ref, hp_ref, hn_ref, mu_ref, w0_ref, w2_ref, a0_ref, a2_ref, g2_ref, kk_ref, ka_ref,
                      rk_ref, e_ref, rvk_o, g_o, bonus_o, lw_o, kb_o, *, rw_w):
    p = p_ref[0]
    p_prev, p_next = _neighbours(p, *_halo_rows(hp_ref, hn_ref))
    mu = mu_ref[...]
    p = p * (1.0 - mu) + (p_prev + p_next) * (0.5 * mu)
    o1, o2, o3 = rw_w, 2 * rw_w, 3 * rw_w
    o4 = o3 + 2 * RW_DECAY_LORA
    o5 = o4 + 2 * RW_AAA_LORA
    r, k, v = p[:, :o1], p[:, o1:o2], p[:, o2:o3]
    wd, ad, gd = p[:, o3:o4], p[:, o4:o5], p[:, o5:]
    w_log = -_softplus(-(w0_ref[...] + _mm(jnp.tanh(wd), w2_ref[...]))) - 0.5
    lw = -jnp.exp(w_log)
    a = _sigmoid(a0_ref[...] + _mm(ad, a2_ref[...]))
    e = e_ref[...]
    kq = k * kk_ref[...]
    kkn = kq * lax.rsqrt(_head_sum(kq * kq, e) + 1e-6)
    rvk_o[0] = jnp.concatenate([r, v, kkn], axis=1).astype(BF16)
    g_o[0] = _mm(_sigmoid(gd), g2_ref[...])
    ka = ka_ref[...]
    rk = r * rk_ref[...]
    acc = jnp.zeros_like(r)
    for d in range(2):
        a_d = a[:, d * rw_w:(d + 1) * rw_w]
        k_d = k * (1.0 + (a_d - 1.0) * ka)
        lw_o[d, 0] = lw[:, d * rw_w:(d + 1) * rw_w]
        kb_o[d, 0] = jnp.concatenate([k_d, kkn * a_d], axis=1).astype(BF16)
        acc = acc + rk * k_d
    bonus_o[0] = _head_sum(acc, e) * v


def _head_indicator(w, head):
    hid = jnp.arange(w) // head
    return (hid[:, None] == hid[None, :]).astype(BF16)


def rwkv_features(p, mu, w0, w2, a0, a2, g2, k_k, k_a, r_k):
    bsz, T, cols = p.shape
    rw_w = g2.shape[1]
    tm = _row_tile(T, 256)
    w2_bd = jnp.zeros((2 * RW_DECAY_LORA, 2 * rw_w), F32)
    w2_bd = w2_bd.at[:RW_DECAY_LORA, :rw_w].set(w2[0]).at[RW_DECAY_LORA:, rw_w:].set(w2[1]).astype(BF16)
    a2_bd = jnp.zeros((2 * RW_AAA_LORA, 2 * rw_w), F32)
    a2_bd = a2_bd.at[:RW_AAA_LORA, :rw_w].set(a2[0]).at[RW_AAA_LORA:, rw_w:].set(a2[1]).astype(BF16)
    vec = lambda t: t.reshape(1, -1).astype(F32)
    full = lambda shape: pl.BlockSpec(shape, lambda b, i: (0,) * len(shape))
    tok = lambda n: pl.BlockSpec((1, tm, n * rw_w), lambda b, i: (b, i, 0))
    tok2 = lambda n: pl.BlockSpec((2, 1, tm, n * rw_w), lambda b, i: (0, b, i, 0))
    s1 = lambda n, t: jax.ShapeDtypeStruct((bsz, T, n * rw_w), t)
    s2 = lambda n, t: jax.ShapeDtypeStruct((2, bsz, T, n * rw_w), t)
    return pl.pallas_call(
        functools.partial(_rwkv_feat_kernel, rw_w=rw_w),
        grid=(bsz, T // tm),
        in_specs=_halo_specs(tm, T, cols) + [
            full((1, cols)), full((1, 2 * rw_w)), full(w2_bd.shape), full((1, 2 * rw_w)),
            full(a2_bd.shape), full(g2.shape), full((1, rw_w)), full((1, rw_w)), full((1, rw_w)),
            full((rw_w, rw_w))],
        out_specs=[tok(3), tok(1), tok(1), tok2(1), tok2(2)],
        out_shape=[s1(3, BF16), s1(1, F32), s1(1, F32), s2(1, F32), s2(2, BF16)],
        compiler_params=_params("parallel", "parallel"),
        name="rwkv_features",
    )(p, p, p, vec(mu), vec(w0), w2_bd, vec(a0), a2_bd, g2.astype(BF16), vec(k_k), vec(k_a), vec(r_k),
      _head_indicator(rw_w, RW_HEAD))


def _rwkv_scan_kernel(rvkc, lwc, kbc, rvkl, lwl, kbl, y_ref, s_ref, *, n_ctx, bb):
    d = pl.program_id(1)
    c = pl.program_id(2)
    C = CHUNK
    n_pairs = s_ref.shape[0] // bb

    @pl.when(c == 0)
    def _():
        s_ref[...] = jnp.zeros_like(s_ref)

    is_ctx = c < n_ctx
    pick = lambda a, b: jnp.where(is_ctx, a, b)
    rev = d == 1
    ti = lax.broadcasted_iota(jnp.int32, (C, C), 0)
    tj = lax.broadcasted_iota(jnp.int32, (C, C), 1)
    tri = jnp.where(jnp.where(rev, tj - ti, ti - tj) >= 0, 1.0, 0.0).astype(BF16)

    def features(bi):
        w = lwc.shape[-1]
        rvk = pick(rvkc[bi], rvkl[bi])
        r, v, kk = rvk[:, :w], rvk[:, w:2 * w], rvk[:, 2 * w:]
        lw = pick(lwc[0, bi], lwl[0, bi])
        kb = pick(kbc[0, bi], kbl[0, bi])
        kd, bd = kb[:, :w], kb[:, w:]
        linc = _mm_exact_lhs(tri, lw)
        ltot = jnp.sum(lw, axis=0, keepdims=True)
        g_inv = jnp.exp(-linc)
        g_end = jnp.exp(ltot - linc)
        return (kk * jnp.exp(linc - lw), r * jnp.exp(linc), kd * g_inv, bd * g_inv, kd * g_end, bd * g_end, v,
                jnp.exp(ltot))

    feats = [features(bi) for bi in range(bb)]

    n = 2 * C
    rid = lax.broadcasted_iota(jnp.int32, (n, n), 0)
    cid = lax.broadcasted_iota(jnp.int32, (n, n), 1)
    same = (rid >> 6) == (cid >> 6)
    diff = jnp.where(rev, cid - rid, rid - cid)
    strict = lambda m: jnp.where(same, jnp.where(diff > 0, m, 0.0), 0.0)
    incl = lambda m: jnp.where(same, jnp.where(diff >= 0, m, 0.0), 0.0)
    first = lax.broadcasted_iota(jnp.int32, (1, LANE), 1) < RW_HEAD

    def stack(x):
        return jnp.concatenate([jnp.where(first, x, 0.0), jnp.where(first, 0.0, x)], axis=0)

    prob = [(bi, slice(p * LANE, (p + 1) * LANE)) for bi in range(bb) for p in range(n_pairs)]
    qs = range(len(prob))
    stk = [[stack(t[:, sl]) for t in feats[bi][:7]] for bi, sl in prob]
    kks, rs, kis, bis, kes, bes, vs = ([stk[q][i] for q in qs] for i in range(7))
    m = [_mm_nt(jnp.concatenate([kks[q], rs[q]], axis=0), jnp.concatenate([kis[q], bis[q]], axis=0)) for q in qs]
    a_k = [strict(t[:n, :n]) for t in m]
    a_b = [strict(t[:n, n:]) for t in m]
    b_k = [incl(t[n:, :n]) for t in m]
    b_b = [incl(t[n:, n:]) for t in m]
    t_inv = _unit_tri_inverse(a_b, rid, cid, C)
    abv = [_mm(jnp.concatenate([a_k[q], b_k[q]], axis=0), vs[q]) for q in qs]
    wu = [_mm(t_inv[q], jnp.concatenate([kks[q], abv[q][:n]], axis=1)) for q in qs]
    w_s = [t[:, :LANE] for t in wu]
    u0_s = [t[:, LANE:] for t in wu]
    vt = [t.T for t in vs]
    u0_t = [t.T for t in u0_s]
    st = [s_ref[q] for q in qs]
    wr = [_mm_nt(jnp.concatenate([w_s[q], rs[q]], axis=0), st[q]) for q in qs]
    u_s = [wr[q][:n] + u0_s[q] for q in qs]
    u_t = [_mm_nt(st[q], w_s[q]) + u0_t[q] for q in qs]
    y_s = [wr[q][n:] + abv[q][n:] - _mm(b_b[q], u_s[q]) for q in qs]
    for q, (bi, sl) in enumerate(prob):
        s_ref[q] = st[q] * feats[bi][7][:, sl] + _mm(jnp.concatenate([vt[q], -u_t[q]], axis=1),
                                                    jnp.concatenate([kes[q], bes[q]], axis=0))
        y_ref[0, bi, :, sl] = (y_s[q][:C] + y_s[q][C:]).astype(BF16)


def rwkv_scan(f_ctx, f_lat):
    rvkc, lwc, kbc = f_ctx
    rvkl, lwl, kbl = f_lat
    bsz, Lc, w = lwc.shape[1:]
    L = lwl.shape[2]
    C = CHUNK
    assert Lc % C == 0 and L % C == 0 and w % LANE == 0
    nc, nl = Lc // C, L // C
    bb = _row_tile(bsz, RW_SCAN_BATCH)
    ctx_idx, lat_idx = _scan_index_maps(nc, nl)
    shared = lambda idx: pl.BlockSpec((bb, C, 3 * w), lambda b, d, c: (b, idx(d, c), 0))
    directed = lambda idx, n: pl.BlockSpec((1, bb, C, n * w), lambda b, d, c: (d, b, idx(d, c), 0))
    return pl.pallas_call(
        functools.partial(_rwkv_scan_kernel, n_ctx=nc, bb=bb),
        grid=(bsz // bb, 2, nc + nl),
        in_specs=[shared(ctx_idx), directed(ctx_idx, 1), directed(ctx_idx, 2),
                  shared(lat_idx), directed(lat_idx, 1), directed(lat_idx, 2)],
        out_specs=directed(lat_idx, 1),
        out_shape=jax.ShapeDtypeStruct((2, bsz, L, w), BF16),
        scratch_shapes=[pltpu.VMEM((bb * (w // LANE), LANE, LANE), F32)],
        compiler_params=_params("parallel", "parallel", "arbitrary"),
        name="rwkv_scan",
    )(rvkc, lwc, kbc, rvkl, lwl, kbl)


def _rwkv_head_out(y, bonus, g, gn_g, gn_b, e):
    inv_n = 1.0 / RW_HEAD
    mu = _head_sum(y, e) * inv_n
    yc = y - mu
    var = _head_sum(yc * yc, e) * inv_n
    return (yc * lax.rsqrt(var + RW_GN_EPS) * gn_g + gn_b + bonus) * g


def rwkv7_scan_lat(p_lat, p_ctx, mu, w0, w2, a0, a2, g2, k_k, k_a, r_k):
    args = (mu, w0, w2, a0, a2, g2, k_k, k_a, r_k)
    rvk_l, g_l, bonus_l, lw_l, kb_l = rwkv_features(p_lat, *args)
    rvk_c, _, _, lw_c, kb_c = rwkv_features(p_ctx, *args)
    return rwkv_scan((rvk_c, lw_c, kb_c), (rvk_l, lw_l, kb_l)), bonus_l, g_l


def _gdn_feat_kernel(p_ref, hp_ref, hn_ref, g_ref, cw_ref, al_ref, dt_ref, qkv_o, gb_o, *, dn_w):
    x = p_ref[0]
    x_prev, x_next = _neighbours(x, *_halo_rows(hp_ref, hn_ref))
    cw = cw_ref[...]
    z = x_prev * cw[0:1] + x * cw[1:2] + x_next * cw[2:3]
    z = z * _sigmoid(z)
    for h in range(dn_w // DN_HEAD):
        sq = slice(h * DN_HEAD, (h + 1) * DN_HEAD)
        sk = slice(dn_w + h * DN_HEAD, dn_w + (h + 1) * DN_HEAD)
        q = z[:, sq]
        k = z[:, sk]
        qn = q * (lax.rsqrt(jnp.sum(q * q, -1, keepdims=True) + 1e-6) * (DN_HEAD ** -0.5))
        qkv_o[0, :, sq] = qn.astype(BF16)
        qkv_o[0, :, sk] = (k * lax.rsqrt(jnp.sum(k * k, -1, keepdims=True) + 1e-6)).astype(BF16)
    qkv_o[0, :, 2 * dn_w:] = z[:, 2 * dn_w:].astype(BF16)
    g = g_ref[0]
    gb_o[0, :, :LANE] = -jnp.exp(al_ref[...]) * _softplus(g[:, :LANE] + dt_ref[...])
    gb_o[0, :, LANE:] = _sigmoid(g[:, LANE:])


def gdn_features(p, conv_w, A_log, dt_bias):
    bsz, T, cols = p.shape
    dn_w = conv_w.shape[1] // 3
    tm = _row_tile(T, 256)
    c3 = 3 * dn_w
    assert (4 * dn_w) % (2 * LANE) == 0
    gcol = (4 * dn_w) // (2 * LANE)
    pad = lambda t: jnp.pad(t.reshape(1, -1).astype(F32), ((0, 0), (0, LANE - t.size)))
    tok = lambda n: pl.BlockSpec((1, tm, n), lambda b, i: (b, i, 0))
    row = pl.BlockSpec((1, LANE), lambda b, i: (0, 0))
    return pl.pallas_call(
        functools.partial(_gdn_feat_kernel, dn_w=dn_w),
        grid=(bsz, T // tm),
        in_specs=_halo_specs(tm, T, c3) + [pl.BlockSpec((1, tm, 2 * LANE), lambda b, i: (b, i, gcol)),
                                          pl.BlockSpec((8, c3), lambda b, i: (0, 0)), row, row],
        out_specs=[tok(c3), tok(2 * LANE)],
        out_shape=[jax.ShapeDtypeStruct((bsz, T, c3), BF16), jax.ShapeDtypeStruct((bsz, T, 2 * LANE), F32)],
        compiler_params=_params("parallel", "parallel"),
        name="gdn_features",
    )(p, p, p, p, jnp.pad(conv_w, ((0, 8 - conv_w.shape[0]), (0, 0))), pad(A_log), pad(dt_bias))


def _gdn_scan_kernel(qkvc, gbc, qkvl, gbl, o_ref, s_ref, *, n_ctx, bb):
    d = pl.program_id(1)
    c = pl.program_id(2)
    C = CHUNK
    heads = s_ref.shape[0] // bb

    @pl.when(c == 0)
    def _():
        s_ref[...] = jnp.zeros_like(s_ref)

    is_ctx = c < n_ctx
    pick = lambda a, b, bi: jnp.where(is_ctx, a[bi], b[bi])
    rev = d == 1

    ti = lax.broadcasted_iota(jnp.int32, (C, C), 0)
    tj = lax.broadcasted_iota(jnp.int32, (C, C), 1)
    tri = jnp.where(jnp.where(rev, tj - ti, ti - tj) >= 0, 1.0, 0.0).astype(BF16)
    lane = lax.broadcasted_iota(jnp.int32, (1, LANE), 1)

    def col(x, h):
        return jnp.sum(jnp.where(lane == d * heads + h, x, 0.0), axis=1, keepdims=True)

    n = 2 * C
    rid = lax.broadcasted_iota(jnp.int32, (n, n), 0)
    cid = lax.broadcasted_iota(jnp.int32, (n, n), 1)
    same = (rid >> 6) == (cid >> 6)
    diff = jnp.where(rev, cid - rid, rid - cid)
    keep_incl = jnp.where(same, jnp.where(diff >= 0, 1.0, 0.0), 0.0)
    eye = jnp.where(rid == cid, 1.0, 0.0)

    hs = lambda h: slice(h * DN_HEAD, (h + 1) * DN_HEAD)

    prob = [(bi, p) for bi in range(bb) for p in range(heads // 2)]
    qs = range(len(prob))
    k_s, q_s, v_s, g_col, b_col, ge_col, g_end = [], [], [], [], [], [], []
    for bi in range(bb):
        w = heads * DN_HEAD
        qkv = pick(qkvc, qkvl, bi)
        q, k, v = qkv[:, :w], qkv[:, w:2 * w], qkv[:, 2 * w:]
        gb = pick(gbc, gbl, bi)
        gl, be = gb[:, :LANE], gb[:, LANE:]
        g_cum = _mm_exact_lhs(tri, gl)
        g_tot = jnp.sum(gl, axis=0, keepdims=True)
        for p in range(heads // 2):
            h0, h1 = 2 * p, 2 * p + 1
            stack = lambda x: jnp.concatenate([x[:, hs(h0)], x[:, hs(h1)]], axis=0)
            k_s.append(stack(k))
            q_s.append(stack(q))
            v_s.append(stack(v))
            g_col.append(jnp.concatenate([col(g_cum, h0), col(g_cum, h1)], axis=0))
            b_col.append(jnp.concatenate([col(be, h0), col(be, h1)], axis=0))
            gt = [col(g_tot, h0), col(g_tot, h1)]
            ge_col.append(jnp.concatenate([jnp.broadcast_to(t, (C, 1)) for t in gt], axis=0))
            g_end.append([jnp.exp(t) for t in gt])
    g_row = [jnp.broadcast_to(g_col[i], (n, n)).T for i in qs]
    decay = [keep_incl * jnp.exp(keep_incl * (g_col[i] - g_row[i])) for i in qs]
    kq = [_mm_nt(jnp.concatenate([k_s[i], q_s[i]], axis=0), k_s[i]) for i in qs]
    a = [(1.0 - eye) * b_col[i] * kq[i][:n] * decay[i] for i in qs]
    qk = [kq[i][n:] * decay[i] for i in qs]
    t_inv = _unit_tri_inverse(a, rid, cid, C)
    eg = [jnp.exp(g_col[i]) for i in qs]
    uw = [_mm(t_inv[i], jnp.concatenate([b_col[i] * v_s[i], (b_col[i] * eg[i]) * k_s[i]], axis=1)) for i in qs]
    u0 = [t[:, :DN_HEAD] for t in uw]
    w = [t[:, DN_HEAD:] for t in uw]
    q_dec = [q_s[i] * eg[i] for i in qs]
    k_dec_t = [_mm_nt(eye, k_s[i] * jnp.exp(ge_col[i] - g_col[i])) for i in qs]
    head_cols = lax.broadcasted_iota(jnp.int32, (1, n), 1) >> 6

    st = [[s_ref[2 * i + t] for t in range(2)] for i in qs]
    ws = [[_mm(jnp.concatenate([w[i][t * C:(t + 1) * C], q_dec[i][t * C:(t + 1) * C]], axis=0), st[i][t])
           for t in range(2)] for i in qs]
    u = [jnp.concatenate([u0[i][:C] - ws[i][0][:C], u0[i][C:] - ws[i][1][:C]], axis=0) for i in qs]
    o = [jnp.concatenate([ws[i][0][C:], ws[i][1][C:]], axis=0) + _mm(qk[i], u[i]) for i in qs]
    for i, (bi, p) in enumerate(prob):
        for t in range(2):
            kd = jnp.where(head_cols == t, k_dec_t[i], 0.0)
            s_ref[2 * i + t] = g_end[i][t] * st[i][t] + _mm(kd, u[i])
            o_ref[0, bi, :, hs(2 * p + t)] = o[i][t * C:(t + 1) * C].astype(BF16)


def gdn_scan(f_ctx, f_lat):
    qkvc, gbc = f_ctx
    qkvl, gbl = f_lat
    bsz, Lc, w3 = qkvc.shape
    L = qkvl.shape[1]
    C = CHUNK
    w = w3 // 3
    heads = w // DN_HEAD
    assert Lc % C == 0 and L % C == 0 and heads % 2 == 0
    nc, nl = Lc // C, L // C
    bb = _row_tile(bsz, DN_SCAN_BATCH)
    ctx_idx, lat_idx = _scan_index_maps(nc, nl)
    spec = lambda idx, n: pl.BlockSpec((bb, C, n), lambda b, d, c: (b, idx(d, c), 0))
    return pl.pallas_call(
        functools.partial(_gdn_scan_kernel, n_ctx=nc, bb=bb),
        grid=(bsz // bb, 2, nc + nl),
        in_specs=[spec(ctx_idx, w3), spec(ctx_idx, 2 * LANE), spec(lat_idx, w3), spec(lat_idx, 2 * LANE)],
        out_specs=pl.BlockSpec((1, bb, C, w), lambda b, d, c: (d, b, lat_idx(d, c), 0)),
        out_shape=jax.ShapeDtypeStruct((2, bsz, L, w), BF16),
        scratch_shapes=[pltpu.VMEM((bb * heads, DN_HEAD, DN_HEAD), F32)],
        compiler_params=_params("parallel", "parallel", "arbitrary"),
        name="gdn_scan",
    )(qkvc, gbc, qkvl, gbl)


def _gdn_head_out(o, z, norm_g):
    outs = []
    for h in range(o.shape[1] // DN_HEAD):
        sl = slice(h * DN_HEAD, (h + 1) * DN_HEAD)
        oh = o[:, sl]
        zh = z[:, sl]
        outs.append(oh * lax.rsqrt(jnp.mean(oh * oh, -1, keepdims=True) + 1e-6) * norm_g * (zh * _sigmoid(zh)))
    return jnp.concatenate(outs, axis=1)


def gdn_scan_lat(p_lat, p_ctx, conv_w, A_log, dt_bias):
    f_lat = gdn_features(p_lat, conv_w, A_log, dt_bias)
    f_ctx = gdn_features(p_ctx, conv_w, A_log, dt_bias)
    return gdn_scan(f_ctx, f_lat)


def _odd_out_ln_kernel(y_ref, bonus_ref, g_ref, gg_ref, gb_ref, e_ref, o_ref, z_ref, ng_ref, wa_ref, wb_ref,
                       x_ref, gate_ref, lg_ref, lb_ref, out_ref):
    both = lambda ref: ref[0, 0].astype(F32) + ref[1, 0].astype(F32)
    y_c = _rwkv_head_out(both(y_ref), bonus_ref[0], g_ref[0], gg_ref[...], gb_ref[...], e_ref[...])
    y_d = _gdn_head_out(both(o_ref), z_ref[0], ng_ref[...])
    y = _mm(y_c, wa_ref[...]) + _mm(y_d, wb_ref[...])
    z = DEEPNORM_ALPHA * x_ref[0] + gate_ref[0] * y
    out_ref[0] = _layer_norm_rows(z, lg_ref[...], lb_ref[...])


def odd_out_proj_ln(y, bonus, g, gn_g, gn_b, o, p_dn, norm_g, w, x, gate, ln_g, ln_b):
    bsz, L, d = x.shape
    rw_w, dn_w = y.shape[-1], o.shape[-1]
    tm = _row_tile(L, 256)
    z_block = 3
    two = lambda w_: pl.BlockSpec((2, 1, tm, w_), lambda b, i: (0, b, i, 0))
    tok = lambda w_: pl.BlockSpec((1, tm, w_), lambda b, i: (b, i, 0))
    row = lambda w_: pl.BlockSpec((1, w_), lambda b, i: (0, 0))
    full = lambda a: pl.BlockSpec(a.shape, lambda b, i: (0, 0))
    wa, wb = w[:rw_w], w[rw_w:]
    e = _head_indicator(rw_w, RW_HEAD)
    return pl.pallas_call(
        _odd_out_ln_kernel,
        grid=(bsz, L // tm),
        in_specs=[two(rw_w), tok(rw_w), tok(rw_w), row(rw_w), row(rw_w), full(e), two(dn_w),
                  pl.BlockSpec((1, tm, dn_w), lambda b, i: (b, i, z_block)), row(DN_HEAD), full(wa), full(wb),
                  tok(d), pl.BlockSpec((1, 1, d), lambda b, i: (b, 0, 0)), row(d), row(d)],
        out_specs=tok(d),
        out_shape=jax.ShapeDtypeStruct((bsz, L, d), F32),
        compiler_params=_params("parallel", "parallel"),
        name="odd_out_proj_ln",
    )(y, bonus, g, gn_g.reshape(1, rw_w), gn_b.reshape(1, rw_w), e, o, p_dn, norm_g.reshape(1, DN_HEAD), wa, wb,
      x, gate, ln_g.reshape(1, d), ln_b.reshape(1, d))


def _gdn_in_weight(w_dn, dn_w, heads):
    padw = ((0, 0), (0, LANE - 2 * heads))
    g = w_dn[:, 4 * dn_w:]
    return jnp.concatenate([w_dn[:, :4 * dn_w], jnp.pad(g[:, :2 * heads], padw), jnp.pad(g[:, 2 * heads:], padw)], 1)


def kernel(x, c, ctx, c_ctx, mod_w, mod_b, ln_g, ln_b, mlp_w1, mlp_w2, e_w_in, e_w_out, hy_conv, hy_ffn_w1,
           hy_ffn_b1, hy_ffn_w2, hy_ffn_b2, hy_sin_freq, hy_ffn_w3, hy_decay, hy_bias, attn_sink, o_w_in,
           o_w_out, rw_mu, rw_w0, rw_w2, rw_a0, rw_a2, rw_g2, rw_kk, rw_ka, rw_rk, rw_lnx_g, rw_lnx_b,
           dn_conv, dn_A_log, dn_dt_bias, dn_norm_g):
    bsz, L, d = x.shape
    assert mod_w.shape[0] == DEPTH == 2
    cc = jnp.concatenate([c, c_ctx[None, :]], axis=0)
    cc = jnp.pad(cc, ((0, (-cc.shape[0]) % 8), (0, 0)))

    def modulation(i):
        m = mod_vectors(cc, mod_w[i], mod_b[i])
        m_lat = m[:bsz].reshape(bsz, 1, N_MOD, d)
        m_ctx = jnp.broadcast_to(m[bsz].reshape(1, 1, N_MOD, d), (bsz, 1, N_MOD, d))
        return [m_lat[:, :, t] for t in range(N_MOD)], [m_ctx[:, :, t] for t in range(N_MOD)]

    (sh1, sc1, g1, sh2, sc2, g2), (csh1, csc1, cg1, csh2, csc2, cg2) = modulation(0)
    hy_cols = hy_conv.shape[2]
    w_in = e_w_in[0].astype(BF16)
    ws = [w_in[:, :hy_cols], w_in[:, hy_cols:]]
    w_out = e_w_out[0].astype(BF16)
    w1, w2 = mlp_w1[0].astype(BF16), mlp_w2[0].astype(BF16)
    hy = (hy_conv[0], hy_ffn_w1[0], hy_ffn_b1[0], hy_ffn_w2[0], hy_ffn_b2[0], hy_sin_freq[0], hy_ffn_w3[0],
          hy_decay[0], hy_bias[0])
    p_hy, p_qkv = mod_linear(x, sc1, sh1, ws)
    pc_hy, pc_qkv = mod_linear(ctx, csc1, csh1, ws)
    y_a = hyena_mixer(p_hy, *hy)
    y_b = windowed_attention(p_qkv, pc_qkv, attn_sink[0])
    yc_a = hyena_mixer(pc_hy, *hy)
    yc_b = context_attention(pc_qkv, attn_sink[0])
    x = out_proj_ln(y_a, y_b, w_out, x, g1, ln_g[0, 0], ln_b[0, 0])
    x = mlp_ln(x, sc2, sh2, g2, w1, w2, ln_g[0, 1], ln_b[0, 1])
    ctx = out_proj_ln(yc_a, yc_b, w_out, ctx, cg1, ln_g[0, 0], ln_b[0, 0])
    ctx = mlp_ln(ctx, csc2, csh2, cg2, w1, w2, ln_g[0, 1], ln_b[0, 1])

    (sh1, sc1, g1, sh2, sc2, g2), (csh1, csc1, _, _, _, _) = modulation(1)
    rw_cols = rw_mu.shape[1]
    dn_w = dn_conv.shape[2] // 3
    w_in = o_w_in[0]
    ws = [w_in[:, :rw_cols].astype(BF16), _gdn_in_weight(w_in[:, rw_cols:], dn_w, dn_w // DN_HEAD).astype(BF16)]
    w_out = o_w_out[0].astype(BF16)
    w1, w2 = mlp_w1[1].astype(BF16), mlp_w2[1].astype(BF16)
    p_rw, p_dn = mod_linear(x, sc1, sh1, ws)
    pc_rw, pc_dn = mod_linear(ctx, csc1, csh1, ws)
    y_rw, bonus, g_rw = rwkv7_scan_lat(p_rw, pc_rw, rw_mu[0], rw_w0[0], rw_w2[0], rw_a0[0], rw_a2[0], rw_g2[0],
                                       rw_kk[0], rw_ka[0], rw_rk[0])
    o_dn = gdn_scan_lat(p_dn, pc_dn, dn_conv[0], dn_A_log[0], dn_dt_bias[0])
    x = odd_out_proj_ln(y_rw, bonus, g_rw, rw_lnx_g[0], rw_lnx_b[0], o_dn, p_dn, dn_norm_g[0], w_out, x, g1,
                        ln_g[1, 0], ln_b[1, 0])
    x = mlp_ln(x, sc2, sh2, g2, w1, w2, ln_g[1, 1], ln_b[1, 1])
    return x
```

```python
import functools
import math

import jax
import jax.numpy as jnp
from jax import lax
from jax.experimental import pallas as pl
from jax.experimental.pallas import tpu as pltpu

F32 = jnp.float32
BF16 = jnp.bfloat16

DEPTH = 2
N_MOD = 6
DEEPNORM_ALPHA = (2.0 * DEPTH) ** 0.25
LN_EPS = 1e-5

HY_BANDS = 16
HEAD_DIM = 64
SWA_KV_HEADS = 2
WINDOW = 128
SWA_BLOCK = 128
SWA_QUERY_TILE = 128
GRID_W = 64
ROPE_BASE = 10000.0

RW_HEAD = 64
RW_DECAY_LORA = 64
RW_AAA_LORA = 64
RW_GN_EPS = 64e-5

DN_HEAD = 128
CHUNK = 64
RW_SCAN_BATCH = 4
DN_SCAN_BATCH = 8

LANE = 128
MXU_WIDTH = 256
VMEM_LIMIT = 56 * 1024 * 1024


def _row_tile(n, cap):
    t = min(n, cap)
    while n % t:
        t //= 2
    return t


def _params(*sem):
    return pltpu.CompilerParams(dimension_semantics=sem, vmem_limit_bytes=VMEM_LIMIT)


def _mm(a, b):
    return jnp.dot(a.astype(BF16), b.astype(BF16), preferred_element_type=F32)


def _mm_nt(a, b):
    return lax.dot_general(a.astype(BF16), b.astype(BF16), (((1,), (1,)), ((), ())), preferred_element_type=F32)


def _split2(x):
    hi = x.astype(BF16)
    return hi, (x - hi.astype(F32)).astype(BF16)


def _split3(x):
    hi = x.astype(BF16)
    r = x - hi.astype(F32)
    mid = r.astype(BF16)
    return hi, mid, (r - mid.astype(F32)).astype(BF16)


def _mm_exact_lhs(a_bf16, x):
    return sum(jnp.dot(a_bf16, t, preferred_element_type=F32) for t in _split3(x))


def _head_sum(x, e_bf16):
    return sum(jnp.dot(t, e_bf16, preferred_element_type=F32) for t in _split2(x))


def _mm_f32(a, b):
    ah, al = _split2(a)
    bh, bl = _split2(b)
    dot = lambda x, y: jnp.dot(x, y, preferred_element_type=F32)
    return dot(ah, bh) + dot(ah, bl) + dot(al, bh)


def _sigmoid(x):
    return 1.0 / (1.0 + jnp.exp(-x))


def _softplus(x):
    return jnp.maximum(x, 0.0) + jnp.log(1.0 + jnp.exp(-jnp.abs(x)))


def _neighbours(x, prev_row, next_row):
    n = x.shape[0]
    rows = lax.broadcasted_iota(jnp.int32, (n, 1), 0)
    x_prev = jnp.where(rows == 0, prev_row, pltpu.roll(x, 1, axis=0))
    x_next = jnp.where(rows == n - 1, next_row, pltpu.roll(x, n - 1, axis=0))
    return x_prev, x_next


def _halo_rows(hp_ref, hn_ref):
    i = pl.program_id(1)
    nt = pl.num_programs(1)
    return jnp.where(i > 0, hp_ref[0][7:8, :], 0.0), jnp.where(i < nt - 1, hn_ref[0][0:1, :], 0.0)


def _halo_specs(tm, T, cols, col_block=0):
    hb = tm // 8
    return [pl.BlockSpec((1, tm, cols), lambda b, i: (b, i, col_block)),
            pl.BlockSpec((1, 8, cols), lambda b, i: (b, jnp.maximum(i * hb - 1, 0), col_block)),
            pl.BlockSpec((1, 8, cols), lambda b, i: (b, jnp.minimum((i + 1) * hb, T // 8 - 1), col_block))]


def _unit_tri_inverse(a_list, rid, cid, blk):
    eye = jnp.where(rid == cid, 1.0, 0.0)
    pair = (rid >> 1) == (cid >> 1)
    d_list = [eye - jnp.where(pair, a, 0.0) for a in a_list]
    s, sh = 2, 1
    while s < blk:
        join = jnp.where((rid >> (sh + 1)) == (cid >> (sh + 1)), jnp.where((rid >> sh) != (cid >> sh), 1.0, 0.0), 0.0)
        dq = [_mm(d, a * join) for d, a in zip(d_list, a_list)]
        d_list = [d - _mm(t, d) for t, d in zip(dq, d_list)]
        s, sh = 2 * s, sh + 1
    return d_list


def _scan_index_maps(nc, nl):
    def ctx_idx(d, c):
        cc = jnp.minimum(c, nc - 1)
        return jnp.where(d == 1, nc - 1 - cc, cc)

    def lat_idx(d, c):
        cl = jnp.maximum(c - nc, 0)
        return jnp.where(d == 1, nl - 1 - cl, cl)

    return ctx_idx, lat_idx


def _mod_kernel(c_ref, w_ref, b_ref, o_ref):
    c = c_ref[...]
    o_ref[...] = _mm(c * _sigmoid(c), w_ref[...]) + b_ref[...]


def mod_vectors(cc, w, b):
    r, d = cc.shape
    n = w.shape[1]
    tn = _row_tile(n, 1024)
    return pl.pallas_call(
        _mod_kernel,
        grid=(n // tn,),
        in_specs=[pl.BlockSpec((r, d), lambda j: (0, 0)),
                  pl.BlockSpec((d, tn), lambda j: (0, j)),
                  pl.BlockSpec((1, tn), lambda j: (0, j))],
        out_specs=pl.BlockSpec((r, tn), lambda j: (0, j)),
        out_shape=jax.ShapeDtypeStruct((r, n), F32),
        name="mod_vectors",
    )(cc, w, b.reshape(1, n))


def _mod_linear_kernel(x_ref, sc_ref, sh_ref, *refs):
    n = len(refs) // 2
    h = (x_ref[0] * (1.0 + sc_ref[0]) + sh_ref[0]).astype(BF16)
    for w_ref, o_ref in zip(refs[:n], refs[n:]):
        o_ref[0] = jnp.dot(h, w_ref[...], preferred_element_type=F32)


def mod_linear(x, sc, sh, ws):
    bsz, L, d = x.shape
    tm = _row_tile(L, 512)
    vec = pl.BlockSpec((1, 1, d), lambda b, i: (b, 0, 0))
    return pl.pallas_call(
        _mod_linear_kernel,
        grid=(bsz, L // tm),
        in_specs=[pl.BlockSpec((1, tm, d), lambda b, i: (b, i, 0)), vec, vec]
        + [pl.BlockSpec(w.shape, lambda b, i: (0, 0)) for w in ws],
        out_specs=[pl.BlockSpec((1, tm, w.shape[1]), lambda b, i: (b, i, 0)) for w in ws],
        out_shape=[jax.ShapeDtypeStruct((bsz, L, w.shape[1]), F32) for w in ws],
        compiler_params=_params("parallel", "parallel"),
        name="mod_linear",
    )(x, sc, sh, *ws)


def _layer_norm_rows(z, g, b):
    mu = jnp.mean(z, -1, keepdims=True)
    zc = z - mu
    var = jnp.mean(zc * zc, -1, keepdims=True)
    return zc * lax.rsqrt(var + LN_EPS) * g + b


def _mlp_ln_rows(x, sc, sh, g, w1_ref, w2_ref, ln_g, ln_b, ff_tile):
    h = (x * (1.0 + sc) + sh).astype(BF16)
    d_ff = w1_ref.shape[1]
    acc = jnp.zeros(x.shape, F32)
    for c in range(d_ff // ff_tile):
        a = jnp.dot(h, w1_ref[:, c * ff_tile:(c + 1) * ff_tile], preferred_element_type=F32)
        a = jnp.maximum(a, 0.0)
        a = (a * a).astype(BF16)
        acc = acc + jnp.dot(a, w2_ref[c * ff_tile:(c + 1) * ff_tile, :], preferred_element_type=F32)
    return _layer_norm_rows(DEEPNORM_ALPHA * x + g * acc, ln_g, ln_b)


def _mlp_ln_kernel(x_ref, sc_ref, sh_ref, g_ref, w1_ref, w2_ref, lg_ref, lb_ref, o_ref, *, ff_tile):
    o_ref[0] = _mlp_ln_rows(x_ref[0], sc_ref[0], sh_ref[0], g_ref[0], w1_ref, w2_ref, lg_ref[...], lb_ref[...],
                            ff_tile)


def mlp_ln(x, sc, sh, gate, w1, w2, ln_g, ln_b):
    bsz, L, d = x.shape
    d_ff = w1.shape[1]
    tm = _row_tile(L, 512)
    ff_tile = _row_tile(d_ff, 1024)
    once = dict(pipeline_mode=pl.Buffered(1))
    vec = pl.BlockSpec((1, 1, d), lambda b, i: (b, 0, 0))
    row = pl.BlockSpec((1, d), lambda b, i: (0, 0))
    return pl.pallas_call(
        functools.partial(_mlp_ln_kernel, ff_tile=ff_tile),
        grid=(bsz, L // tm),
        in_specs=[pl.BlockSpec((1, tm, d), lambda b, i: (b, i, 0)), vec, vec, vec,
                  pl.BlockSpec((d, d_ff), lambda b, i: (0, 0), **once),
                  pl.BlockSpec((d_ff, d), lambda b, i: (0, 0), **once), row, row],
        out_specs=pl.BlockSpec((1, tm, d), lambda b, i: (b, i, 0)),
        out_shape=jax.ShapeDtypeStruct((bsz, L, d), F32),
        compiler_params=_params("parallel", "parallel"),
        name="mlp_ln",
    )(x, sc, sh, gate, w1, w2, ln_g.reshape(1, d), ln_b.reshape(1, d))


def _out_mlp_ln_kernel(ya_ref, yb_ref, wa_ref, wb_ref, x_ref, g1_ref, lg1_ref, lb1_ref, sc_ref, sh_ref, g2_ref,
                       w1_ref, w2_ref, lg2_ref, lb2_ref, o_ref, *, ff_tile):
    y = _mm(ya_ref[0], wa_ref[...]) + _mm(yb_ref[0], wb_ref[...])
    x1 = _layer_norm_rows(DEEPNORM_ALPHA * x_ref[0] + g1_ref[0] * y, lg1_ref[...], lb1_ref[...])
    o_ref[0] = _mlp_ln_rows(x1, sc_ref[0], sh_ref[0], g2_ref[0], w1_ref, w2_ref, lg2_ref[...], lb2_ref[...], ff_tile)


def out_mlp_ln(ya, yb, w_out, x, g1, ln1_g, ln1_b, sc, sh, g2, w1, w2, ln2_g, ln2_b):
    bsz, L, d = x.shape
    ka, kb = ya.shape[-1], yb.shape[-1]
    d_ff = w1.shape[1]
    tm = _row_tile(L, 512)
    once = dict(pipeline_mode=pl.Buffered(1))
    tok = lambda n: pl.BlockSpec((1, tm, n), lambda b, i: (b, i, 0))
    vec = pl.BlockSpec((1, 1, d), lambda b, i: (b, 0, 0))
    row = pl.BlockSpec((1, d), lambda b, i: (0, 0))
    const = lambda shape: pl.BlockSpec(shape, lambda b, i: (0, 0), **once)
    r = lambda t: t.reshape(1, d)
    return pl.pallas_call(
        functools.partial(_out_mlp_ln_kernel, ff_tile=_row_tile(d_ff, 1024)),
        grid=(bsz, L // tm),
        in_specs=[tok(ka), tok(kb), const((ka, d)), const((kb, d)), tok(d), vec, row, row, vec, vec, vec,
                  const((d, d_ff)), const((d_ff, d)), row, row],
        out_specs=tok(d),
        out_shape=jax.ShapeDtypeStruct((bsz, L, d), F32),
        compiler_params=_params("parallel", "parallel"),
        name="out_mlp_ln",
    )(ya, yb, w_out[:ka], w_out[ka:], x, g1, r(ln1_g), r(ln1_b), sc, sh, g2, w1, w2, r(ln2_g), r(ln2_b))


def dft_tables(T):
    n = 2 * T
    f = jnp.arange(T, dtype=jnp.int32)
    ang = ((f[:, None] * f[None, :]) % n).astype(F32) * (2.0 * math.pi / n)
    return jnp.cos(ang).astype(BF16), (-jnp.sin(ang)).astype(BF16)


def hyena_pos_features(T):
    t = jnp.arange(T, dtype=F32)
    t_norm = t / max(T - 1, 1)
    bands = jnp.linspace(1e-4, HY_BANDS - 1, HY_BANDS, dtype=F32)
    ang = 2.0 * math.pi * t[:, None] * bands[None, :] / T
    pe = jnp.concatenate([t_norm[:, None], jnp.cos(ang), -jnp.sin(ang)], axis=-1)
    return jnp.pad(pe, ((0, 0), (0, (-pe.shape[1]) % 8)))


def _hy_filter_kernel(pe_ref, w1_ref, b1_ref, w2_ref, b2_ref, fr_ref, w3f_ref, w3b_ref, df_ref, db_ref,
                      cm_ref, sm_ref, kr_o, ki_o, kn_o):
    T = pe_ref.shape[0]
    pe = pe_ref[...]
    fr = fr_ref[...]
    h = jnp.sin(fr * (_mm_f32(pe, w1_ref[...]) + b1_ref[...]))
    h = jnp.sin(fr * (_mm_f32(h, w2_ref[...]) + b2_ref[...]))
    t_norm = pe[:, 0:1]
    h_f = _mm_f32(h, w3f_ref[...]) * jnp.exp(-t_norm * jnp.abs(df_ref[...]))
    h_b = _mm_f32(h, w3b_ref[...]) * jnp.exp(-t_norm * jnp.abs(db_ref[...]))
    row = lax.broadcasted_iota(jnp.int32, (T, 1), 0)
    h_b = jnp.where(row == 0, 0.0, h_b)
    scale = jnp.where(row == 0, 1.0, 2.0) * (1.0 / (2 * T))
    kr_o[...] = _mm(cm_ref[...], h_f + h_b) * scale
    ki_o[...] = _mm(sm_ref[...], h_f - h_b) * scale
    sign = jnp.where((row & 1) == 0, 1.0, -1.0)
    kn = jnp.sum(sign * (h_f + h_b), axis=0, keepdims=True) * (1.0 / (2 * T))
    kn_o[...] = jnp.broadcast_to(kn, kn_o.shape)


def hyena_spectrum(T, cm, sm, w1, b1, w2, b2, freq, w3, decay):
    hy_w = w3.shape[1] // 2
    ct = MXU_WIDTH
    nj = hy_w // ct
    pe = hyena_pos_features(T)
    w1p = jnp.pad(w1, ((0, pe.shape[1] - w1.shape[0]), (0, 0)))
    row = lambda t: t.reshape(1, -1)
    full = lambda a: pl.BlockSpec(a.shape, lambda j: (0,) * a.ndim)
    ins = (pe, w1p, row(b1), w2, row(b2), row(freq))
    return pl.pallas_call(
        _hy_filter_kernel,
        grid=(nj,),
        in_specs=[full(a) for a in ins] + [
            pl.BlockSpec((w3.shape[0], ct), lambda j: (0, j)), pl.BlockSpec((w3.shape[0], ct), lambda j: (0, nj + j)),
            pl.BlockSpec((1, ct), lambda j: (0, j)), pl.BlockSpec((1, ct), lambda j: (0, nj + j)),
            pl.BlockSpec((T, T), lambda j: (0, 0)), pl.BlockSpec((T, T), lambda j: (0, 0))],
        out_specs=[pl.BlockSpec((T, ct), lambda j: (0, j)), pl.BlockSpec((T, ct), lambda j: (0, j)),
                   pl.BlockSpec((8, ct), lambda j: (0, j))],
        out_shape=[jax.ShapeDtypeStruct((T, hy_w), F32), jax.ShapeDtypeStruct((T, hy_w), F32),
                   jax.ShapeDtypeStruct((8, hy_w), F32)],
        compiler_params=_params("arbitrary"),
        name="hyena_spectrum",
    )(*ins, w3, w3, row(decay), row(decay), cm, sm)


def _hy_conv_kernel(x0_ref, x1_ref, xv_ref, c0_ref, c1_ref, cv_ref, bias_ref, kr_ref, ki_ref, kn_ref,
                    cm_ref, sm_ref, o_ref):
    T = x0_ref.shape[1]
    row = lax.broadcasted_iota(jnp.int32, (T, 1), 0)
    zero = jnp.zeros((1, x0_ref.shape[2]), F32)

    def short_conv(x_ref, c_ref):
        x = x_ref[0]
        c = c_ref[...]
        x_prev, x_next = _neighbours(x, zero, zero)
        return x_prev * c[0:1] + x * c[1:2] + x_next * c[2:3]

    u = short_conv(x1_ref, c1_ref) * short_conv(xv_ref, cv_ref)
    ub = u.astype(BF16)
    sign = jnp.where((row & 1) == 0, 1.0, -1.0)
    un = jnp.sum(sign * u, axis=0, keepdims=True)
    y = sign * (un * kn_ref[0:1, :]) + u * bias_ref[...]
    ft = min(T, 512)
    yr, yi = [], []
    for f0 in range(0, T, ft):
        fs = slice(f0, f0 + ft)
        ur = jnp.dot(cm_ref[fs, :], ub, preferred_element_type=F32)
        ui = jnp.dot(sm_ref[fs, :], ub, preferred_element_type=F32)
        kr = kr_ref[fs, :]
        ki = ki_ref[fs, :]
        yr.append((ur * kr - ui * ki).astype(BF16))
        yi.append((ur * ki + ui * kr).astype(BF16))
    y = y + jnp.dot(cm_ref[...], jnp.concatenate(yr, axis=0), preferred_element_type=F32)
    y = y + jnp.dot(sm_ref[...], jnp.concatenate(yi, axis=0), preferred_element_type=F32)
    o_ref[0] = short_conv(x0_ref, c0_ref) * y


def hyena_mixer(p, conv_w, w1, b1, w2, b2, freq, w3, decay, bias):
    bsz, T, _ = p.shape
    hy_w = bias.shape[0]
    ct = MXU_WIDTH
    nj = hy_w // ct
    cm, sm = dft_tables(T)
    kr, ki, kn = hyena_spectrum(T, cm, sm, w1, b1, w2, b2, freq, w3, decay)
    cw = jnp.pad(conv_w, ((0, 8 - conv_w.shape[0]), (0, 0)))
    once = dict(pipeline_mode=pl.Buffered(1))
    xs = lambda part: pl.BlockSpec((1, T, ct), lambda j, b: (b, 0, part * nj + j))
    cs = lambda part: pl.BlockSpec((8, ct), lambda j, b: (0, part * nj + j))
    return pl.pallas_call(
        _hy_conv_kernel,
        grid=(nj, bsz),
        in_specs=[xs(0), xs(1), xs(2), cs(0), cs(1), cs(2),
                  pl.BlockSpec((1, ct), lambda j, b: (0, j)),
                  pl.BlockSpec((T, ct), lambda j, b: (0, j), **once),
                  pl.BlockSpec((T, ct), lambda j, b: (0, j), **once),
                  pl.BlockSpec((8, ct), lambda j, b: (0, j)),
                  pl.BlockSpec((T, T), lambda j, b: (0, 0), **once),
                  pl.BlockSpec((T, T), lambda j, b: (0, 0), **once)],
        out_specs=pl.BlockSpec((1, T, ct), lambda j, b: (b, 0, j)),
        out_shape=jax.ShapeDtypeStruct((bsz, T, hy_w), F32),
        compiler_params=_params("parallel", "parallel"),
        name="hyena_conv",
    )(p, p, p, cw, cw, cw, bias.reshape(1, hy_w), kr, ki, kn, cm, sm)


def rope_tables(L):
    quarter = HEAD_DIM // 4
    pos = jnp.arange(L, dtype=jnp.int32)
    inv_freq = ROPE_BASE ** (-jnp.arange(quarter, dtype=F32) / quarter)
    a_row = (pos // GRID_W).astype(F32)[:, None] * inv_freq[None, :]
    a_col = (pos % GRID_W).astype(F32)[:, None] * inv_freq[None, :]
    ang = jnp.concatenate([a_row, a_row, a_col, a_col], axis=-1)
    sgn = jnp.tile(jnp.concatenate([-jnp.ones(quarter, F32), jnp.ones(quarter, F32)]), 2)
    return jnp.tile(jnp.cos(ang), (1, 2)), jnp.tile(jnp.sin(ang) * sgn, (1, 2))


def _rope(t, cos, sin, first_quarter):
    w = t.shape[1]
    partner = jnp.where(first_quarter, pltpu.roll(t, w - 16, axis=1), pltpu.roll(t, 16, axis=1))
    return t * cos + partner * sin


def _attn_kernel(*refs, has_local, L, span, group):
    if has_local:
        q_ref, kv_ref, kvc_ref, cos_ref, sin_ref, sink_ref, o_ref = refs
    else:
        q_ref, kvc_ref, sink_ref, o_ref = refs
    i = pl.program_id(1)
    tq = q_ref.shape[1]
    lane = lax.broadcasted_iota(jnp.int32, (1, LANE), 1)
    low = lane < HEAD_DIM
    q = q_ref[0]
    n_heads = q.shape[1] // HEAD_DIM
    kvc = kvc_ref[0]
    kc, vc = kvc[:, :LANE], kvc[:, LANE:]
    if has_local:
        fq = (lane & 31) < 16
        q0 = pl.multiple_of(i * tq, tq)
        cos_q = cos_ref[pl.ds(q0, tq), :]
        sin_q = sin_ref[pl.ds(q0, tq), :]
        k0 = pl.multiple_of(jnp.clip(i * tq - WINDOW, 0, L - span), SWA_BLOCK)
        kv = kv_ref[0, pl.ds(k0, span), :]
        k = _rope(kv[:, :LANE], cos_ref[pl.ds(k0, span), :], sin_ref[pl.ds(k0, span), :], fq)
        v = kv[:, LANE:]
        qpos = q0 + lax.broadcasted_iota(jnp.int32, (tq, span), 0)
        kpos = k0 + lax.broadcasted_iota(jnp.int32, (tq, span), 1)
        ok1 = jnp.abs(qpos - kpos) <= WINDOW
        ok = jnp.concatenate([ok1, ok1], axis=0)
    scale = HEAD_DIM ** -0.5

    def dup(x, g):
        sw = pltpu.roll(x, HEAD_DIM, axis=1)
        return jnp.where(low, x, sw) if g == 0 else jnp.where(low, sw, x)

    def with_ones(x, g):
        return jnp.where(low, x if g == 0 else pltpu.roll(x, HEAD_DIM, axis=1), 1.0)

    js = range(n_heads // 2)
    kvg = [(2 * j) // group for j in js]
    qs, sk = [], []
    for j in js:
        q2 = q[:, j * LANE:(j + 1) * LANE]
        if has_local:
            q2 = _rope(q2, cos_q, sin_q, fq)
        q2 = q2 * scale
        qs.append(jnp.concatenate([jnp.where(low, q2, 0.0), jnp.where(low, 0.0, q2)], axis=0))
        sk.append(jnp.concatenate([jnp.broadcast_to(sink_ref[h:h + 1, 0:1], (tq, 1)) for h in (2 * j, 2 * j + 1)],
                                  axis=0))
    kc_g = [dup(kc, g) for g in range(SWA_KV_HEADS)]
    vc_g = [with_ones(vc, g) for g in range(SWA_KV_HEADS)]
    s_ctx = [_mm_nt(qs[j], kc_g[kvg[j]]) for j in js]
    m = [jnp.maximum(jnp.max(s_ctx[j], -1, keepdims=True), sk[j]) for j in js]
    if has_local:
        k_g = [dup(k, g) for g in range(SWA_KV_HEADS)]
        v_g = [with_ones(v, g) for g in range(SWA_KV_HEADS)]
        s_loc = [jnp.where(ok, _mm_nt(qs[j], k_g[kvg[j]]), -jnp.inf) for j in js]
        m = [jnp.maximum(m[j], jnp.max(s_loc[j], -1, keepdims=True)) for j in js]
    acc = [_mm(jnp.exp(s_ctx[j] - m[j]), vc_g[kvg[j]]) for j in js]
    if has_local:
        acc = [acc[j] + _mm(jnp.exp(s_loc[j] - m[j]), v_g[kvg[j]]) for j in js]
    den = [pltpu.roll(acc[j], HEAD_DIM, axis=1) + jnp.exp(sk[j] - m[j]) for j in js]
    for j in js:
        o = acc[j] / den[j]
        o_ref[0, :, j * LANE:(j + 1) * LANE] = jnp.where(low, o[:tq], pltpu.roll(o[tq:], HEAD_DIM, axis=1))


def _sink_rows(sink):
    return jnp.broadcast_to(sink.astype(F32)[:, None], (sink.shape[0], LANE))


def windowed_attention(p_lat, p_ctx, sink):
    bsz, L, cols = p_lat.shape
    Lc = p_ctx.shape[1]
    q_cols = cols - 2 * LANE
    n_heads = q_cols // HEAD_DIM
    tq = _row_tile(L, SWA_QUERY_TILE)
    span = min(L, tq + 2 * WINDOW)
    assert q_cols % (2 * LANE) == 0 and tq % SWA_BLOCK == 0
    kvb = q_cols // (2 * LANE)
    cos, sin = rope_tables(L)
    return pl.pallas_call(
        functools.partial(_attn_kernel, has_local=True, L=L, span=span, group=n_heads // SWA_KV_HEADS),
        grid=(bsz, L // tq),
        in_specs=[pl.BlockSpec((1, tq, q_cols), lambda b, i: (b, i, 0)),
                  pl.BlockSpec((1, L, 2 * LANE), lambda b, i: (b, 0, kvb)),
                  pl.BlockSpec((1, Lc, 2 * LANE), lambda b, i: (b, 0, kvb)),
                  pl.BlockSpec((L, LANE), lambda b, i: (0, 0)),
                  pl.BlockSpec((L, LANE), lambda b, i: (0, 0)),
                  pl.BlockSpec((n_heads, LANE), lambda b, i: (0, 0))],
        out_specs=pl.BlockSpec((1, tq, q_cols), lambda b, i: (b, i, 0)),
        out_shape=jax.ShapeDtypeStruct((bsz, L, q_cols), F32),
        compiler_params=_params("parallel", "parallel"),
        name="windowed_attention",
    )(p_lat, p_lat, p_ctx, cos, sin, _sink_rows(sink))


def context_attention(p_ctx, sink):
    bsz, Lc, cols = p_ctx.shape
    q_cols = cols - 2 * LANE
    n_heads = q_cols // HEAD_DIM
    tq = min(Lc, SWA_BLOCK)
    kvb = q_cols // (2 * LANE)
    return pl.pallas_call(
        functools.partial(_attn_kernel, has_local=False, L=Lc, span=0, group=n_heads // SWA_KV_HEADS),
        grid=(bsz, Lc // tq),
        in_specs=[pl.BlockSpec((1, tq, q_cols), lambda b, i: (b, i, 0)),
                  pl.BlockSpec((1, Lc, 2 * LANE), lambda b, i: (b, 0, kvb)),
                  pl.BlockSpec((n_heads, LANE), lambda b, i: (0, 0))],
        out_specs=pl.BlockSpec((1, tq, q_cols), lambda b, i: (b, i, 0)),
        out_shape=jax.ShapeDtypeStruct((bsz, Lc, q_cols), F32),
        compiler_params=_params("parallel", "parallel"),
        name="context_attention",
    )(p_ctx, p_ctx, _sink_rows(sink))


def _rwkv_feat_kernel(p_ref, hp_ref, hn_ref, mu_ref, w0_ref, w2_ref, a0_ref, a2_ref, g2_ref, kk_ref, ka_ref,
                      rk_ref, e_ref, rvk_o, g_o, bonus_o, lw_o, kb_o, *, rw_w):
    p = p_ref[0]
    p_prev, p_next = _neighbours(p, *_halo_rows(hp_ref, hn_ref))
    mu = mu_ref[...]
    p = p * (1.0 - mu) + (p_prev + p_next) * (0.5 * mu)
    o1, o2, o3 = rw_w, 2 * rw_w, 3 * rw_w
    o4 = o3 + 2 * RW_DECAY_LORA
    o5 = o4 + 2 * RW_AAA_LORA
    r, k, v = p[:, :o1], p[:, o1:o2], p[:, o2:o3]
    wd, ad, gd = p[:, o3:o4], p[:, o4:o5], p[:, o5:]
    w_log = -_softplus(-(w0_ref[...] + _mm(jnp.tanh(wd), w2_ref[...]))) - 0.5
    lw = -jnp.exp(w_log)
    a = _sigmoid(a0_ref[...] + _mm(ad, a2_ref[...]))
    e = e_ref[...]
    kq = k * kk_ref[...]
    kkn = kq * lax.rsqrt(_head_sum(kq * kq, e) + 1e-6)
    rvk_o[0] = jnp.concatenate([r, v, kkn], axis=1).astype(BF16)
    g_o[0] = _mm(_sigmoid(gd), g2_ref[...])
    ka = ka_ref[...]
    rk = r * rk_ref[...]
    acc = jnp.zeros_like(r)
    for d in range(2):
        a_d = a[:, d * rw_w:(d + 1) * rw_w]
        k_d = k * (1.0 + (a_d - 1.0) * ka)
        lw_o[d, 0] = lw[:, d * rw_w:(d + 1) * rw_w]
        kb_o[d, 0] = jnp.concatenate([k_d, kkn * a_d], axis=1).astype(BF16)
        acc = acc + rk * k_d
    bonus_o[0] = _head_sum(acc, e) * v


def _head_indicator(w, head):
    hid = jnp.arange(w) // head
    return (hid[:, None] == hid[None, :]).astype(BF16)


def rwkv_features(p, mu, w0, w2, a0, a2, g2, k_k, k_a, r_k):
    bsz, T, cols = p.shape
    rw_w = g2.shape[1]
    tm = _row_tile(T, 256)
    w2_bd = jnp.zeros((2 * RW_DECAY_LORA, 2 * rw_w), F32)
    w2_bd = w2_bd.at[:RW_DECAY_LORA, :rw_w].set(w2[0]).at[RW_DECAY_LORA:, rw_w:].set(w2[1]).astype(BF16)
    a2_bd = jnp.zeros((2 * RW_AAA_LORA, 2 * rw_w), F32)
    a2_bd = a2_bd.at[:RW_AAA_LORA, :rw_w].set(a2[0]).at[RW_AAA_LORA:, rw_w:].set(a2[1]).astype(BF16)
    vec = lambda t: t.reshape(1, -1).astype(F32)
    full = lambda shape: pl.BlockSpec(shape, lambda b, i: (0,) * len(shape))
    tok = lambda n: pl.BlockSpec((1, tm, n * rw_w), lambda b, i: (b, i, 0))
    tok2 = lambda n: pl.BlockSpec((2, 1, tm, n * rw_w), lambda b, i: (0, b, i, 0))
    s1 = lambda n, t: jax.ShapeDtypeStruct((bsz, T, n * rw_w), t)
    s2 = lambda n, t: jax.ShapeDtypeStruct((2, bsz, T, n * rw_w), t)
    return pl.pallas_call(
        functools.partial(_rwkv_feat_kernel, rw_w=rw_w),
        grid=(bsz, T // tm),
        in_specs=_halo_specs(tm, T, cols) + [
            full((1, cols)), full((1, 2 * rw_w)), full(w2_bd.shape), full((1, 2 * rw_w)),
            full(a2_bd.shape), full(g2.shape), full((1, rw_w)), full((1, rw_w)), full((1, rw_w)),
            full((rw_w, rw_w))],
        out_specs=[tok(3), tok(1), tok(1), tok2(1), tok2(2)],
        out_shape=[s1(3, BF16), s1(1, F32), s1(1, F32), s2(1, F32), s2(2, BF16)],
        compiler_params=_params("parallel", "parallel"),
        name="rwkv_features",
    )(p, p, p, vec(mu), vec(w0), w2_bd, vec(a0), a2_bd, g2.astype(BF16), vec(k_k), vec(k_a), vec(r_k),
      _head_indicator(rw_w, RW_HEAD))


def _rwkv_scan_kernel(rvkc, lwc, kbc, rvkl, lwl, kbl, y_ref, s_ref, *, n_ctx, bb):
    d = pl.program_id(1)
    c = pl.program_id(2)
    C = CHUNK
    n_pairs = s_ref.shape[0] // bb

    @pl.when(c == 0)
    def _():
        s_ref[...] = jnp.zeros_like(s_ref)

    is_ctx = c < n_ctx
    pick = lambda a, b: jnp.where(is_ctx, a, b)
    rev = d == 1
    ti = lax.broadcasted_iota(jnp.int32, (C, C), 0)
    tj = lax.broadcasted_iota(jnp.int32, (C, C), 1)
    tri = jnp.where(jnp.where(rev, tj - ti, ti - tj) >= 0, 1.0, 0.0).astype(BF16)

    def features(bi):
        w = lwc.shape[-1]
        rvk = pick(rvkc[bi], rvkl[bi])
        r, v, kk = rvk[:, :w], rvk[:, w:2 * w], rvk[:, 2 * w:]
        lw = pick(lwc[0, bi], lwl[0, bi])
        kb = pick(kbc[0, bi], kbl[0, bi])
        kd, bd = kb[:, :w], kb[:, w:]
        linc = _mm_exact_lhs(tri, lw)
        ltot = jnp.sum(lw, axis=0, keepdims=True)
        g_inv = jnp.exp(-linc)
        g_end = jnp.exp(ltot - linc)
        return (kk * jnp.exp(linc - lw), r * jnp.exp(linc), kd * g_inv, bd * g_inv, kd * g_end, bd * g_end, v,
                jnp.exp(ltot))

    feats = [features(bi) for bi in range(bb)]

    n = 2 * C
    rid = lax.broadcasted_iota(jnp.int32, (n, n), 0)
    cid = lax.broadcasted_iota(jnp.int32, (n, n), 1)
    same = (rid >> 6) == (cid >> 6)
    diff = jnp.where(rev, cid - rid, rid - cid)
    strict = lambda m: jnp.where(same, jnp.where(diff > 0, m, 0.0), 0.0)
    incl = lambda m: jnp.where(same, jnp.where(diff >= 0, m, 0.0), 0.0)
    first = lax.broadcasted_iota(jnp.int32, (1, LANE), 1) < RW_HEAD

    def stack(x):
        return jnp.concatenate([jnp.where(first, x, 0.0), jnp.where(first, 0.0, x)], axis=0)

    prob = [(bi, slice(p * LANE, (p + 1) * LANE)) for bi in range(bb) for p in range(n_pairs)]
    qs = range(len(prob))
    stk = [[stack(t[:, sl]) for t in feats[bi][:7]] for bi, sl in prob]
    kks, rs, kis, bis, kes, bes, vs = ([stk[q][i] for q in qs] for i in range(7))
    m = [_mm_nt(jnp.concatenate([kks[q], rs[q]], axis=0), jnp.concatenate([kis[q], bis[q]], axis=0)) for q in qs]
    a_k = [strict(t[:n, :n]) for t in m]
    a_b = [strict(t[:n, n:]) for t in m]
    b_k = [incl(t[n:, :n]) for t in m]
    b_b = [incl(t[n:, n:]) for t in m]
    t_inv = _unit_tri_inverse(a_b, rid, cid, C)
    abv = [_mm(jnp.concatenate([a_k[q], b_k[q]], axis=0), vs[q]) for q in qs]
    wu = [_mm(t_inv[q], jnp.concatenate([kks[q], abv[q][:n]], axis=1)) for q in qs]
    w_s = [t[:, :LANE] for t in wu]
    u0_s = [t[:, LANE:] for t in wu]
    vt = [t.T for t in vs]
    u0_t = [t.T for t in u0_s]
    st = [s_ref[q] for q in qs]
    wr = [_mm_nt(jnp.concatenate([w_s[q], rs[q]], axis=0), st[q]) for q in qs]
    u_s = [wr[q][:n] + u0_s[q] for q in qs]
    u_t = [_mm_nt(st[q], w_s[q]) + u0_t[q] for q in qs]
    y_s = [wr[q][n:] + abv[q][n:] - _mm(b_b[q], u_s[q]) for q in qs]
    for q, (bi, sl) in enumerate(prob):
        s_ref[q] = st[q] * feats[bi][7][:, sl] + _mm(jnp.concatenate([vt[q], -u_t[q]], axis=1),
                                                    jnp.concatenate([kes[q], bes[q]], axis=0))
        y_ref[0, bi, :, sl] = (y_s[q][:C] + y_s[q][C:]).astype(BF16)


def rwkv_scan(f_ctx, f_lat):
    rvkc, lwc, kbc = f_ctx
    rvkl, lwl, kbl = f_lat
    bsz, Lc, w = lwc.shape[1:]
    L = lwl.shape[2]
    C = CHUNK
    assert Lc % C == 0 and L % C == 0 and w % LANE == 0
    nc, nl = Lc // C, L // C
    bb = _row_tile(bsz, RW_SCAN_BATCH)
    ctx_idx, lat_idx = _scan_index_maps(nc, nl)
    shared = lambda idx: pl.BlockSpec((bb, C, 3 * w), lambda b, d, c: (b, idx(d, c), 0))
    directed = lambda idx, n: pl.BlockSpec((1, bb, C, n * w), lambda b, d, c: (d, b, idx(d, c), 0))
    return pl.pallas_call(
        functools.partial(_rwkv_scan_kernel, n_ctx=nc, bb=bb),
        grid=(bsz // bb, 2, nc + nl),
        in_specs=[shared(ctx_idx), directed(ctx_idx, 1), directed(ctx_idx, 2),
                  shared(lat_idx), directed(lat_idx, 1), directed(lat_idx, 2)],
        out_specs=directed(lat_idx, 1),
        out_shape=jax.ShapeDtypeStruct((2, bsz, L, w), BF16),
        scratch_shapes=[pltpu.VMEM((bb * (w // LANE), LANE, LANE), F32)],
        compiler_params=_params("parallel", "parallel", "arbitrary"),
        name="rwkv_scan",
    )(rvkc, lwc, kbc, rvkl, lwl, kbl)


def _rwkv_head_out(y, bonus, g, gn_g, gn_b, e):
    inv_n = 1.0 / RW_HEAD
    mu = _head_sum(y, e) * inv_n
    yc = y - mu
    var = _head_sum(yc * yc, e) * inv_n
    return (yc * lax.rsqrt(var + RW_GN_EPS) * gn_g + gn_b + bonus) * g


def rwkv7_scan_lat(p_lat, p_ctx, mu, w0, w2, a0, a2, g2, k_k, k_a, r_k):
    args = (mu, w0, w2, a0, a2, g2, k_k, k_a, r_k)
    rvk_l, g_l, bonus_l, lw_l, kb_l = rwkv_features(p_lat, *args)
    rvk_c, _, _, lw_c, kb_c = rwkv_features(p_ctx, *args)
    return rwkv_scan((rvk_c, lw_c, kb_c), (rvk_l, lw_l, kb_l)), bonus_l, g_l


def _gdn_feat_kernel(p_ref, hp_ref, hn_ref, g_ref, cw_ref, al_ref, dt_ref, qkv_o, gb_o, *, dn_w):
    x = p_ref[0]
    x_prev, x_next = _neighbours(x, *_halo_rows(hp_ref, hn_ref))
    cw = cw_ref[...]
    z = x_prev * cw[0:1] + x * cw[1:2] + x_next * cw[2:3]
    z = z * _sigmoid(z)
    for h in range(dn_w // DN_HEAD):
        sq = slice(h * DN_HEAD, (h + 1) * DN_HEAD)
        sk = slice(dn_w + h * DN_HEAD, dn_w + (h + 1) * DN_HEAD)
        q = z[:, sq]
        k = z[:, sk]
        qn = q * (lax.rsqrt(jnp.sum(q * q, -1, keepdims=True) + 1e-6) * (DN_HEAD ** -0.5))
        qkv_o[0, :, sq] = qn.astype(BF16)
        qkv_o[0, :, sk] = (k * lax.rsqrt(jnp.sum(k * k, -1, keepdims=True) + 1e-6)).astype(BF16)
    qkv_o[0, :, 2 * dn_w:] = z[:, 2 * dn_w:].astype(BF16)
    g = g_ref[0]
    gb_o[0, :, :LANE] = -jnp.exp(al_ref[...]) * _softplus(g[:, :LANE] + dt_ref[...])
    gb_o[0, :, LANE:] = _sigmoid(g[:, LANE:])


def gdn_features(p, conv_w, A_log, dt_bias):
    bsz, T, cols = p.shape
    dn_w = conv_w.shape[1] // 3
    tm = _row_tile(T, 256)
    c3 = 3 * dn_w
    assert (4 * dn_w) % (2 * LANE) == 0
    gcol = (4 * dn_w) // (2 * LANE)
    pad = lambda t: jnp.pad(t.reshape(1, -1).astype(F32), ((0, 0), (0, LANE - t.size)))
    tok = lambda n: pl.BlockSpec((1, tm, n), lambda b, i: (b, i, 0))
    row = pl.BlockSpec((1, LANE), lambda b, i: (0, 0))
    return pl.pallas_call(
        functools.partial(_gdn_feat_kernel, dn_w=dn_w),
        grid=(bsz, T // tm),
        in_specs=_halo_specs(tm, T, c3) + [pl.BlockSpec((1, tm, 2 * LANE), lambda b, i: (b, i, gcol)),
                                          pl.BlockSpec((8, c3), lambda b, i: (0, 0)), row, row],
        out_specs=[tok(c3), tok(2 * LANE)],
        out_shape=[jax.ShapeDtypeStruct((bsz, T, c3), BF16), jax.ShapeDtypeStruct((bsz, T, 2 * LANE), F32)],
        compiler_params=_params("parallel", "parallel"),
        name="gdn_features",
    )(p, p, p, p, jnp.pad(conv_w, ((0, 8 - conv_w.shape[0]), (0, 0))), pad(A_log), pad(dt_bias))


def _gdn_scan_kernel(qkvc, gbc, qkvl, gbl, o_ref, s_ref, *, n_ctx, bb):
    d = pl.program_id(1)
    c = pl.program_id(2)
    C = CHUNK
    heads = s_ref.shape[0] // bb

    @pl.when(c == 0)
    def _():
        s_ref[...] = jnp.zeros_like(s_ref)

    is_ctx = c < n_ctx
    pick = lambda a, b, bi: jnp.where(is_ctx, a[bi], b[bi])
    rev = d == 1

    ti = lax.broadcasted_iota(jnp.int32, (C, C), 0)
    tj = lax.broadcasted_iota(jnp.int32, (C, C), 1)
    tri = jnp.where(jnp.where(rev, tj - ti, ti - tj) >= 0, 1.0, 0.0).astype(BF16)
    lane = lax.broadcasted_iota(jnp.int32, (1, LANE), 1)

    def col(x, h):
        return jnp.sum(jnp.where(lane == d * heads + h, x, 0.0), axis=1, keepdims=True)

    n = 2 * C
    rid = lax.broadcasted_iota(jnp.int32, (n, n), 0)
    cid = lax.broadcasted_iota(jnp.int32, (n, n), 1)
    same = (rid >> 6) == (cid >> 6)
    diff = jnp.where(rev, cid - rid, rid - cid)
    keep_incl = jnp.where(same, jnp.where(diff >= 0, 1.0, 0.0), 0.0)
    eye = jnp.where(rid == cid, 1.0, 0.0)

    hs = lambda h: slice(h * DN_HEAD, (h + 1) * DN_HEAD)

    prob = [(bi, p) for bi in range(bb) for p in range(heads // 2)]
    qs = range(len(prob))
    k_s, q_s, v_s, g_col, b_col, ge_col, g_end = [], [], [], [], [], [], []
    for bi in range(bb):
        w = heads * DN_HEAD
        qkv = pick(qkvc, qkvl, bi)
        q, k, v = qkv[:, :w], qkv[:, w:2 * w], qkv[:, 2 * w:]
        gb = pick(gbc, gbl, bi)
        gl, be = gb[:, :LANE], gb[:, LANE:]
        g_cum = _mm_exact_lhs(tri, gl)
        g_tot = jnp.sum(gl, axis=0, keepdims=True)
        for p in range(heads // 2):
            h0, h1 = 2 * p, 2 * p + 1
            stack = lambda x: jnp.concatenate([x[:, hs(h0)], x[:, hs(h1)]], axis=0)
            k_s.append(stack(k))
            q_s.append(stack(q))
            v_s.append(stack(v))
            g_col.append(jnp.concatenate([col(g_cum, h0), col(g_cum, h1)], axis=0))
            b_col.append(jnp.concatenate([col(be, h0), col(be, h1)], axis=0))
            gt = [col(g_tot, h0), col(g_tot, h1)]
            ge_col.append(jnp.concatenate([jnp.broadcast_to(t, (C, 1)) for t in gt], axis=0))
            g_end.append([jnp.exp(t) for t in gt])
    g_row = [jnp.broadcast_to(g_col[i], (n, n)).T for i in qs]
    decay = [keep_incl * jnp.exp(keep_incl * (g_col[i] - g_row[i])) for i in qs]
    kq = [_mm_nt(jnp.concatenate([k_s[i], q_s[i]], axis=0), k_s[i]) for i in qs]
    a = [(1.0 - eye) * b_col[i] * kq[i][:n] * decay[i] for i in qs]
    qk = [kq[i][n:] * decay[i] for i in qs]
    t_inv = _unit_tri_inverse(a, rid, cid, C)
    eg = [jnp.exp(g_col[i]) for i in qs]
    uw = [_mm(t_inv[i], jnp.concatenate([b_col[i] * v_s[i], (b_col[i] * eg[i]) * k_s[i]], axis=1)) for i in qs]
    u0 = [t[:, :DN_HEAD] for t in uw]
    w = [t[:, DN_HEAD:] for t in uw]
    q_dec = [q_s[i] * eg[i] for i in qs]
    k_dec_t = [_mm_nt(eye, k_s[i] * jnp.exp(ge_col[i] - g_col[i])) for i in qs]
    head_cols = lax.broadcasted_iota(jnp.int32, (1, n), 1) >> 6

    st = [[s_ref[2 * i + t] for t in range(2)] for i in qs]
    ws = [[_mm(jnp.concatenate([w[i][t * C:(t + 1) * C], q_dec[i][t * C:(t + 1) * C]], axis=0), st[i][t])
           for t in range(2)] for i in qs]
    u = [jnp.concatenate([u0[i][:C] - ws[i][0][:C], u0[i][C:] - ws[i][1][:C]], axis=0) for i in qs]
    o = [jnp.concatenate([ws[i][0][C:], ws[i][1][C:]], axis=0) + _mm(qk[i], u[i]) for i in qs]
    for i, (bi, p) in enumerate(prob):
        for t in range(2):
            kd = jnp.where(head_cols == t, k_dec_t[i], 0.0)
            s_ref[2 * i + t] = g_end[i][t] * st[i][t] + _mm(kd, u[i])
            o_ref[0, bi, :, hs(2 * p + t)] = o[i][t * C:(t + 1) * C].astype(BF16)


def gdn_scan(f_ctx, f_lat):
    qkvc, gbc = f_ctx
    qkvl, gbl = f_lat
    bsz, Lc, w3 = qkvc.shape
    L = qkvl.shape[1]
    C = CHUNK
    w = w3 // 3
    heads = w // DN_HEAD
    assert Lc % C == 0 and L % C == 0 and heads % 2 == 0
    nc, nl = Lc // C, L // C
    bb = _row_tile(bsz, DN_SCAN_BATCH)
    ctx_idx, lat_idx = _scan_index_maps(nc, nl)
    spec = lambda idx, n: pl.BlockSpec((bb, C, n), lambda b, d, c: (b, idx(d, c), 0))
    return pl.pallas_call(
        functools.partial(_gdn_scan_kernel, n_ctx=nc, bb=bb),
        grid=(bsz // bb, 2, nc + nl),
        in_specs=[spec(ctx_idx, w3), spec(ctx_idx, 2 * LANE), spec(lat_idx, w3), spec(lat_idx, 2 * LANE)],
        out_specs=pl.BlockSpec((1, bb, C, w), lambda b, d, c: (d, b, lat_idx(d, c), 0)),
        out_shape=jax.ShapeDtypeStruct((2, bsz, L, w), BF16),
        scratch_shapes=[pltpu.VMEM((bb * heads, DN_HEAD, DN_HEAD), F32)],
        compiler_params=_params("parallel", "parallel", "arbitrary"),
        name="gdn_scan",
    )(qkvc, gbc, qkvl, gbl)


def _gdn_head_out(o, z, norm_g):
    outs = []
    for h in range(o.shape[1] // DN_HEAD):
        sl = slice(h * DN_HEAD, (h + 1) * DN_HEAD)
        oh = o[:, sl]
        zh = z[:, sl]
        outs.append(oh * lax.rsqrt(jnp.mean(oh * oh, -1, keepdims=True) + 1e-6) * norm_g * (zh * _sigmoid(zh)))
    return jnp.concatenate(outs, axis=1)


def gdn_scan_lat(p_lat, p_ctx, conv_w, A_log, dt_bias):
    f_lat = gdn_features(p_lat, conv_w, A_log, dt_bias)
    f_ctx = gdn_features(p_ctx, conv_w, A_log, dt_bias)
    return gdn_scan(f_ctx, f_lat)


def _odd_out_ln_kernel(y_ref, bonus_ref, g_ref, gg_ref, gb_ref, e_ref, o_ref, z_ref, ng_ref, wa_ref, wb_ref,
                       x_ref, gate_ref, lg_ref, lb_ref, out_ref):
    both = lambda ref: ref[0, 0].astype(F32) + ref[1, 0].astype(F32)
    y_c = _rwkv_head_out(both(y_ref), bonus_ref[0], g_ref[0], gg_ref[...], gb_ref[...], e_ref[...])
    y_d = _gdn_head_out(both(o_ref), z_ref[0], ng_ref[...])
    y = _mm(y_c, wa_ref[...]) + _mm(y_d, wb_ref[...])
    z = DEEPNORM_ALPHA * x_ref[0] + gate_ref[0] * y
    out_ref[0] = _layer_norm_rows(z, lg_ref[...], lb_ref[...])


def odd_out_proj_ln(y, bonus, g, gn_g, gn_b, o, p_dn, norm_g, w, x, gate, ln_g, ln_b):
    bsz, L, d = x.shape
    rw_w, dn_w = y.shape[-1], o.shape[-1]
    tm = _row_tile(L, 256)
    z_block = 3
    two = lambda w_: pl.BlockSpec((2, 1, tm, w_), lambda b, i: (0, b, i, 0))
    tok = lambda w_: pl.BlockSpec((1, tm, w_), lambda b, i: (b, i, 0))
    row = lambda w_: pl.BlockSpec((1, w_), lambda b, i: (0, 0))
    full = lambda a: pl.BlockSpec(a.shape, lambda b, i: (0, 0))
    wa, wb = w[:rw_w], w[rw_w:]
    e = _head_indicator(rw_w, RW_HEAD)
    return pl.pallas_call(
        _odd_out_ln_kernel,
        grid=(bsz, L // tm),
        in_specs=[two(rw_w), tok(rw_w), tok(rw_w), row(rw_w), row(rw_w), full(e), two(dn_w),
                  pl.BlockSpec((1, tm, dn_w), lambda b, i: (b, i, z_block)), row(DN_HEAD), full(wa), full(wb),
                  tok(d), pl.BlockSpec((1, 1, d), lambda b, i: (b, 0, 0)), row(d), row(d)],
        out_specs=tok(d),
        out_shape=jax.ShapeDtypeStruct((bsz, L, d), F32),
        compiler_params=_params("parallel", "parallel"),
        name="odd_out_proj_ln",
    )(y, bonus, g, gn_g.reshape(1, rw_w), gn_b.reshape(1, rw_w), e, o, p_dn, norm_g.reshape(1, DN_HEAD), wa, wb,
      x, gate, ln_g.reshape(1, d), ln_b.reshape(1, d))


def _gdn_in_weight(w_dn, dn_w, heads):
    padw = ((0, 0), (0, LANE - 2 * heads))
    g = w_dn[:, 4 * dn_w:]
    return jnp.concatenate([w_dn[:, :4 * dn_w], jnp.pad(g[:, :2 * heads], padw), jnp.pad(g[:, 2 * heads:], padw)], 1)


def kernel(x, c, ctx, c_ctx, mod_w, mod_b, ln_g, ln_b, mlp_w1, mlp_w2, e_w_in, e_w_out, hy_conv, hy_ffn_w1,
           hy_ffn_b1, hy_ffn_w2, hy_ffn_b2, hy_sin_freq, hy_ffn_w3, hy_decay, hy_bias, attn_sink, o_w_in,
           o_w_out, rw_mu, rw_w0, rw_w2, rw_a0, rw_a2, rw_g2, rw_kk, rw_ka, rw_rk, rw_lnx_g, rw_lnx_b,
           dn_conv, dn_A_log, dn_dt_bias, dn_norm_g):
    bsz, L, d = x.shape
    assert mod_w.shape[0] == DEPTH == 2
    cc = jnp.concatenate([c, c_ctx[None, :]], axis=0)
    cc = jnp.pad(cc, ((0, (-cc.shape[0]) % 8), (0, 0)))

    def modulation(i):
        m = mod_vectors(cc, mod_w[i], mod_b[i])
        m_lat = m[:bsz].reshape(bsz, 1, N_MOD, d)
        m_ctx = jnp.broadcast_to(m[bsz].reshape(1, 1, N_MOD, d), (bsz, 1, N_MOD, d))
        return [m_lat[:, :, t] for t in range(N_MOD)], [m_ctx[:, :, t] for t in range(N_MOD)]

    (sh1, sc1, g1, sh2, sc2, g2), (csh1, csc1, cg1, csh2, csc2, cg2) = modulation(0)
    hy_cols = hy_conv.shape[2]
    w_in = e_w_in[0].astype(BF16)
    ws = [w_in[:, :hy_cols], w_in[:, hy_cols:]]
    w_out = e_w_out[0].astype(BF16)
    w1, w2 = mlp_w1[0].astype(BF16), mlp_w2[0].astype(BF16)
    hy = (hy_conv[0], hy_ffn_w1[0], hy_ffn_b1[0], hy_ffn_w2[0], hy_ffn_b2[0], hy_sin_freq[0], hy_ffn_w3[0],
          hy_decay[0], hy_bias[0])
    p_hy, p_qkv = mod_linear(x, sc1, sh1, ws)
    pc_hy, pc_qkv = mod_linear(ctx, csc1, csh1, ws)
    y_a = hyena_mixer(p_hy, *hy)
    y_b = windowed_attention(p_qkv, pc_qkv, attn_sink[0])
    yc_a = hyena_mixer(pc_hy, *hy)
    yc_b = context_attention(pc_qkv, attn_sink[0])
    x = out_mlp_ln(y_a, y_b, w_out, x, g1, ln_g[0, 0], ln_b[0, 0], sc2, sh2, g2, w1, w2, ln_g[0, 1], ln_b[0, 1])
    ctx = out_mlp_ln(yc_a, yc_b, w_out, ctx, cg1, ln_g[0, 0], ln_b[0, 0], csc2, csh2, cg2, w1, w2, ln_g[0, 1],
                     ln_b[0, 1])

    (sh1, sc1, g1, sh2, sc2, g2), (csh1, csc1, _, _, _, _) = modulation(1)
    rw_cols = rw_mu.shape[1]
    dn_w = dn_conv.shape[2] // 3
    w_in = o_w_in[0]
    ws = [w_in[:, :rw_cols].astype(BF16), _gdn_in_weight(w_in[:, rw_cols:], dn_w, dn_w // DN_HEAD).astype(BF16)]
    w_out = o_w_out[0].astype(BF16)
    w1, w2 = mlp_w1[1].astype(BF16), mlp_w2[1].astype(BF16)
    p_rw, p_dn = mod_linear(x, sc1, sh1, ws)
    pc_rw, pc_dn = mod_linear(ctx, csc1, csh1, ws)
    y_rw, bonus, g_rw = rwkv7_scan_lat(p_rw, pc_rw, rw_mu[0], rw_w0[0], rw_w2[0], rw_a0[0], rw_a2[0], rw_g2[0],
                                       rw_kk[0], rw_ka[0], rw_rk[0])
    o_dn = gdn_scan_lat(p_dn, pc_dn, dn_conv[0], dn_A_log[0], dn_dt_bias[0])
    x = odd_out_proj_ln(y_rw, bonus, g_rw, rw_lnx_g[0], rw_lnx_b[0], o_dn, p_dn, dn_norm_g[0], w_out, x, g1,
                        ln_g[1, 0], ln_b[1, 0])
    x = mlp_ln(x, sc2, sh2, g2, w1, w2, ln_g[1, 1], ln_b[1, 1])
    return x
```
